```python
import math
import jax
import jax.numpy as jnp
from jax import lax
import numpy as np

D_MODEL = 1024
BATCH = 16
SEQ = 2048
DEPTH = 1

GDN_HEADS = 4
GDN_DK = 128
GDN_DV = 128
GDN_CONV = 4
RET_HEADS = 4
RET_DK = 64
RET_DV = 128
CHUNK = 64
GDN_QK = GDN_HEADS * GDN_DK
GDN_VW = GDN_HEADS * GDN_DV
RET_QK = RET_HEADS * RET_DK
RET_VW = RET_HEADS * RET_DV
MIX_WIDTH = GDN_VW + RET_VW
IN_SPLITS = (GDN_QK, GDN_QK, GDN_VW, GDN_VW, GDN_HEADS, GDN_HEADS, RET_QK, RET_QK, RET_VW, RET_VW)
IN_WIDTH = 2 * GDN_QK + 2 * GDN_VW + 2 * GDN_HEADS + 2 * RET_QK + 2 * RET_VW
ROPE_BASE = 10000.0
N_EXPERTS = 256
TOP_K = 8
N_GROUPS = 8
TOPK_GROUPS = 4
D_EXPERT = 256
D_SHARED = 256
ROUTED_SCALE = 2.5
MOE_BLOCK = 128
DN_ALPHA = (2.0 * DEPTH) ** 0.25
DN_BETA = (8.0 * DEPTH) ** -0.25
LN_EPS = 1e-5
NORM_EPS = 1e-6

kernel_name = "hybrid_gdn_retention_moe_deepnorm"


def _layer_norm(x, g, b):
    xf = x.astype(jnp.float32)
    mu = jnp.mean(xf, -1, keepdims=True)
    var = jnp.mean(jnp.square(xf - mu), -1, keepdims=True)
    return ((xf - mu) * lax.rsqrt(var + LN_EPS) * g.astype(jnp.float32) + b.astype(jnp.float32)).astype(x.dtype)


def _l2norm(u):
    return u * lax.rsqrt(jnp.sum(jnp.square(u), -1, keepdims=True) + NORM_EPS)


def _heads(u, n_heads):
    return u.reshape(u.shape[0], u.shape[1], n_heads, -1)


def _to_chunks(u):
    b, s, h = u.shape[:3]
    u = jnp.moveaxis(u, 2, 1)
    return u.reshape((b, h, s // CHUNK, CHUNK) + u.shape[3:])


def _from_chunks(u):
    b, h, n, c, d = u.shape
    return jnp.moveaxis(u.reshape(b, h, n * c, d), 1, 2)


def _causal_depthwise_conv(u, w):
    k, ch = w.shape
    return lax.conv_general_dilated(u, w[:, None, :], window_strides=(1,), padding=[(k - 1, 0)],
                                    dimension_numbers=("NWC", "WIO", "NWC"), feature_group_count=ch)


def _gated_delta_chunked(q, k, v, log_alpha, beta):
    c = q.shape[-2]
    g = jnp.cumsum(log_alpha, axis=-1)
    causal = jnp.tril(jnp.ones((c, c), dtype=bool))
    strict = jnp.tril(jnp.ones((c, c), dtype=bool), -1)
    decay = jnp.exp(jnp.where(causal, g[..., :, None] - g[..., None, :], -jnp.inf))
    k_beta = k * beta[..., None]
    lower = jnp.where(strict, jnp.einsum("bhncd,bhnsd->bhncs", k_beta, k) * decay, 0.0)
    eye = jnp.broadcast_to(jnp.eye(c, dtype=q.dtype), lower.shape)
    t_inv = lax.linalg.triangular_solve(lower, eye, left_side=True, lower=True, unit_diagonal=True)
    w_val = jnp.einsum("bhncs,bhnse->bhnce", t_inv, v * beta[..., None])
    k_cum = jnp.einsum("bhncs,bhnsd->bhncd", t_inv, k_beta * jnp.exp(g)[..., None])
    attn_in = jnp.where(causal, jnp.einsum("bhncd,bhnsd->bhncs", q, k) * decay, 0.0)
    q_dec = q * jnp.exp(g)[..., None]
    k_dec = k * jnp.exp(g[..., -1:] - g)[..., None]
    g_last = jnp.exp(g[..., -1])

    def step(state, inp):
        wv, kc, qd, kd, a_in, gl = inp
        v_new = wv - jnp.einsum("bhcd,bhde->bhce", kc, state)
        o = jnp.einsum("bhcd,bhde->bhce", qd, state) + jnp.einsum("bhcs,bhse->bhce", a_in, v_new)
        state = state * gl[..., None, None] + jnp.einsum("bhcd,bhce->bhde", kd, v_new)
        return state, o

    xs = tuple(jnp.moveaxis(t, 2, 0) for t in (w_val, k_cum, q_dec, k_dec, attn_in, g_last))
    b, h = q.shape[:2]
    state0 = jnp.zeros((b, h, q.shape[-1], v.shape[-1]), q.dtype)
    _, o = lax.scan(step, state0, xs)
    return jnp.moveaxis(o, 0, 2)


def _retention_log_gamma():
    return jnp.log(1.0 - 2.0 ** (-5.0 - jnp.arange(RET_HEADS, dtype=jnp.float32)))


def _rotary_tables(s):
    inv = 1.0 / (ROPE_BASE ** jnp.linspace(0.0, 1.0, RET_DK // 2, dtype=jnp.float32))
    ang = jnp.arange(s, dtype=jnp.float32)[:, None] * inv[None, :]
    return jnp.cos(ang), jnp.sin(ang)


def _rotate_pairs(u, cos, sin):
    u1, u2 = u[..., 0::2], u[..., 1::2]
    c, s = cos[:, None, :], sin[:, None, :]
    return jnp.stack([u1 * c - u2 * s, u1 * s + u2 * c], axis=-1).reshape(u.shape)


def _retention_chunked(q, k, v):
    c = q.shape[-2]
    lg = _retention_log_gamma()[:, None, None]
    pos = jnp.arange(c, dtype=jnp.float32)
    diff = pos[:, None] - pos[None, :]
    dmask = jnp.where(diff >= 0, jnp.exp(lg * jnp.maximum(diff, 0.0)), 0.0)
    inner = jnp.einsum("bhncd,bhnsd->bhncs", q, k) * dmask[None, :, None]
    o_inner = jnp.einsum("bhncs,bhnse->bhnce", inner, v)
    q_dec = q * jnp.exp(lg * (pos[:, None] + 1.0))[None, :, None]
    k_dec = k * jnp.exp(lg * (c - 1.0 - pos[:, None]))[None, :, None]
    g_chunk = jnp.exp(lg[:, 0, 0] * c)

    def step(state, inp):
        qd, kd, vc = inp
        o = jnp.einsum("bhcd,bhde->bhce", qd, state)
        state = state * g_chunk[None, :, None, None] + jnp.einsum("bhcd,bhce->bhde", kd, vc)
        return state, o

    xs = tuple(jnp.moveaxis(t, 2, 0) for t in (q_dec, k_dec, v))
    b, h = q.shape[:2]
    state0 = jnp.zeros((b, h, q.shape[-1], v.shape[-1]), q.dtype)
    _, o_cross = lax.scan(step, state0, xs)
    return o_inner + jnp.moveaxis(o_cross, 0, 2)


def _hybrid_mixer(x, w_in, conv_w, a_log, dt_bias, norm_w, w_out):
    f32 = jnp.float32
    b, s, _ = x.shape
    proj = (x @ w_in).astype(f32)
    qa, ka, va, za, ba, aa, qb, kb, vb, gb = jnp.split(proj, np.cumsum(IN_SPLITS)[:-1].tolist(), axis=-1)
    qkv = jax.nn.silu(_causal_depthwise_conv(jnp.concatenate([qa, ka, va], -1), conv_w.astype(f32)))
    qa, ka, va = jnp.split(qkv, [GDN_QK, 2 * GDN_QK], axis=-1)
    qa = _l2norm(_heads(qa, GDN_HEADS)) * (GDN_DK ** -0.5)
    ka = _l2norm(_heads(ka, GDN_HEADS))
    va = _heads(va, GDN_HEADS)
    beta = jax.nn.sigmoid(ba)
    log_alpha = -jnp.exp(a_log.astype(f32)) * jax.nn.softplus(aa + dt_bias.astype(f32))
    oa = _gated_delta_chunked(_to_chunks(qa), _to_chunks(ka), _to_chunks(va),
                              _to_chunks(log_alpha), _to_chunks(beta))
    oa = _from_chunks(oa)
    oa = oa * lax.rsqrt(jnp.mean(jnp.square(oa), -1, keepdims=True) + NORM_EPS) * norm_w.astype(f32)
    oa = oa * jax.nn.silu(_heads(za, GDN_HEADS))
    cos, sin = _rotary_tables(s)
    qb = _rotate_pairs(_heads(qb, RET_HEADS), cos, sin)
    kb = _rotate_pairs(_heads(kb, RET_HEADS), cos, sin) * (RET_DK ** -0.5)
    ob = _from_chunks(_retention_chunked(_to_chunks(qb), _to_chunks(kb), _to_chunks(_heads(vb, RET_HEADS))))
    mu = jnp.mean(ob, -1, keepdims=True)
    ob = (ob - mu) * lax.rsqrt(jnp.mean(jnp.square(ob - mu), -1, keepdims=True) + NORM_EPS)
    ob = ob * jax.nn.silu(_heads(gb, RET_HEADS))
    mix = jnp.concatenate([oa.reshape(b, s, GDN_VW), ob.reshape(b, s, RET_VW)], axis=-1).astype(x.dtype)
    return mix @ w_out


def _moe(h, w_router, router_bias, w_gate_e, w_up_e, w_down_e, w_gate_s, w_up_s, w_down_s):
    b, s, d = h.shape
    t = h.reshape(-1, d)
    n_tok = t.shape[0]
    scores = jax.nn.sigmoid(t.astype(jnp.float32) @ w_router.astype(jnp.float32))
    choice = scores + router_bias.astype(jnp.float32)
    grp_score = jnp.sum(lax.top_k(choice.reshape(n_tok, N_GROUPS, -1), 2)[0], -1)
    _, top_g = lax.top_k(grp_score, TOPK_GROUPS)
    gmask = jnp.sum(jax.nn.one_hot(top_g, N_GROUPS, dtype=jnp.float32), axis=1) > 0
    masked = jnp.where(jnp.repeat(gmask, N_EXPERTS // N_GROUPS, axis=1), choice, -jnp.inf)
    _, idx = lax.top_k(masked, TOP_K)
    wts = jnp.take_along_axis(scores, idx, axis=1)
    wts = wts / jnp.sum(wts, -1, keepdims=True) * ROUTED_SCALE
    n_assign = n_tok * TOP_K
    flat_e = idx.reshape(-1)
    flat_tok = jnp.repeat(jnp.arange(n_tok, dtype=jnp.int32), TOP_K)
    flat_w = wts.reshape(-1)
    order = jnp.argsort(flat_e, stable=True)
    se, st, sw = flat_e[order], flat_tok[order], flat_w[order]
    counts = jax.ops.segment_sum(jnp.ones((n_assign,), jnp.int32), flat_e, num_segments=N_EXPERTS)
    starts = jnp.cumsum(counts) - counts
    padded = (counts + MOE_BLOCK - 1) // MOE_BLOCK * MOE_BLOCK
    pends = jnp.cumsum(padded)
    pstarts = pends - padded
    dest = pstarts[se] + (jnp.arange(n_assign, dtype=jnp.int32) - starts[se])
    n_blk = n_assign // MOE_BLOCK + N_EXPERTS
    n_slot = n_blk * MOE_BLOCK
    slot_tok = jnp.zeros((n_slot,), jnp.int32).at[dest].set(st)
    slot_w = jnp.zeros((n_slot,), jnp.float32).at[dest].set(sw)
    blk_e = jnp.minimum(jnp.searchsorted(pends, jnp.arange(n_blk, dtype=jnp.int32) * MOE_BLOCK, side="right"),
                        N_EXPERTS - 1)

    def accumulate(y, blk):
        toks, bw, e = blk
        xb = t[toks]
        hid = jax.nn.silu(xb @ w_gate_e[e]) * (xb @ w_up_e[e])
        return y.at[toks].add(((hid @ w_down_e[e]) * bw[:, None]).astype(y.dtype)), None

    y, _ = lax.scan(accumulate, jnp.zeros_like(t),
                    (slot_tok.reshape(n_blk, MOE_BLOCK), slot_w.reshape(n_blk, MOE_BLOCK), blk_e))
    shared = (jax.nn.silu(t @ w_gate_s) * (t @ w_up_s)) @ w_down_s
    return (y + shared).reshape(b, s, d)


def setup_inputs(seed: int = 0) -> dict:
    key = jax.random.key(seed)
    ks = jax.random.split(key, 20)
    f32 = jnp.float32

    def nrm(k, shape, scale):
        return jax.random.normal(k, shape, f32) * scale

    L = DEPTH
    dt = jnp.exp(jax.random.uniform(ks[4], (L, GDN_HEADS), f32, math.log(1e-3), math.log(1e-1)))
    return {
        "x": nrm(ks[0], (BATCH, SEQ, D_MODEL), 1.0),
        "w_in": nrm(ks[1], (L, D_MODEL, IN_WIDTH), D_MODEL ** -0.5),
        "gdn_conv_w": nrm(ks[2], (L, GDN_CONV, 2 * GDN_QK + GDN_VW), GDN_CONV ** -0.5),
        "gdn_a_log": jnp.log(jax.random.uniform(ks[3], (L, GDN_HEADS), f32, 1.0, 16.0)),
        "gdn_dt_bias": dt + jnp.log(-jnp.expm1(-dt)),
        "gdn_norm_w": 1.0 + nrm(ks[5], (L, GDN_DV), 0.02),
        "w_out": nrm(ks[6], (L, MIX_WIDTH, D_MODEL), MIX_WIDTH ** -0.5 * DN_BETA),
        "ln1_g": 1.0 + nrm(ks[7], (L, D_MODEL), 0.02),
        "ln1_b": nrm(ks[8], (L, D_MODEL), 0.02),
        "w_router": nrm(ks[9], (L, D_MODEL, N_EXPERTS), D_MODEL ** -0.5),
        "router_bias": nrm(ks[10], (L, N_EXPERTS), 0.01),
        "w_gate_e": nrm(ks[11], (L, N_EXPERTS, D_MODEL, D_EXPERT), D_MODEL ** -0.5),
        "w_up_e": nrm(ks[12], (L, N_EXPERTS, D_MODEL, D_EXPERT), D_MODEL ** -0.5),
        "w_down_e": nrm(ks[13], (L, N_EXPERTS, D_EXPERT, D_MODEL), D_EXPERT ** -0.5 * DN_BETA),
        "w_gate_s": nrm(ks[14], (L, D_MODEL, D_SHARED), D_MODEL ** -0.5),
        "w_up_s": nrm(ks[15], (L, D_MODEL, D_SHARED), D_MODEL ** -0.5),
        "w_down_s": nrm(ks[16], (L, D_SHARED, D_MODEL), D_SHARED ** -0.5 * DN_BETA),
        "ln2_g": 1.0 + nrm(ks[17], (L, D_MODEL), 0.02),
        "ln2_b": nrm(ks[18], (L, D_MODEL), 0.02),
    }


def reference(x, w_in, gdn_conv_w, gdn_a_log, gdn_dt_bias, gdn_norm_w, w_out, ln1_g, ln1_b,
              w_router, router_bias, w_gate_e, w_up_e, w_down_e, w_gate_s, w_up_s, w_down_s,
              ln2_g, ln2_b):
    h = x
    for l in range(DEPTH):
        mix = _hybrid_mixer(h, w_in[l], gdn_conv_w[l], gdn_a_log[l], gdn_dt_bias[l], gdn_norm_w[l], w_out[l])
        h = _layer_norm(DN_ALPHA * h + mix, ln1_g[l], ln1_b[l])
        ffn = _moe(h, w_router[l], router_bias[l], w_gate_e[l], w_up_e[l], w_down_e[l],
                   w_gate_s[l], w_up_s[l], w_down_s[l])
        h = _layer_norm(DN_ALPHA * h + ffn, ln2_g[l], ln2_b[l])
    return h
```

```python
import functools
import math

import numpy as np
import jax
import jax.numpy as jnp
from jax import lax
from jax.experimental import pallas as pl
from jax.experimental.pallas import tpu as pltpu

GDN_HEADS = 4
GDN_DK = 128
GDN_DV = 128
GDN_CONV = 4
GDN_CHUNK = 64
RET_HEADS = 4
RET_DK = 64
RET_DV = 128
ROPE_BASE = 10000.0
N_GROUPS = 8
TOPK_GROUPS = 4
TOP_K = 8
ROUTED_SCALE = 2.5
DEPTH = 1
DN_ALPHA = (2.0 * DEPTH) ** 0.25
LN_EPS = 1e-5
NORM_EPS = 1e-6

GDN_QK = GDN_HEADS * GDN_DK
GDN_VW = GDN_HEADS * GDN_DV
RET_QK = RET_HEADS * RET_DK
RET_VW = RET_HEADS * RET_DV

LANES = 128
SUBLANES = 8
VMEM_LIMIT_BYTES = 56 * 1024 * 1024

SUPER = 256
NEG_INF = float("-inf")

COL_QA = 0
COL_KA = COL_QA + GDN_HEADS
COL_VA = COL_KA + GDN_HEADS
COL_ZA = COL_VA + GDN_HEADS
COL_BA = COL_ZA + GDN_HEADS
COL_QKB = COL_BA + 1
COL_VB = COL_QKB + RET_HEADS
COL_GB = COL_VB + RET_HEADS
N_COLBLK = COL_GB + RET_HEADS
PROJ_W = N_COLBLK * LANES

BF16 = jnp.bfloat16
F32 = jnp.float32


def _dot(a, b):
    return jnp.dot(a, b, preferred_element_type=F32)


def _dot_nt(a, b):
    return lax.dot_general(a, b, (((1,), (1,)), ((), ())), preferred_element_type=F32)


def _silu(x):
    return x * jax.nn.sigmoid(x)


def _softplus(x):
    return jnp.maximum(x, 0.0) + jnp.log1p(jnp.exp(-jnp.abs(x)))


def _iota(shape, dim, dtype=jnp.int32):
    return lax.broadcasted_iota(dtype, shape, dim)


def _params(sem):
    return pltpu.CompilerParams(dimension_semantics=sem, vmem_limit_bytes=VMEM_LIMIT_BYTES)


def _proj_kernel(x_ref, w_ref, proj_ref, bat_ref):
    xb = x_ref[...].astype(BF16)
    step = 4 * LANES
    for j in range(0, PROJ_W, step):
        n = min(step, PROJ_W - j)
        proj_ref[:, j:j + n] = _dot(xb, w_ref[:, j:j + n])
    ba = proj_ref[:, COL_BA * LANES:(COL_BA + 1) * LANES]
    bat_ref[...] = ba.T[0:SUBLANES, :]


def _proj(x2, w_bf, tm):
    t, d = x2.shape
    return pl.pallas_call(
        _proj_kernel,
        grid=(t // tm,),
        in_specs=[pl.BlockSpec((tm, d), lambda i: (i, 0)),
                  pl.BlockSpec((d, PROJ_W), lambda i: (0, 0))],
        out_specs=[pl.BlockSpec((tm, PROJ_W), lambda i: (i, 0)),
                   pl.BlockSpec((SUBLANES, tm), lambda i: (0, i))],
        out_shape=[jax.ShapeDtypeStruct((t, PROJ_W), F32),
                   jax.ShapeDtypeStruct((SUBLANES, t), F32)],
        compiler_params=_params(("parallel",)),
        name="proj",
    )(x2, w_bf)


def _mixer_kernel(hp_ref,
                  q_ref, k_ref, v_ref, z_ref, ba_ref, qkb_ref, vb_ref, gb_ref, bat_ref,
                  cwq_ref, cwk_ref, cwv_ref, nw_ref, rc_ref, rs_ref,
                  oa_ref, ob_ref):
    h = pl.program_id(1)
    seq = q_ref.shape[0]
    n_super = seq // SUPER
    n_chunk = SUPER // GDN_CHUNK
    neg_a = hp_ref[0, h]
    dt_b = hp_ref[1, h]
    lg = hp_ref[2, h]

    ri = _iota((SUPER, SUPER), 0)
    ci = _iota((SUPER, SUPER), 1)
    shift = int(math.log2(GDN_CHUNK))
    same = (ri >> shift) == (ci >> shift)
    causal = same & (ci <= ri)
    strict = same & (ci < ri)
    causal_t = same & (ri <= ci)
    eye = jnp.where(ri == ci, 1.0, 0.0)
    lane = _iota((SUPER, LANES), 1)
    dpos = (ri - ci).astype(F32)
    dmask = jnp.where(ri >= ci, jnp.exp(lg * jnp.maximum(dpos, 0.0)), 0.0)
    pos = _iota((SUPER, 1), 0).astype(F32)
    q_scale = jnp.exp(lg * (pos + 1.0))
    k_scale = jnp.exp(lg * (SUPER - 1.0 - pos))
    g_chunk = jnp.exp(jnp.full((1, 1), SUPER, F32) * lg)
    half = lane < RET_DK

    def conv_silu(ref, cw_ref, r0, not_first):
        cur = ref[pl.ds(r0, SUPER), :]
        prev = ref[pl.ds(jnp.maximum(r0 - SUBLANES, 0), SUBLANES), :] * not_first
        ext = jnp.concatenate([prev, cur], axis=0)
        acc = jnp.zeros((SUPER, LANES), F32)
        for j in range(GDN_CONV):
            off = SUBLANES - (GDN_CONV - 1) + j
            acc = acc + cw_ref[j:j + 1, :] * ext[off:off + SUPER, :]
        return _silu(acc)

    def l2n(u):
        return u * lax.rsqrt(jnp.sum(u * u, axis=-1, keepdims=True) + NORM_EPS)

    def body(s, carry):
        st, st2 = carry
        r0 = pl.multiple_of(s * SUPER, SUPER)
        not_first = jnp.where(s > 0, 1.0, 0.0)
        q = l2n(conv_silu(q_ref, cwq_ref, r0, not_first)) * (GDN_DK ** -0.5)
        k = l2n(conv_silu(k_ref, cwk_ref, r0, not_first))
        v = conv_silu(v_ref, cwv_ref, r0, not_first)
        ba = ba_ref[pl.ds(r0, SUPER), :]
        b_col = jnp.sum(jnp.where(lane == h, ba, 0.0), axis=1, keepdims=True)
        a_col = jnp.sum(jnp.where(lane == h + GDN_HEADS, ba, 0.0), axis=1, keepdims=True)
        a_row = bat_ref[pl.ds(h + GDN_HEADS, 1), pl.ds(r0, SUPER)]
        beta = jax.nn.sigmoid(b_col)
        la_col = neg_a * _softplus(a_col + dt_b)
        la_row = neg_a * _softplus(a_row + dt_b)
        g_col = jnp.sum(jnp.where(causal, la_row, 0.0), axis=1, keepdims=True)
        g_row = jnp.sum(jnp.where(causal_t, la_col, 0.0), axis=0, keepdims=True)
        gl_col = jnp.sum(jnp.where(same, la_row, 0.0), axis=1, keepdims=True)
        decay_c = jnp.exp(jnp.where(causal, g_col - g_row, NEG_INF))
        decay_s = jnp.where(strict, decay_c, 0.0)
        kb = k * beta
        k_bf = k.astype(BF16)
        low = _dot_nt(kb.astype(BF16), k_bf) * decay_s
        p = eye - low
        sq = low
        for _ in range(int(math.log2(GDN_CHUNK)) - 1):
            sq_bf = sq.astype(BF16)
            sq = _dot(sq_bf, sq_bf)
            p = p + _dot(p.astype(BF16), sq.astype(BF16))
        eg = jnp.exp(g_col)
        rhs = jnp.concatenate([v * beta, kb * eg], axis=1).astype(BF16)
        wk = _dot(p.astype(BF16), rhs)
        w_val = wk[:, :GDN_DV]
        k_cum = wk[:, GDN_DV:].astype(BF16)
        attn = (_dot_nt(q.astype(BF16), k_bf) * decay_c).astype(BF16)
        q_dec = (q * eg).astype(BF16)
        kd_t = (k * jnp.exp(gl_col - g_col)).T.astype(BF16)
        gl = jnp.exp(gl_col)
        outs = []
        for c in range(n_chunk):
            lo = c * GDN_CHUNK
            hi = lo + GDN_CHUNK
            st_bf = st.astype(BF16)
            v_new = w_val[lo:hi] - _dot(k_cum[lo:hi], st_bf)
            pieces = [jnp.zeros((GDN_CHUNK, GDN_DV), F32)] * n_chunk
            pieces[c] = v_new
            v_pad = jnp.concatenate(pieces, axis=0).astype(BF16)
            outs.append(_dot(q_dec[lo:hi], st_bf) + _dot(attn[lo:hi], v_pad))
            st = st * gl[lo:lo + 1, :] + _dot(kd_t, v_pad)
        o = jnp.concatenate(outs, axis=0)
        o = o * lax.rsqrt(jnp.mean(o * o, axis=-1, keepdims=True) + NORM_EPS) * nw_ref[...]
        oa_ref[pl.ds(r0, SUPER), :] = (o * _silu(z_ref[pl.ds(r0, SUPER), :])).astype(oa_ref.dtype)
        x = qkb_ref[pl.ds(r0, SUPER), :]
        swapped = jnp.where((lane % (2 * (RET_DK // 2))) < (RET_DK // 2),
                            pltpu.roll(x, LANES - RET_DK // 2, 1),
                            pltpu.roll(x, RET_DK // 2, 1))
        xr = x * rc_ref[pl.ds(r0, SUPER), :] + swapped * rs_ref[pl.ds(r0, SUPER), :]
        qm = jnp.where(half, xr, 0.0)
        km = jnp.where(half, pltpu.roll(xr, LANES - RET_DK, 1), 0.0)
        vb = vb_ref[pl.ds(r0, SUPER), :].astype(BF16)
        inner = (_dot_nt(qm.astype(BF16), km.astype(BF16)) * dmask).astype(BF16)
        ob = _dot(inner, vb) + _dot((qm * q_scale).astype(BF16), st2.astype(BF16))
        st2 = st2 * g_chunk + _dot((km * k_scale).T.astype(BF16), vb)
        mu = jnp.mean(ob, axis=-1, keepdims=True)
        oc = ob - mu
        oc = oc * lax.rsqrt(jnp.mean(oc * oc, axis=-1, keepdims=True) + NORM_EPS)
        ob_ref[pl.ds(r0, SUPER), :] = (oc * _silu(gb_ref[pl.ds(r0, SUPER), :])).astype(ob_ref.dtype)
        return st, st2

    init = (jnp.zeros((GDN_DK, GDN_DV), F32), jnp.zeros((LANES, RET_DV), F32))
    lax.fori_loop(0, n_super, body, init)


def _mixer(proj, bat, hp, conv_w, norm_w, rot_c, rot_s, batch, seq):
    t = proj.shape[0]

    def col(c0):
        return pl.BlockSpec((seq, LANES), lambda b, h, c0=c0: (b, c0 + h))

    cw = lambda c0: pl.BlockSpec((GDN_CONV, LANES), lambda b, h, c0=c0: (0, c0 + h))
    const2 = lambda shape: pl.BlockSpec(shape, lambda b, h: (0, 0))
    in_specs = [
        pl.BlockSpec(memory_space=pltpu.SMEM),
        col(COL_QA), col(COL_KA), col(COL_VA), col(COL_ZA),
        pl.BlockSpec((seq, LANES), lambda b, h: (b, COL_BA)),
        col(COL_QKB), col(COL_VB), col(COL_GB),
        pl.BlockSpec((SUBLANES, seq), lambda b, h: (0, b)),
        cw(0), cw(GDN_HEADS), cw(2 * GDN_HEADS),
        const2((1, GDN_DV)), const2((seq, LANES)), const2((seq, LANES)),
    ]
    out_spec = pl.BlockSpec((seq, LANES), lambda b, h: (b, h))
    return pl.pallas_call(
        _mixer_kernel,
        grid=(batch, GDN_HEADS),
        in_specs=in_specs,
        out_specs=[out_spec, out_spec],
        out_shape=[jax.ShapeDtypeStruct((t, GDN_VW), BF16), jax.ShapeDtypeStruct((t, RET_VW), BF16)],
        compiler_params=_params(("parallel", "parallel")),
        name="mixer",
    )(hp, proj, proj, proj, proj, proj, proj, proj, proj, bat, conv_w, conv_w, conv_w, norm_w, rot_c, rot_s)


def _layer_norm(u, g, b):
    mu = jnp.mean(u, axis=-1, keepdims=True)
    uc = u - mu
    var = jnp.mean(uc * uc, axis=-1, keepdims=True)
    return uc * lax.rsqrt(var + LN_EPS) * g + b


def _post_mix_kernel(x_ref, oa_ref, ob_ref, wo_ref, g_ref, b_ref, wr_ref, rb_ref,
                     h_ref, idx_ref, wts_ref, cnt_ref):
    n_exp = wr_ref.shape[0]
    tm = x_ref.shape[0]
    per_grp = n_exp // N_GROUPS
    mix = _dot(oa_ref[...], wo_ref[:GDN_VW, :]) + _dot(ob_ref[...], wo_ref[GDN_VW:, :])
    hh = _layer_norm(DN_ALPHA * x_ref[...] + mix, g_ref[...], b_ref[...])
    h_ref[...] = hh
    scores = jax.nn.sigmoid(_dot_nt(wr_ref[...], hh.astype(BF16)))
    choice = scores + rb_ref[...]
    big = float(n_exp)
    io_g = _iota((per_grp, tm), 0).astype(F32)
    grp = []
    for g in range(N_GROUPS):
        blk = choice[g * per_grp:(g + 1) * per_grp, :]
        m1 = jnp.max(blk, axis=0, keepdims=True)
        i1 = jnp.min(jnp.where(blk == m1, io_g, big), axis=0, keepdims=True)
        m2 = jnp.max(jnp.where(io_g == i1, NEG_INF, blk), axis=0, keepdims=True)
        grp.append(m1 + m2)
    gsc = jnp.concatenate(grp, axis=0)
    io8 = _iota((N_GROUPS, tm), 0).astype(F32)
    gsel = jnp.zeros((N_GROUPS, tm), F32)
    for _ in range(TOPK_GROUPS):
        m = jnp.max(gsc, axis=0, keepdims=True)
        i = jnp.min(jnp.where(gsc == m, io8, big), axis=0, keepdims=True)
        hit = io8 == i
        gsel = jnp.where(hit, 1.0, gsel)
        gsc = jnp.where(hit, NEG_INF, gsc)
    masked = jnp.concatenate(
        [jnp.where(gsel[g:g + 1, :] > 0.0, choice[g * per_grp:(g + 1) * per_grp, :], NEG_INF)
         for g in range(N_GROUPS)], axis=0)
    io_e = _iota((n_exp, tm), 0).astype(F32)
    sel = jnp.zeros((n_exp, tm), F32)
    ids, ws = [], []
    for _ in range(TOP_K):
        m = jnp.max(masked, axis=0, keepdims=True)
        i = jnp.min(jnp.where(masked == m, io_e, big), axis=0, keepdims=True)
        hit = io_e == i
        ws.append(jnp.sum(jnp.where(hit, scores, 0.0), axis=0, keepdims=True))
        ids.append(i)
        masked = jnp.where(hit, NEG_INF, masked)
        sel = jnp.where(hit, 1.0, sel)
    w = jnp.concatenate(ws, axis=0)
    wts_ref[...] = w / jnp.sum(w, axis=0, keepdims=True) * ROUTED_SCALE
    idx_ref[...] = jnp.concatenate(ids, axis=0).astype(jnp.int32)

    @pl.when(pl.program_id(0) == 0)
    def _():
        cnt_ref[...] = jnp.zeros_like(cnt_ref)

    cnt_ref[...] += jnp.sum(sel, axis=1, keepdims=True)


def _post_mix(x2, oa, ob, wo_bf, g1, b1, wr_t, rbias, tm):
    t, d = x2.shape
    n_exp = wr_t.shape[0]
    row = lambda w: pl.BlockSpec((tm, w), lambda i: (i, 0))
    const = lambda shape: pl.BlockSpec(shape, lambda i: (0, 0))
    tok = pl.BlockSpec((TOP_K, tm), lambda i: (0, i))
    return pl.pallas_call(
        _post_mix_kernel,
        grid=(t // tm,),
        in_specs=[row(d), row(GDN_VW), row(RET_VW), const(wo_bf.shape), const((1, d)), const((1, d)),
                  const(wr_t.shape), const((n_exp, 1))],
        out_specs=[row(d), tok, tok, const((n_exp, 1))],
        out_shape=[jax.ShapeDtypeStruct((t, d), F32),
                   jax.ShapeDtypeStruct((TOP_K, t), jnp.int32),
                   jax.ShapeDtypeStruct((TOP_K, t), F32),
                   jax.ShapeDtypeStruct((n_exp, 1), F32)],
        compiler_params=_params(("arbitrary",)),
        name="post_mix",
    )(x2, oa, ob, wo_bf, g1, b1, wr_t, rbias)


def _dest_kernel(idx_ref, base_ref, dest_ref, carry_ref):
    n_exp = base_ref.shape[0]
    tm = idx_ref.shape[1]

    @pl.when(pl.program_id(0) == 0)
    def _():
        carry_ref[...] = jnp.zeros_like(carry_ref)

    io_e = _iota((n_exp, tm), 0)
    idx = idx_ref[...]
    hits = [io_e == idx[k:k + 1, :] for k in range(TOP_K)]
    sel = jnp.zeros((n_exp, tm), F32)
    for hit in hits:
        sel = jnp.where(hit, 1.0, sel)
    before = jnp.where(_iota((tm, tm), 0) < _iota((tm, tm), 1), 1.0, 0.0).astype(BF16)
    rank = _dot(sel.astype(BF16), before)
    tot = base_ref[...] + carry_ref[...] + rank
    dest = [jnp.sum(jnp.where(hit, tot, 0.0), axis=0, keepdims=True) for hit in hits]
    dest_ref[...] = jnp.concatenate(dest, axis=0).astype(jnp.int32)
    carry_ref[...] += jnp.sum(sel, axis=1, keepdims=True)


def _dest(idx_t, base, tm):
    t = idx_t.shape[1]
    n_exp = base.shape[0]
    tok = pl.BlockSpec((TOP_K, tm), lambda i: (0, i))
    return pl.pallas_call(
        _dest_kernel,
        grid=(t // tm,),
        in_specs=[tok, pl.BlockSpec((n_exp, 1), lambda i: (0, 0))],
        out_specs=tok,
        out_shape=jax.ShapeDtypeStruct((TOP_K, t), jnp.int32),
        scratch_shapes=[pltpu.VMEM((n_exp, 1), F32)],
        compiler_params=_params(("arbitrary",)),
        name="dest",
    )(idx_t, base)


def _row_copy(src_ref, src_row, dst_ref, dst_row, sem):
    return pltpu.make_async_copy(src_ref.at[pl.ds(src_row, 1)], dst_ref.at[pl.ds(dst_row, 1)], sem)


def _dispatch_kernel(dest_ref, h_ref, xs_ref, sem):
    tm = h_ref.shape[0]

    def start(t, c):
        for k in range(TOP_K):
            _row_copy(h_ref, t, xs_ref, dest_ref[k, t], sem).start()
        return c

    lax.fori_loop(0, tm, start, 0)

    def wait(t, c):
        for k in range(TOP_K):
            _row_copy(h_ref, t, xs_ref, dest_ref[k, t], sem).wait()
        return c

    lax.fori_loop(0, tm, wait, 0)


def _dispatch(dest_t, h, tm):
    t, d = h.shape
    return pl.pallas_call(
        _dispatch_kernel,
        grid=(t // tm,),
        in_specs=[pl.BlockSpec((TOP_K, tm), lambda i: (0, i), memory_space=pltpu.SMEM),
                  pl.BlockSpec((tm, d), lambda i: (i, 0))],
        out_specs=pl.BlockSpec(memory_space=pl.ANY),
        out_shape=jax.ShapeDtypeStruct((t * TOP_K, d), F32),
        scratch_shapes=[pltpu.SemaphoreType.DMA(())],
        compiler_params=_params(("arbitrary",)),
        name="dispatch",
    )(dest_t, h)


def _expert_kernel(blk_ref, exp_ref, lo_ref, hi_ref, first_ref,
                   xs_ref, wg_ref, wu_ref, wd_ref, eo_ref):
    v = pl.program_id(0)
    tm = xs_ref.shape[0]
    xb = xs_ref[...].astype(BF16)
    gate = _dot(xb, wg_ref[0].astype(BF16))
    up = _dot(xb, wu_ref[0].astype(BF16))
    res = _dot((_silu(gate) * up).astype(BF16), wd_ref[0].astype(BF16))
    rows = _iota((tm, 1), 0)
    mine = (rows >= lo_ref[v]) & (rows < hi_ref[v])

    @pl.when(first_ref[v] == 1)
    def _():
        eo_ref[...] = jnp.where(mine, res, 0.0)

    @pl.when(first_ref[v] == 0)
    def _():
        eo_ref[...] = jnp.where(mine, res, eo_ref[...])


def _experts(meta, xs, wg, wu, wd, tm):
    n_slot, d = xs.shape
    n_exp, _, d_e = wg.shape
    n_visit = meta[0].shape[0]
    grid_spec = pltpu.PrefetchScalarGridSpec(
        num_scalar_prefetch=5,
        grid=(n_visit,),
        in_specs=[pl.BlockSpec((tm, d), lambda v, blk, ex, lo, hi, fi: (blk[v], 0)),
                  pl.BlockSpec((1, d, d_e), lambda v, blk, ex, lo, hi, fi: (ex[v], 0, 0)),
                  pl.BlockSpec((1, d, d_e), lambda v, blk, ex, lo, hi, fi: (ex[v], 0, 0)),
                  pl.BlockSpec((1, d_e, d), lambda v, blk, ex, lo, hi, fi: (ex[v], 0, 0))],
        out_specs=pl.BlockSpec((tm, d), lambda v, blk, ex, lo, hi, fi: (blk[v], 0)),
    )
    return pl.pallas_call(
        _expert_kernel,
        grid_spec=grid_spec,
        out_shape=jax.ShapeDtypeStruct((n_slot, d), F32),
        compiler_params=_params(("arbitrary",)),
        name="experts",
    )(*meta, xs, wg, wu, wd)


def _visit_plan(counts, n_slot, tm):
    n_exp = counts.shape[0]
    ends = jnp.cumsum(counts)
    starts = ends - counts
    first_tile = starts // tm
    last_tile = jnp.where(counts > 0, (ends - 1) // tm, first_tile - 1)
    n_tiles = last_tile - first_tile + 1
    vend = jnp.cumsum(n_tiles)
    vstart = vend - n_tiles
    n_visit = n_slot // tm + n_exp - 1
    v = jnp.arange(n_visit, dtype=jnp.int32)
    total = vend[-1]
    valid = v < total
    vv = jnp.minimum(v, total - 1)
    e = jnp.minimum(jnp.searchsorted(vend, vv, side="right"), n_exp - 1).astype(jnp.int32)
    blk = (first_tile[e] + (vv - vstart[e])).astype(jnp.int32)
    lo = jnp.clip(starts[e] - blk * tm, 0, tm)
    hi = jnp.clip(ends[e] - blk * tm, 0, tm)
    lo = jnp.where(valid, lo, 0).astype(jnp.int32)
    hi = jnp.where(valid, hi, 0).astype(jnp.int32)
    first = (valid & (lo == 0)).astype(jnp.int32)
    return blk, e, lo, hi, first


def _combine_kernel(dest_ref, wts_ref, h_ref, eo_ref, wgs_ref, wus_ref, wds_ref, g_ref, b_ref,
                    out_ref, gbuf, sem):
    tm = h_ref.shape[0]

    def start(t, c):
        for k in range(TOP_K):
            _row_copy(eo_ref, dest_ref[k, t], gbuf.at[k], t, sem).start()
        return c

    lax.fori_loop(0, tm, start, 0)
    hh = h_ref[...]
    hb = hh.astype(BF16)
    hid = _silu(_dot(hb, wgs_ref[...])) * _dot(hb, wus_ref[...])
    acc = DN_ALPHA * hh + _dot(hid.astype(BF16), wds_ref[...])
    w_t = jnp.concatenate([wts_ref[...], jnp.zeros((LANES - TOP_K, tm), F32)], axis=0).T

    def wait(t, c):
        for k in range(TOP_K):
            _row_copy(eo_ref, dest_ref[k, t], gbuf.at[k], t, sem).wait()
        return c

    lax.fori_loop(0, tm, wait, 0)
    for k in range(TOP_K):
        acc = acc + gbuf[k] * w_t[:, k:k + 1]
    out_ref[...] = _layer_norm(acc, g_ref[...], b_ref[...])


def _combine(dest_t, wts_t, h, eo, wgs, wus, wds, g2, b2, tm):
    t, d = h.shape
    const = lambda shape: pl.BlockSpec(shape, lambda i: (0, 0))
    return pl.pallas_call(
        _combine_kernel,
        grid=(t // tm,),
        in_specs=[pl.BlockSpec((TOP_K, tm), lambda i: (0, i), memory_space=pltpu.SMEM),
                  pl.BlockSpec((TOP_K, tm), lambda i: (0, i)),
                  pl.BlockSpec((tm, d), lambda i: (i, 0)),
                  pl.BlockSpec(memory_space=pl.ANY),
                  const(wgs.shape), const(wus.shape), const(wds.shape), const((1, d)), const((1, d))],
        out_specs=pl.BlockSpec((tm, d), lambda i: (i, 0)),
        out_shape=jax.ShapeDtypeStruct((t, d), F32),
        scratch_shapes=[pltpu.VMEM((TOP_K, tm, d), F32), pltpu.SemaphoreType.DMA(())],
        compiler_params=_params(("arbitrary",)),
        name="combine",
    )(dest_t, wts_t, h, eo, wgs, wus, wds, g2, b2)


def _regroup_w_in(w):
    d = w.shape[0]
    splits = np.cumsum([GDN_QK, GDN_QK, GDN_VW, GDN_VW, GDN_HEADS, GDN_HEADS, RET_QK, RET_QK, RET_VW])
    qa, ka, va, za, ba, aa, qb, kb, vb, gb = jnp.split(w, splits.tolist(), axis=1)
    ba_blk = jnp.concatenate([ba, aa, jnp.zeros((d, LANES - 2 * GDN_HEADS), w.dtype)], axis=1)
    qb = qb.reshape(d, RET_HEADS, RET_DK // 2, 2)
    kb = kb.reshape(d, RET_HEADS, RET_DK // 2, 2)
    qkb = jnp.concatenate([qb[..., 0], qb[..., 1], kb[..., 0], kb[..., 1]], axis=-1).reshape(d, RET_HEADS * LANES)
    return jnp.concatenate([qa, ka, va, za, ba_blk, qkb, vb, gb], axis=1).astype(BF16)


def _rotary_tables(seq):
    inv = 1.0 / (ROPE_BASE ** jnp.linspace(0.0, 1.0, RET_DK // 2, dtype=F32))
    ang = jnp.arange(seq, dtype=F32)[:, None] * inv[None, :]
    c, s = jnp.cos(ang), jnp.sin(ang)
    ks = RET_DK ** -0.5
    rot_c = jnp.concatenate([c, c, c * ks, c * ks], axis=1)
    rot_s = jnp.concatenate([-s, s, -s * ks, s * ks], axis=1)
    return rot_c, rot_s


def _tile(n, pref):
    return pref if n % pref == 0 else n


def kernel(x, w_in, gdn_conv_w, gdn_a_log, gdn_dt_bias, gdn_norm_w, w_out, ln1_g, ln1_b, w_router, router_bias,
           w_gate_e, w_up_e, w_down_e, w_gate_s, w_up_s, w_down_s, ln2_g, ln2_b):
    batch, seq, d = x.shape
    t = batch * seq
    n_exp = w_router.shape[-1]
    hcur = x.reshape(t, d)
    for l in range(DEPTH):
        w_bf = _regroup_w_in(w_in[l])
        log_gamma = jnp.log(1.0 - 2.0 ** (-5.0 - jnp.arange(RET_HEADS, dtype=F32)))
        hp = jnp.stack([-jnp.exp(gdn_a_log[l].astype(F32)), gdn_dt_bias[l].astype(F32), log_gamma])
        rot_c, rot_s = _rotary_tables(seq)
        proj, bat = _proj(hcur, w_bf, _tile(t, 512))
        oa, ob = _mixer(proj, bat, hp, gdn_conv_w[l].astype(F32), gdn_norm_w[l].reshape(1, -1).astype(F32),
                        rot_c, rot_s, batch, seq)
        hh, idx_t, wts_t, counts = _post_mix(
            hcur, oa, ob, w_out[l].astype(BF16), ln1_g[l].reshape(1, d), ln1_b[l].reshape(1, d),
            w_router[l].T.astype(BF16), router_bias[l].reshape(n_exp, 1).astype(F32), _tile(t, 256))
        base = jnp.cumsum(counts, axis=0) - counts
        dest_t = _dest(idx_t, base, _tile(t, 512))
        xs = _dispatch(dest_t, hh, _tile(t, 256))
        tm_e = 256
        meta = _visit_plan(counts.reshape(n_exp).astype(jnp.int32), t * TOP_K, tm_e)
        eo = _experts(meta, xs, w_gate_e[l], w_up_e[l], w_down_e[l], tm_e)
        hcur = _combine(dest_t, wts_t, hh, eo, w_gate_s[l].astype(BF16), w_up_s[l].astype(BF16),
                        w_down_s[l].astype(BF16), ln2_g[l].reshape(1, d), ln2_b[l].reshape(1, d), _tile(t, 128))
    return hcur.reshape(batch, seq, d)
```

```python
import math

import numpy as np
import jax
import jax.numpy as jnp
from jax import lax
from jax.experimental import pallas as pl
from jax.experimental.pallas import tpu as pltpu

GDN_HEADS = 4
GDN_DK = 128
GDN_DV = 128
GDN_CONV = 4
GDN_CHUNK = 64
RET_HEADS = 4
RET_DK = 64
RET_DV = 128
ROPE_BASE = 10000.0
N_GROUPS = 8
TOPK_GROUPS = 4
TOP_K = 8
ROUTED_SCALE = 2.5
DEPTH = 1
DN_ALPHA = (2.0 * DEPTH) ** 0.25
LN_EPS = 1e-5
NORM_EPS = 1e-6

GDN_QK = GDN_HEADS * GDN_DK
GDN_VW = GDN_HEADS * GDN_DV
RET_QK = RET_HEADS * RET_DK
RET_VW = RET_HEADS * RET_DV

LANES = 128
SUBLANES = 8
VMEM_LIMIT_BYTES = 56 * 1024 * 1024

SUPER = 256
NEG_INF = float("-inf")

COL_QA = 0
COL_KA = COL_QA + GDN_HEADS
COL_VA = COL_KA + GDN_HEADS
COL_ZA = COL_VA + GDN_HEADS
COL_QKB = COL_ZA + GDN_HEADS
COL_VB = COL_QKB + RET_HEADS
COL_GB = COL_VB + RET_HEADS
COL_BA = COL_GB + RET_HEADS
N_COLBLK = COL_BA + 1
PROJ_W = N_COLBLK * LANES
HEAD_W = GDN_HEADS * LANES
MIX_TS = 512

BF16 = jnp.bfloat16
F32 = jnp.float32


def _dot(a, b):
    return jnp.dot(a, b, preferred_element_type=F32)


def _dot_nt(a, b):
    return lax.dot_general(a, b, (((1,), (1,)), ((), ())), preferred_element_type=F32)


def _silu(x):
    return x * jax.nn.sigmoid(x)


def _softplus(x):
    return jnp.maximum(x, 0.0) + jnp.log1p(jnp.exp(-jnp.abs(x)))


def _iota(shape, dim, dtype=jnp.int32):
    return lax.broadcasted_iota(dtype, shape, dim)


def _params(sem):
    return pltpu.CompilerParams(dimension_semantics=sem, vmem_limit_bytes=VMEM_LIMIT_BYTES)


def _proj_kernel(x_ref, w_ref, proj_ref, bat_ref):
    xb = x_ref[...].astype(BF16)
    step = 4 * LANES
    for j in range(0, PROJ_W, step):
        n = min(step, PROJ_W - j)
        proj_ref[:, j:j + n] = _dot(xb, w_ref[:, j:j + n])
    ba = proj_ref[:, COL_BA * LANES:(COL_BA + 1) * LANES]
    bat_ref[...] = ba.T[0:SUBLANES, :]


def _proj(x2, w_bf, tm):
    t, d = x2.shape
    return pl.pallas_call(
        _proj_kernel,
        grid=(t // tm,),
        in_specs=[pl.BlockSpec((tm, d), lambda i: (i, 0)),
                  pl.BlockSpec((d, PROJ_W), lambda i: (0, 0))],
        out_specs=[pl.BlockSpec((tm, PROJ_W), lambda i: (i, 0)),
                   pl.BlockSpec((SUBLANES, tm), lambda i: (0, i))],
        out_shape=[jax.ShapeDtypeStruct((t, PROJ_W), F32),
                   jax.ShapeDtypeStruct((SUBLANES, t), F32)],
        compiler_params=_params(("parallel",)),
        name="proj",
    )(x2, w_bf)


def _mixer_kernel(hp_ref,
                  q_ref, k_ref, v_ref, z_ref, qkb_ref, vb_ref, gb_ref, ba_ref, bat_ref,
                  qt_ref, kt_ref, vt_ref,
                  cw_ref, nw_ref, rc_ref, rs_ref,
                  oa_ref, ob_ref, st_ref, st2_ref):
    j = pl.program_id(1)
    ts = q_ref.shape[0]
    n_chunk = SUPER // GDN_CHUNK

    @pl.when(j == 0)
    def _():
        st_ref[...] = jnp.zeros_like(st_ref)
        st2_ref[...] = jnp.zeros_like(st2_ref)

    ri = _iota((SUPER, SUPER), 0)
    ci = _iota((SUPER, SUPER), 1)
    shift = int(math.log2(GDN_CHUNK))
    same = (ri >> shift) == (ci >> shift)
    causal = same & (ci <= ri)
    strict = same & (ci < ri)
    causal_t = same & (ri <= ci)
    eye = jnp.where(ri == ci, 1.0, 0.0)
    lane = _iota((SUPER, LANES), 1)
    dpos = jnp.maximum((ri - ci).astype(F32), 0.0)
    lower = ri >= ci
    pos = _iota((SUPER, 1), 0).astype(F32)
    half = lane < RET_DK
    pair_lo = (lane % RET_DK) < (RET_DK // 2)

    def conv_silu(ref, tail_ref, c0, h, r0, s):
        cols = slice(h * LANES, (h + 1) * LANES)
        cur = ref[pl.ds(r0, SUPER), cols]
        inside = ref[pl.ds(pl.multiple_of(jnp.maximum(r0 - SUBLANES, 0), SUBLANES), SUBLANES), cols]
        prev = jnp.where(s > 0, inside, tail_ref[:, cols])
        prev = jnp.where((s == 0) & (j == 0), 0.0, prev)
        ext = jnp.concatenate([prev, cur], axis=0)
        acc = jnp.zeros((SUPER, LANES), F32)
        for jj in range(GDN_CONV):
            off = SUBLANES - (GDN_CONV - 1) + jj
            acc = acc + cw_ref[jj:jj + 1, c0 + h * LANES:c0 + (h + 1) * LANES] * ext[off:off + SUPER, :]
        return _silu(acc)

    def l2n(u):
        return u * lax.rsqrt(jnp.sum(u * u, axis=-1, keepdims=True) + NORM_EPS)

    heads = range(GDN_HEADS)
    hcols = [slice(h * LANES, (h + 1) * LANES) for h in heads]

    def gdn_heads(r0, s):
        q = [l2n(conv_silu(q_ref, qt_ref, 0, h, r0, s)) * (GDN_DK ** -0.5) for h in heads]
        k = [l2n(conv_silu(k_ref, kt_ref, GDN_QK, h, r0, s)) for h in heads]
        v = [conv_silu(v_ref, vt_ref, 2 * GDN_QK, h, r0, s) for h in heads]
        ba = ba_ref[pl.ds(r0, SUPER), :]
        beta, g_col, gl_col, decay_c, decay_s = [], [], [], [], []
        for h in heads:
            neg_a = hp_ref[0, h]
            dt_b = hp_ref[1, h]
            b_col = jnp.sum(jnp.where(lane == h, ba, 0.0), axis=1, keepdims=True)
            a_col = jnp.sum(jnp.where(lane == h + GDN_HEADS, ba, 0.0), axis=1, keepdims=True)
            a_row = bat_ref[h + GDN_HEADS:h + GDN_HEADS + 1, pl.ds(r0, SUPER)]
            beta.append(jax.nn.sigmoid(b_col))
            la_col = neg_a * _softplus(a_col + dt_b)
            la_row = neg_a * _softplus(a_row + dt_b)
            gc = jnp.sum(jnp.where(causal, la_row, 0.0), axis=1, keepdims=True)
            gr = jnp.sum(jnp.where(causal_t, la_col, 0.0), axis=0, keepdims=True)
            g_col.append(gc)
            gl_col.append(jnp.sum(jnp.where(same, la_row, 0.0), axis=1, keepdims=True))
            dc = jnp.exp(jnp.where(causal, gc - gr, NEG_INF))
            decay_c.append(dc)
            decay_s.append(jnp.where(strict, dc, 0.0))
        kb = [k[h] * beta[h] for h in heads]
        k_bf = [k[h].astype(BF16) for h in heads]
        low = [_dot_nt(kb[h].astype(BF16), k_bf[h]) * decay_s[h] for h in heads]
        p = [eye - low[h] for h in heads]
        sq = low
        for _ in range(shift - 1):
            sq_bf = [sq[h].astype(BF16) for h in heads]
            sq = [_dot(sq_bf[h], sq_bf[h]) for h in heads]
            p = [p[h] + _dot(p[h].astype(BF16), sq[h].astype(BF16)) for h in heads]
        eg = [jnp.exp(g_col[h]) for h in heads]
        rhs = [jnp.concatenate([v[h] * beta[h], kb[h] * eg[h]], axis=1).astype(BF16) for h in heads]
        wk = [_dot(p[h].astype(BF16), rhs[h]) for h in heads]
        w_val = [wk[h][:, :GDN_DV] for h in heads]
        k_cum = [wk[h][:, GDN_DV:].astype(BF16) for h in heads]
        attn = [(_dot_nt(q[h].astype(BF16), k_bf[h]) * decay_c[h]).astype(BF16) for h in heads]
        q_dec = [(q[h] * eg[h]).astype(BF16) for h in heads]
        kd_t = [(k[h] * jnp.exp(gl_col[h] - g_col[h])).T.astype(BF16) for h in heads]
        gl = [jnp.exp(gl_col[h]) for h in heads]
        st = [st_ref[h] for h in heads]
        outs = [[] for _ in heads]
        for c in range(n_chunk):
            lo = c * GDN_CHUNK
            hi = lo + GDN_CHUNK
            st_bf = [st[h].astype(BF16) for h in heads]
            v_new = [w_val[h][lo:hi] - _dot(k_cum[h][lo:hi], st_bf[h]) for h in heads]
            v_pad = []
            for h in heads:
                pieces = [jnp.zeros((GDN_CHUNK, GDN_DV), F32)] * n_chunk
                pieces[c] = v_new[h]
                v_pad.append(jnp.concatenate(pieces, axis=0).astype(BF16))
            for h in heads:
                outs[h].append(_dot(q_dec[h][lo:hi], st_bf[h]) + _dot(attn[h][lo:hi], v_pad[h]))
            st = [st[h] * gl[h][lo:lo + 1, :] + _dot(kd_t[h], v_pad[h]) for h in heads]
        for h in heads:
            st_ref[h] = st[h]
            o = jnp.concatenate(outs[h], axis=0)
            o = o * lax.rsqrt(jnp.mean(o * o, axis=-1, keepdims=True) + NORM_EPS) * nw_ref[...]
            oa_ref[pl.ds(r0, SUPER), hcols[h]] = (o * _silu(z_ref[pl.ds(r0, SUPER), hcols[h]])).astype(oa_ref.dtype)

    def ret_heads(r0):
        rc = rc_ref[pl.ds(r0, SUPER), :]
        rs = rs_ref[pl.ds(r0, SUPER), :]
        lg = [hp_ref[2, h] for h in heads]
        qm, km, vb = [], [], []
        for h in heads:
            x = qkb_ref[pl.ds(r0, SUPER), hcols[h]]
            swapped = jnp.where(pair_lo, pltpu.roll(x, LANES - RET_DK // 2, 1), pltpu.roll(x, RET_DK // 2, 1))
            xr = x * rc + swapped * rs
            qm.append(jnp.where(half, xr, 0.0))
            km.append(jnp.where(half, pltpu.roll(xr, LANES - RET_DK, 1), 0.0))
            vb.append(vb_ref[pl.ds(r0, SUPER), hcols[h]].astype(BF16))
        inner = [(_dot_nt(qm[h].astype(BF16), km[h].astype(BF16))
                  * jnp.where(lower, jnp.exp(lg[h] * dpos), 0.0)).astype(BF16) for h in heads]
        st2 = [st2_ref[h] for h in heads]
        ob = [_dot(inner[h], vb[h])
              + _dot((qm[h] * jnp.exp(lg[h] * (pos + 1.0))).astype(BF16), st2[h].astype(BF16)) for h in heads]
        for h in heads:
            g_chunk = jnp.exp(jnp.full((1, 1), SUPER, F32) * lg[h])
            st2_ref[h] = st2[h] * g_chunk + _dot((km[h] * jnp.exp(lg[h] * (SUPER - 1.0 - pos))).T.astype(BF16), vb[h])
        for h in heads:
            mu = jnp.mean(ob[h], axis=-1, keepdims=True)
            oc = ob[h] - mu
            oc = oc * lax.rsqrt(jnp.mean(oc * oc, axis=-1, keepdims=True) + NORM_EPS)
            ob_ref[pl.ds(r0, SUPER), hcols[h]] = (oc * _silu(gb_ref[pl.ds(r0, SUPER), hcols[h]])).astype(ob_ref.dtype)

    def body(s, carry):
        r0 = pl.multiple_of(s * SUPER, SUPER)
        gdn_heads(r0, s)
        ret_heads(r0)
        return carry

    lax.fori_loop(0, ts // SUPER, body, 0)


def _mixer(proj, bat, hp, conv_w, norm_w, rot_c, rot_s, batch, seq):
    t = proj.shape[0]
    ts = MIX_TS if seq % MIX_TS == 0 else seq
    nj = seq // ts
    grp = lambda c0: pl.BlockSpec((ts, HEAD_W), lambda b, j, c=c0 // GDN_HEADS: (b * nj + j, c))
    tail = lambda c0: pl.BlockSpec(
        (SUBLANES, HEAD_W),
        lambda b, j, c=c0 // GDN_HEADS: (jnp.maximum((b * seq + j * ts) // SUBLANES - 1, 0), c))
    const2 = lambda shape: pl.BlockSpec(shape, lambda b, j: (0, 0))
    tab = pl.BlockSpec((ts, LANES), lambda b, j: (j, 0))
    in_specs = [
        pl.BlockSpec(memory_space=pltpu.SMEM),
        grp(COL_QA), grp(COL_KA), grp(COL_VA), grp(COL_ZA), grp(COL_QKB), grp(COL_VB), grp(COL_GB),
        pl.BlockSpec((ts, LANES), lambda b, j: (b * nj + j, COL_BA)),
        pl.BlockSpec((SUBLANES, ts), lambda b, j: (0, b * nj + j)),
        tail(COL_QA), tail(COL_KA), tail(COL_VA),
        const2(conv_w.shape), const2((1, GDN_DV)), tab, tab,
    ]
    out_spec = pl.BlockSpec((ts, HEAD_W), lambda b, j: (b * nj + j, 0))
    return pl.pallas_call(
        _mixer_kernel,
        grid=(batch, nj),
        in_specs=in_specs,
        out_specs=[out_spec, out_spec],
        out_shape=[jax.ShapeDtypeStruct((t, GDN_VW), BF16), jax.ShapeDtypeStruct((t, RET_VW), BF16)],
        scratch_shapes=[pltpu.VMEM((GDN_HEADS, GDN_DK, GDN_DV), F32),
                        pltpu.VMEM((RET_HEADS, LANES, RET_DV), F32)],
        compiler_params=_params(("parallel", "arbitrary")),
        name="mixer",
    )(hp, proj, proj, proj, proj, proj, proj, proj, proj, bat, proj, proj, proj, conv_w, norm_w, rot_c, rot_s)


def _layer_norm(u, g, b):
    mu = jnp.mean(u, axis=-1, keepdims=True)
    uc = u - mu
    var = jnp.mean(uc * uc, axis=-1, keepdims=True)
    return uc * lax.rsqrt(var + LN_EPS) * g + b


def _post_mix_kernel(x_ref, oa_ref, ob_ref, wo_ref, g_ref, b_ref, wr_ref, rb_ref,
                     h_ref, idx_ref, wts_ref, cnt_ref):
    n_exp = wr_ref.shape[0]
    tm = x_ref.shape[0]
    per_grp = n_exp // N_GROUPS
    mix = _dot(oa_ref[...], wo_ref[:GDN_VW, :]) + _dot(ob_ref[...], wo_ref[GDN_VW:, :])
    hh = _layer_norm(DN_ALPHA * x_ref[...] + mix, g_ref[...], b_ref[...])
    h_ref[...] = hh
    scores = jax.nn.sigmoid(_dot_nt(wr_ref[...], hh.astype(BF16)))
    choice = scores + rb_ref[...]
    big = float(n_exp)
    io_g = _iota((per_grp, tm), 0).astype(F32)
    grp = []
    for g in range(N_GROUPS):
        blk = choice[g * per_grp:(g + 1) * per_grp, :]
        m1 = jnp.max(blk, axis=0, keepdims=True)
        i1 = jnp.min(jnp.where(blk == m1, io_g, big), axis=0, keepdims=True)
        m2 = jnp.max(jnp.where(io_g == i1, NEG_INF, blk), axis=0, keepdims=True)
        grp.append(m1 + m2)
    gsc = jnp.concatenate(grp, axis=0)
    io8 = _iota((N_GROUPS, tm), 0).astype(F32)
    gsel = jnp.zeros((N_GROUPS, tm), F32)
    for _ in range(TOPK_GROUPS):
        m = jnp.max(gsc, axis=0, keepdims=True)
        i = jnp.min(jnp.where(gsc == m, io8, big), axis=0, keepdims=True)
        hit = io8 == i
        gsel = jnp.where(hit, 1.0, gsel)
        gsc = jnp.where(hit, NEG_INF, gsc)
    masked = jnp.concatenate(
        [jnp.where(gsel[g:g + 1, :] > 0.0, choice[g * per_grp:(g + 1) * per_grp, :], NEG_INF)
         for g in range(N_GROUPS)], axis=0)
    io_e = _iota((n_exp, tm), 0).astype(F32)
    sel = jnp.zeros((n_exp, tm), F32)
    ids, ws = [], []
    for _ in range(TOP_K):
        m = jnp.max(masked, axis=0, keepdims=True)
        i = jnp.min(jnp.where(masked == m, io_e, big), axis=0, keepdims=True)
        hit = io_e == i
        ws.append(jnp.sum(jnp.where(hit, scores, 0.0), axis=0, keepdims=True))
        ids.append(i)
        masked = jnp.where(hit, NEG_INF, masked)
        sel = jnp.where(hit, 1.0, sel)
    w = jnp.concatenate(ws, axis=0)
    wts_ref[...] = w / jnp.sum(w, axis=0, keepdims=True) * ROUTED_SCALE
    idx_ref[...] = jnp.concatenate(ids, axis=0).astype(jnp.int32)

    @pl.when(pl.program_id(0) == 0)
    def _():
        cnt_ref[...] = jnp.zeros_like(cnt_ref)

    cnt_ref[...] += jnp.sum(sel, axis=1, keepdims=True)


def _post_mix(x2, oa, ob, wo_bf, g1, b1, wr_t, rbias, tm):
    t, d = x2.shape
    n_exp = wr_t.shape[0]
    row = lambda w: pl.BlockSpec((tm, w), lambda i: (i, 0))
    const = lambda shape: pl.BlockSpec(shape, lambda i: (0, 0))
    tok = pl.BlockSpec((TOP_K, tm), lambda i: (0, i))
    return pl.pallas_call(
        _post_mix_kernel,
        grid=(t // tm,),
        in_specs=[row(d), row(GDN_VW), row(RET_VW), const(wo_bf.shape), const((1, d)), const((1, d)),
                  const(wr_t.shape), const((n_exp, 1))],
        out_specs=[row(d), tok, tok, const((n_exp, 1))],
        out_shape=[jax.ShapeDtypeStruct((t, d), F32),
                   jax.ShapeDtypeStruct((TOP_K, t), jnp.int32),
                   jax.ShapeDtypeStruct((TOP_K, t), F32),
                   jax.ShapeDtypeStruct((n_exp, 1), F32)],
        compiler_params=_params(("arbitrary",)),
        name="post_mix",
    )(x2, oa, ob, wo_bf, g1, b1, wr_t, rbias)


def _dest_kernel(idx_ref, base_ref, dest_ref, carry_ref):
    n_exp = base_ref.shape[0]
    tm = idx_ref.shape[1]

    @pl.when(pl.program_id(0) == 0)
    def _():
        carry_ref[...] = jnp.zeros_like(carry_ref)

    io_e = _iota((n_exp, tm), 0)
    idx = idx_ref[...]
    hits = [io_e == idx[k:k + 1, :] for k in range(TOP_K)]
    sel = jnp.zeros((n_exp, tm), F32)
    for hit in hits:
        sel = jnp.where(hit, 1.0, sel)
    before = jnp.where(_iota((tm, tm), 0) < _iota((tm, tm), 1), 1.0, 0.0).astype(BF16)
    rank = _dot(sel.astype(BF16), before)
    tot = base_ref[...] + carry_ref[...] + rank
    dest = [jnp.sum(jnp.where(hit, tot, 0.0), axis=0, keepdims=True) for hit in hits]
    dest_ref[...] = jnp.concatenate(dest, axis=0).astype(jnp.int32)
    carry_ref[...] += jnp.sum(sel, axis=1, keepdims=True)


def _dest(idx_t, base, tm):
    t = idx_t.shape[1]
    n_exp = base.shape[0]
    tok = pl.BlockSpec((TOP_K, tm), lambda i: (0, i))
    return pl.pallas_call(
        _dest_kernel,
        grid=(t // tm,),
        in_specs=[tok, pl.BlockSpec((n_exp, 1), lambda i: (0, 0))],
        out_specs=tok,
        out_shape=jax.ShapeDtypeStruct((TOP_K, t), jnp.int32),
        scratch_shapes=[pltpu.VMEM((n_exp, 1), F32)],
        compiler_params=_params(("arbitrary",)),
        name="dest",
    )(idx_t, base)


def _row_copy(src_ref, src_row, dst_ref, dst_row, sem):
    return pltpu.make_async_copy(src_ref.at[pl.ds(src_row, 1)], dst_ref.at[pl.ds(dst_row, 1)], sem)


def _dispatch_kernel(dest_ref, h_ref, xs_ref, sem):
    tm = h_ref.shape[0]

    def start(t, c):
        for k in range(TOP_K):
            _row_copy(h_ref, t, xs_ref, dest_ref[k, t], sem).start()
        return c

    lax.fori_loop(0, tm, start, 0)

    def wait(t, c):
        for k in range(TOP_K):
            _row_copy(h_ref, t, xs_ref, dest_ref[k, t], sem).wait()
        return c

    lax.fori_loop(0, tm, wait, 0)


def _dispatch(dest_t, h, tm):
    t, d = h.shape
    return pl.pallas_call(
        _dispatch_kernel,
        grid=(t // tm,),
        in_specs=[pl.BlockSpec((TOP_K, tm), lambda i: (0, i), memory_space=pltpu.SMEM),
                  pl.BlockSpec((tm, d), lambda i: (i, 0))],
        out_specs=pl.BlockSpec(memory_space=pl.ANY),
        out_shape=jax.ShapeDtypeStruct((t * TOP_K, d), F32),
        scratch_shapes=[pltpu.SemaphoreType.DMA(())],
        compiler_params=_params(("arbitrary",)),
        name="dispatch",
    )(dest_t, h)


def _expert_kernel(blk_ref, exp_ref, lo_ref, hi_ref, first_ref,
                   xs_ref, wg_ref, wu_ref, wd_ref, eo_ref):
    v = pl.program_id(0)
    tm = xs_ref.shape[0]
    xb = xs_ref[...].astype(BF16)
    gate = _dot(xb, wg_ref[0].astype(BF16))
    up = _dot(xb, wu_ref[0].astype(BF16))
    res = _dot((_silu(gate) * up).astype(BF16), wd_ref[0].astype(BF16))
    rows = _iota((tm, 1), 0)
    mine = (rows >= lo_ref[v]) & (rows < hi_ref[v])

    @pl.when(first_ref[v] == 1)
    def _():
        eo_ref[...] = jnp.where(mine, res, 0.0)

    @pl.when(first_ref[v] == 0)
    def _():
        eo_ref[...] = jnp.where(mine, res, eo_ref[...])


def _experts(meta, xs, wg, wu, wd, tm):
    n_slot, d = xs.shape
    n_exp, _, d_e = wg.shape
    n_visit = meta[0].shape[0]
    grid_spec = pltpu.PrefetchScalarGridSpec(
        num_scalar_prefetch=5,
        grid=(n_visit,),
        in_specs=[pl.BlockSpec((tm, d), lambda v, blk, ex, lo, hi, fi: (blk[v], 0)),
                  pl.BlockSpec((1, d, d_e), lambda v, blk, ex, lo, hi, fi: (ex[v], 0, 0)),
                  pl.BlockSpec((1, d, d_e), lambda v, blk, ex, lo, hi, fi: (ex[v], 0, 0)),
                  pl.BlockSpec((1, d_e, d), lambda v, blk, ex, lo, hi, fi: (ex[v], 0, 0))],
        out_specs=pl.BlockSpec((tm, d), lambda v, blk, ex, lo, hi, fi: (blk[v], 0)),
    )
    return pl.pallas_call(
        _expert_kernel,
        grid_spec=grid_spec,
        out_shape=jax.ShapeDtypeStruct((n_slot, d), F32),
        compiler_params=_params(("arbitrary",)),
        name="experts",
    )(*meta, xs, wg, wu, wd)


def _visit_plan(counts, n_slot, tm):
    n_exp = counts.shape[0]
    ends = jnp.cumsum(counts)
    starts = ends - counts
    first_tile = starts // tm
    last_tile = jnp.where(counts > 0, (ends - 1) // tm, first_tile - 1)
    n_tiles = last_tile - first_tile + 1
    vend = jnp.cumsum(n_tiles)
    vstart = vend - n_tiles
    n_visit = n_slot // tm + n_exp - 1
    v = jnp.arange(n_visit, dtype=jnp.int32)
    total = vend[-1]
    valid = v < total
    vv = jnp.minimum(v, total - 1)
    e = jnp.minimum(jnp.sum((vend[None, :] <= vv[:, None]).astype(jnp.int32), axis=1), n_exp - 1)
    blk = (first_tile[e] + (vv - vstart[e])).astype(jnp.int32)
    lo = jnp.clip(starts[e] - blk * tm, 0, tm)
    hi = jnp.clip(ends[e] - blk * tm, 0, tm)
    lo = jnp.where(valid, lo, 0).astype(jnp.int32)
    hi = jnp.where(valid, hi, 0).astype(jnp.int32)
    first = (valid & (lo == 0)).astype(jnp.int32)
    return blk, e, lo, hi, first


def _combine_kernel(dest_ref, wts_ref, h_ref, eo_ref, wgs_ref, wus_ref, wds_ref, g_ref, b_ref,
                    out_ref, gbuf, sem):
    tm = h_ref.shape[0]

    def start(t, c):
        for k in range(TOP_K):
            _row_copy(eo_ref, dest_ref[k, t], gbuf.at[k], t, sem).start()
        return c

    lax.fori_loop(0, tm, start, 0)
    hh = h_ref[...]
    hb = hh.astype(BF16)
    hid = _silu(_dot(hb, wgs_ref[...])) * _dot(hb, wus_ref[...])
    acc = DN_ALPHA * hh + _dot(hid.astype(BF16), wds_ref[...])
    w_t = jnp.concatenate([wts_ref[...], jnp.zeros((LANES - TOP_K, tm), F32)], axis=0).T

    def wait(t, c):
        for k in range(TOP_K):
            _row_copy(eo_ref, dest_ref[k, t], gbuf.at[k], t, sem).wait()
        return c

    lax.fori_loop(0, tm, wait, 0)
    for k in range(TOP_K):
        acc = acc + gbuf[k] * w_t[:, k:k + 1]
    out_ref[...] = _layer_norm(acc, g_ref[...], b_ref[...])


def _combine(dest_t, wts_t, h, eo, wgs, wus, wds, g2, b2, tm):
    t, d = h.shape
    const = lambda shape: pl.BlockSpec(shape, lambda i: (0, 0))
    return pl.pallas_call(
        _combine_kernel,
        grid=(t // tm,),
        in_specs=[pl.BlockSpec((TOP_K, tm), lambda i: (0, i), memory_space=pltpu.SMEM),
                  pl.BlockSpec((TOP_K, tm), lambda i: (0, i)),
                  pl.BlockSpec((tm, d), lambda i: (i, 0)),
                  pl.BlockSpec(memory_space=pl.ANY),
                  const(wgs.shape), const(wus.shape), const(wds.shape), const((1, d)), const((1, d))],
        out_specs=pl.BlockSpec((tm, d), lambda i: (i, 0)),
        out_shape=jax.ShapeDtypeStruct((t, d), F32),
        scratch_shapes=[pltpu.VMEM((TOP_K, tm, d), F32), pltpu.SemaphoreType.DMA(())],
        compiler_params=_params(("arbitrary",)),
        name="combine",
    )(dest_t, wts_t, h, eo, wgs, wus, wds, g2, b2)


def _regroup_w_in(w):
    d = w.shape[0]
    splits = np.cumsum([GDN_QK, GDN_QK, GDN_VW, GDN_VW, GDN_HEADS, GDN_HEADS, RET_QK, RET_QK, RET_VW])
    qa, ka, va, za, ba, aa, qb, kb, vb, gb = jnp.split(w, splits.tolist(), axis=1)
    ba_blk = jnp.concatenate([ba, aa, jnp.zeros((d, LANES - 2 * GDN_HEADS), w.dtype)], axis=1)
    qb = qb.reshape(d, RET_HEADS, RET_DK // 2, 2)
    kb = kb.reshape(d, RET_HEADS, RET_DK // 2, 2)
    qkb = jnp.concatenate([qb[..., 0], qb[..., 1], kb[..., 0], kb[..., 1]], axis=-1).reshape(d, RET_HEADS * LANES)
    return jnp.concatenate([qa, ka, va, za, qkb, vb, gb, ba_blk], axis=1).astype(BF16)


def _rotary_tables(seq):
    inv = 1.0 / (ROPE_BASE ** jnp.linspace(0.0, 1.0, RET_DK // 2, dtype=F32))
    ang = jnp.arange(seq, dtype=F32)[:, None] * inv[None, :]
    c, s = jnp.cos(ang), jnp.sin(ang)
    ks = RET_DK ** -0.5
    rot_c = jnp.concatenate([c, c, c * ks, c * ks], axis=1)
    rot_s = jnp.concatenate([-s, s, -s * ks, s * ks], axis=1)
    return rot_c, rot_s


def _tile(n, pref):
    return pref if n % pref == 0 else n


def kernel(x, w_in, gdn_conv_w, gdn_a_log, gdn_dt_bias, gdn_norm_w, w_out, ln1_g, ln1_b, w_router, router_bias,
           w_gate_e, w_up_e, w_down_e, w_gate_s, w_up_s, w_down_s, ln2_g, ln2_b):
    batch, seq, d = x.shape
    t = batch * seq
    n_exp = w_router.shape[-1]
    hcur = x.reshape(t, d)
    for l in range(DEPTH):
        w_bf = _regroup_w_in(w_in[l])
        log_gamma = jnp.log(1.0 - 2.0 ** (-5.0 - jnp.arange(RET_HEADS, dtype=F32)))
        hp = jnp.stack([-jnp.exp(gdn_a_log[l].astype(F32)), gdn_dt_bias[l].astype(F32), log_gamma])
        rot_c, rot_s = _rotary_tables(seq)
        proj, bat = _proj(hcur, w_bf, _tile(t, 512))
        oa, ob = _mixer(proj, bat, hp, gdn_conv_w[l].astype(F32), gdn_norm_w[l].reshape(1, -1).astype(F32),
                        rot_c, rot_s, batch, seq)
        hh, idx_t, wts_t, counts = _post_mix(
            hcur, oa, ob, w_out[l].astype(BF16), ln1_g[l].reshape(1, d), ln1_b[l].reshape(1, d),
            w_router[l].T.astype(BF16), router_bias[l].reshape(n_exp, 1).astype(F32), _tile(t, 256))
        base = jnp.cumsum(counts, axis=0) - counts
        dest_t = _dest(idx_t, base, _tile(t, 512))
        xs = _dispatch(dest_t, hh, _tile(t, 256))
        tm_e = 256
        meta = _visit_plan(counts.reshape(n_exp).astype(jnp.int32), t * TOP_K, tm_e)
        eo = _experts(meta, xs, w_gate_e[l], w_up_e[l], w_down_e[l], tm_e)
        hcur = _combine(dest_t, wts_t, hh, eo, w_gate_s[l].astype(BF16), w_up_s[l].astype(BF16),
                        w_down_s[l].astype(BF16), ln2_g[l].reshape(1, d), ln2_b[l].reshape(1, d), _tile(t, 128))
    return hcur.reshape(batch, seq, d)
```

```python
import math

import numpy as np
import jax
import jax.numpy as jnp
from jax import lax
from jax.experimental import pallas as pl
from jax.experimental.pallas import tpu as pltpu

GDN_HEADS = 4
GDN_DK = 128
GDN_DV = 128
GDN_CONV = 4
GDN_CHUNK = 64
RET_HEADS = 4
RET_DK = 64
RET_DV = 128
ROPE_BASE = 10000.0
N_GROUPS = 8
TOPK_GROUPS = 4
TOP_K = 8
ROUTED_SCALE = 2.5
DEPTH = 1
DN_ALPHA = (2.0 * DEPTH) ** 0.25
LN_EPS = 1e-5
NORM_EPS = 1e-6

GDN_QK = GDN_HEADS * GDN_DK
GDN_VW = GDN_HEADS * GDN_DV
RET_QK = RET_HEADS * RET_DK
RET_VW = RET_HEADS * RET_DV

LANES = 128
SUBLANES = 8
VMEM_LIMIT_BYTES = 56 * 1024 * 1024

SUPER = 256
NEG_INF = float("-inf")

COL_QA = 0
COL_KA = COL_QA + GDN_HEADS
COL_VA = COL_KA + GDN_HEADS
COL_ZA = COL_VA + GDN_HEADS
COL_QKB = COL_ZA + GDN_HEADS
COL_VB = COL_QKB + RET_HEADS
COL_GB = COL_VB + RET_HEADS
COL_BA = COL_GB + RET_HEADS
N_COLBLK = COL_BA + 1
PROJ_W = N_COLBLK * LANES
HEAD_W = GDN_HEADS * LANES
MIX_TS = 512
EXPERT_CHUNK = 256

BF16 = jnp.bfloat16
F32 = jnp.float32


def _dot(a, b):
    return jnp.dot(a, b, preferred_element_type=F32)


def _dot_nt(a, b):
    return lax.dot_general(a, b, (((1,), (1,)), ((), ())), preferred_element_type=F32)


def _silu(x):
    return x * jax.nn.sigmoid(x)


def _softplus(x):
    return jnp.maximum(x, 0.0) + jnp.log1p(jnp.exp(-jnp.abs(x)))


def _iota(shape, dim, dtype=jnp.int32):
    return lax.broadcasted_iota(dtype, shape, dim)


def _params(sem):
    return pltpu.CompilerParams(dimension_semantics=sem, vmem_limit_bytes=VMEM_LIMIT_BYTES)


def _load_token_tiles(ref, n, lead=()):
    parts = [ref[lead + (pl.ds(s, n, stride=SUBLANES), slice(None))] for s in range(SUBLANES)]
    return jnp.concatenate(parts, axis=1)


def _store_token_tiles(ref, val, lead=()):
    n = val.shape[0]
    for s in range(SUBLANES):
        ref[lead + (pl.ds(s, n, stride=SUBLANES), slice(None))] = val[:, s * LANES:(s + 1) * LANES]


def _tile_copy(src_ref, src_row, dst_ref, dst_row, sem, n_tiles=1):
    rows = n_tiles * SUBLANES
    aligned = lambda r: r if isinstance(r, int) else pl.multiple_of(r, SUBLANES)
    return pltpu.make_async_copy(src_ref.at[pl.ds(aligned(src_row), rows)],
                                 dst_ref.at[pl.ds(aligned(dst_row), rows)], sem)


def _proj_kernel(x_ref, w_ref, proj_ref, bat_ref):
    xb = x_ref[...].astype(BF16)
    step = 4 * LANES
    for j in range(0, PROJ_W, step):
        n = min(step, PROJ_W - j)
        proj_ref[:, j:j + n] = _dot(xb, w_ref[:, j:j + n])
    ba = proj_ref[:, COL_BA * LANES:(COL_BA + 1) * LANES]
    bat_ref[...] = ba.T[0:SUBLANES, :]


def _proj(x2, w_bf, tm):
    t, d = x2.shape
    return pl.pallas_call(
        _proj_kernel,
        grid=(t // tm,),
        in_specs=[pl.BlockSpec((tm, d), lambda i: (i, 0)),
                  pl.BlockSpec((d, PROJ_W), lambda i: (0, 0))],
        out_specs=[pl.BlockSpec((tm, PROJ_W), lambda i: (i, 0)),
                   pl.BlockSpec((SUBLANES, tm), lambda i: (0, i))],
        out_shape=[jax.ShapeDtypeStruct((t, PROJ_W), F32),
                   jax.ShapeDtypeStruct((SUBLANES, t), F32)],
        compiler_params=_params(("parallel",)),
        name="proj",
    )(x2, w_bf)


def _mixer_kernel(hp_ref,
                  q_ref, k_ref, v_ref, z_ref, qkb_ref, vb_ref, gb_ref, ba_ref, bat_ref,
                  qt_ref, kt_ref, vt_ref,
                  cw_ref, nw_ref, rc_ref, rs_ref,
                  oa_ref, ob_ref, st_ref, st2_ref):
    j = pl.program_id(1)
    ts = q_ref.shape[0]
    n_chunk = SUPER // GDN_CHUNK

    @pl.when(j == 0)
    def _():
        st_ref[...] = jnp.zeros_like(st_ref)
        st2_ref[...] = jnp.zeros_like(st2_ref)

    ri = _iota((SUPER, SUPER), 0)
    ci = _iota((SUPER, SUPER), 1)
    shift = int(math.log2(GDN_CHUNK))
    same = (ri >> shift) == (ci >> shift)
    causal = same & (ci <= ri)
    strict = same & (ci < ri)
    causal_t = same & (ri <= ci)
    eye = jnp.where(ri == ci, 1.0, 0.0)
    lane = _iota((SUPER, LANES), 1)
    dpos = jnp.maximum((ri - ci).astype(F32), 0.0)
    lower = ri >= ci
    pos = _iota((SUPER, 1), 0).astype(F32)
    half = lane < RET_DK
    pair_lo = (lane % RET_DK) < (RET_DK // 2)

    def conv_silu(ref, tail_ref, c0, h, r0, s):
        cols = slice(h * LANES, (h + 1) * LANES)
        cur = ref[pl.ds(r0, SUPER), cols]
        inside = ref[pl.ds(pl.multiple_of(jnp.maximum(r0 - SUBLANES, 0), SUBLANES), SUBLANES), cols]
        prev = jnp.where(s > 0, inside, tail_ref[:, cols])
        prev = jnp.where((s == 0) & (j == 0), 0.0, prev)
        ext = jnp.concatenate([prev, cur], axis=0)
        acc = jnp.zeros((SUPER, LANES), F32)
        for jj in range(GDN_CONV):
            off = SUBLANES - (GDN_CONV - 1) + jj
            acc = acc + cw_ref[jj:jj + 1, c0 + h * LANES:c0 + (h + 1) * LANES] * ext[off:off + SUPER, :]
        return _silu(acc)

    def l2n(u):
        return u * lax.rsqrt(jnp.sum(u * u, axis=-1, keepdims=True) + NORM_EPS)

    heads = range(GDN_HEADS)
    hcols = [slice(h * LANES, (h + 1) * LANES) for h in heads]

    def gdn_heads(r0, s):
        q = [l2n(conv_silu(q_ref, qt_ref, 0, h, r0, s)) * (GDN_DK ** -0.5) for h in heads]
        k = [l2n(conv_silu(k_ref, kt_ref, GDN_QK, h, r0, s)) for h in heads]
        v = [conv_silu(v_ref, vt_ref, 2 * GDN_QK, h, r0, s) for h in heads]
        ba = ba_ref[pl.ds(r0, SUPER), :]
        beta, g_col, gl_col, decay_c, decay_s = [], [], [], [], []
        for h in heads:
            neg_a = hp_ref[0, h]
            dt_b = hp_ref[1, h]
            b_col = jnp.sum(jnp.where(lane == h, ba, 0.0), axis=1, keepdims=True)
            a_col = jnp.sum(jnp.where(lane == h + GDN_HEADS, ba, 0.0), axis=1, keepdims=True)
            a_row = bat_ref[h + GDN_HEADS:h + GDN_HEADS + 1, pl.ds(r0, SUPER)]
            beta.append(jax.nn.sigmoid(b_col))
            la_col = neg_a * _softplus(a_col + dt_b)
            la_row = neg_a * _softplus(a_row + dt_b)
            gc = jnp.sum(jnp.where(causal, la_row, 0.0), axis=1, keepdims=True)
            gr = jnp.sum(jnp.where(causal_t, la_col, 0.0), axis=0, keepdims=True)
            g_col.append(gc)
            gl_col.append(jnp.sum(jnp.where(same, la_row, 0.0), axis=1, keepdims=True))
            dc = jnp.exp(jnp.where(causal, gc - gr, NEG_INF))
            decay_c.append(dc)
            decay_s.append(jnp.where(strict, dc, 0.0))
        kb = [k[h] * beta[h] for h in heads]
        k_bf = [k[h].astype(BF16) for h in heads]
        low = [_dot_nt(kb[h].astype(BF16), k_bf[h]) * decay_s[h] for h in heads]
        p = [eye - low[h] for h in heads]
        sq = low
        for _ in range(shift - 1):
            sq_bf = [sq[h].astype(BF16) for h in heads]
            sq = [_dot(sq_bf[h], sq_bf[h]) for h in heads]
            p = [p[h] + _dot(p[h].astype(BF16), sq[h].astype(BF16)) for h in heads]
        eg = [jnp.exp(g_col[h]) for h in heads]
        rhs = [jnp.concatenate([v[h] * beta[h], kb[h] * eg[h]], axis=1).astype(BF16) for h in heads]
        wk = [_dot(p[h].astype(BF16), rhs[h]) for h in heads]
        w_val = [wk[h][:, :GDN_DV] for h in heads]
        k_cum = [wk[h][:, GDN_DV:].astype(BF16) for h in heads]
        attn = [(_dot_nt(q[h].astype(BF16), k_bf[h]) * decay_c[h]).astype(BF16) for h in heads]
        q_dec = [(q[h] * eg[h]).astype(BF16) for h in heads]
        kd_t = [(k[h] * jnp.exp(gl_col[h] - g_col[h])).T.astype(BF16) for h in heads]
        gl = [jnp.exp(gl_col[h]) for h in heads]
        st = [st_ref[h] for h in heads]
        outs = [[] for _ in heads]
        for c in range(n_chunk):
            lo = c * GDN_CHUNK
            hi = lo + GDN_CHUNK
            st_bf = [st[h].astype(BF16) for h in heads]
            v_new = [w_val[h][lo:hi] - _dot(k_cum[h][lo:hi], st_bf[h]) for h in heads]
            v_pad = []
            for h in heads:
                pieces = [jnp.zeros((GDN_CHUNK, GDN_DV), F32)] * n_chunk
                pieces[c] = v_new[h]
                v_pad.append(jnp.concatenate(pieces, axis=0).astype(BF16))
            for h in heads:
                outs[h].append(_dot(q_dec[h][lo:hi], st_bf[h]) + _dot(attn[h][lo:hi], v_pad[h]))
            st = [st[h] * gl[h][lo:lo + 1, :] + _dot(kd_t[h], v_pad[h]) for h in heads]
        for h in heads:
            st_ref[h] = st[h]
            o = jnp.concatenate(outs[h], axis=0)
            o = o * lax.rsqrt(jnp.mean(o * o, axis=-1, keepdims=True) + NORM_EPS) * nw_ref[...]
            oa_ref[pl.ds(r0, SUPER), hcols[h]] = (o * _silu(z_ref[pl.ds(r0, SUPER), hcols[h]])).astype(oa_ref.dtype)

    def ret_heads(r0):
        rc = rc_ref[pl.ds(r0, SUPER), :]
        rs = rs_ref[pl.ds(r0, SUPER), :]
        lg = [hp_ref[2, h] for h in heads]
        qm, km, vb = [], [], []
        for h in heads:
            x = qkb_ref[pl.ds(r0, SUPER), hcols[h]]
            swapped = jnp.where(pair_lo, pltpu.roll(x, LANES - RET_DK // 2, 1), pltpu.roll(x, RET_DK // 2, 1))
            xr = x * rc + swapped * rs
            qm.append(jnp.where(half, xr, 0.0))
            km.append(jnp.where(half, pltpu.roll(xr, LANES - RET_DK, 1), 0.0))
            vb.append(vb_ref[pl.ds(r0, SUPER), hcols[h]].astype(BF16))
        inner = [(_dot_nt(qm[h].astype(BF16), km[h].astype(BF16))
                  * jnp.where(lower, jnp.exp(lg[h] * dpos), 0.0)).astype(BF16) for h in heads]
        st2 = [st2_ref[h] for h in heads]
        ob = [_dot(inner[h], vb[h])
              + _dot((qm[h] * jnp.exp(lg[h] * (pos + 1.0))).astype(BF16), st2[h].astype(BF16)) for h in heads]
        for h in heads:
            g_chunk = jnp.exp(jnp.full((1, 1), SUPER, F32) * lg[h])
            st2_ref[h] = st2[h] * g_chunk + _dot((km[h] * jnp.exp(lg[h] * (SUPER - 1.0 - pos))).T.astype(BF16), vb[h])
        for h in heads:
            mu = jnp.mean(ob[h], axis=-1, keepdims=True)
            oc = ob[h] - mu
            oc = oc * lax.rsqrt(jnp.mean(oc * oc, axis=-1, keepdims=True) + NORM_EPS)
            ob_ref[pl.ds(r0, SUPER), hcols[h]] = (oc * _silu(gb_ref[pl.ds(r0, SUPER), hcols[h]])).astype(ob_ref.dtype)

    def body(s, carry):
        r0 = pl.multiple_of(s * SUPER, SUPER)
        gdn_heads(r0, s)
        ret_heads(r0)
        return carry

    lax.fori_loop(0, ts // SUPER, body, 0)


def _mixer(proj, bat, hp, conv_w, norm_w, rot_c, rot_s, batch, seq):
    t = proj.shape[0]
    ts = MIX_TS if seq % MIX_TS == 0 else seq
    nj = seq // ts
    grp = lambda c0: pl.BlockSpec((ts, HEAD_W), lambda b, j, c=c0 // GDN_HEADS: (b * nj + j, c))
    tail = lambda c0: pl.BlockSpec(
        (SUBLANES, HEAD_W),
        lambda b, j, c=c0 // GDN_HEADS: (jnp.maximum((b * seq + j * ts) // SUBLANES - 1, 0), c))
    const2 = lambda shape: pl.BlockSpec(shape, lambda b, j: (0, 0))
    tab = pl.BlockSpec((ts, LANES), lambda b, j: (j, 0))
    in_specs = [
        pl.BlockSpec(memory_space=pltpu.SMEM),
        grp(COL_QA), grp(COL_KA), grp(COL_VA), grp(COL_ZA), grp(COL_QKB), grp(COL_VB), grp(COL_GB),
        pl.BlockSpec((ts, LANES), lambda b, j: (b * nj + j, COL_BA)),
        pl.BlockSpec((SUBLANES, ts), lambda b, j: (0, b * nj + j)),
        tail(COL_QA), tail(COL_KA), tail(COL_VA),
        const2(conv_w.shape), const2((1, GDN_DV)), tab, tab,
    ]
    out_spec = pl.BlockSpec((ts, HEAD_W), lambda b, j: (b * nj + j, 0))
    return pl.pallas_call(
        _mixer_kernel,
        grid=(batch, nj),
        in_specs=in_specs,
        out_specs=[out_spec, out_spec],
        out_shape=[jax.ShapeDtypeStruct((t, GDN_VW), BF16), jax.ShapeDtypeStruct((t, RET_VW), BF16)],
        scratch_shapes=[pltpu.VMEM((GDN_HEADS, GDN_DK, GDN_DV), F32),
                        pltpu.VMEM((RET_HEADS, LANES, RET_DV), F32)],
        compiler_params=_params(("parallel", "arbitrary")),
        name="mixer",
    )(hp, proj, proj, proj, proj, proj, proj, proj, proj, bat, proj, proj, proj, conv_w, norm_w, rot_c, rot_s)


def _layer_norm(u, g, b):
    mu = jnp.mean(u, axis=-1, keepdims=True)
    uc = u - mu
    var = jnp.mean(uc * uc, axis=-1, keepdims=True)
    return uc * lax.rsqrt(var + LN_EPS) * g + b


def _post_mix_kernel(x_ref, oa_ref, ob_ref, wo_ref, g_ref, b_ref, wr_ref, rb_ref,
                     h_ref, idx_ref, wts_ref, cnt_ref):
    n_exp = wr_ref.shape[0]
    tm = x_ref.shape[0]
    per_grp = n_exp // N_GROUPS
    mix = _dot(oa_ref[...], wo_ref[:GDN_VW, :]) + _dot(ob_ref[...], wo_ref[GDN_VW:, :])
    hh = _layer_norm(DN_ALPHA * x_ref[...] + mix, g_ref[...], b_ref[...])
    _store_token_tiles(h_ref, hh)
    scores = jax.nn.sigmoid(_dot_nt(wr_ref[...], hh.astype(BF16)))
    choice = scores + rb_ref[...]
    big = float(n_exp)
    io_g = _iota((per_grp, tm), 0).astype(F32)
    grp = []
    for g in range(N_GROUPS):
        blk = choice[g * per_grp:(g + 1) * per_grp, :]
        m1 = jnp.max(blk, axis=0, keepdims=True)
        i1 = jnp.min(jnp.where(blk == m1, io_g, big), axis=0, keepdims=True)
        m2 = jnp.max(jnp.where(io_g == i1, NEG_INF, blk), axis=0, keepdims=True)
        grp.append(m1 + m2)
    gsc = jnp.concatenate(grp, axis=0)
    io8 = _iota((N_GROUPS, tm), 0).astype(F32)
    gsel = jnp.zeros((N_GROUPS, tm), F32)
    for _ in range(TOPK_GROUPS):
        m = jnp.max(gsc, axis=0, keepdims=True)
        i = jnp.min(jnp.where(gsc == m, io8, big), axis=0, keepdims=True)
        hit = io8 == i
        gsel = jnp.where(hit, 1.0, gsel)
        gsc = jnp.where(hit, NEG_INF, gsc)
    masked = jnp.concatenate(
        [jnp.where(gsel[g:g + 1, :] > 0.0, choice[g * per_grp:(g + 1) * per_grp, :], NEG_INF)
         for g in range(N_GROUPS)], axis=0)
    io_e = _iota((n_exp, tm), 0).astype(F32)
    sel = jnp.zeros((n_exp, tm), F32)
    ids, ws = [], []
    for _ in range(TOP_K):
        m = jnp.max(masked, axis=0, keepdims=True)
        i = jnp.min(jnp.where(masked == m, io_e, big), axis=0, keepdims=True)
        hit = io_e == i
        ws.append(jnp.sum(jnp.where(hit, scores, 0.0), axis=0, keepdims=True))
        ids.append(i)
        masked = jnp.where(hit, NEG_INF, masked)
        sel = jnp.where(hit, 1.0, sel)
    w = jnp.concatenate(ws, axis=0)
    wts_ref[...] = w / jnp.sum(w, axis=0, keepdims=True) * ROUTED_SCALE
    idx_ref[...] = jnp.concatenate(ids, axis=0).astype(jnp.int32)

    @pl.when(pl.program_id(0) == 0)
    def _():
        cnt_ref[...] = jnp.zeros_like(cnt_ref)

    cnt_ref[...] += jnp.sum(sel, axis=1, keepdims=True)


def _post_mix(x2, oa, ob, wo_bf, g1, b1, wr_t, rbias, tm):
    t, d = x2.shape
    n_exp = wr_t.shape[0]
    row = lambda w: pl.BlockSpec((tm, w), lambda i: (i, 0))
    const = lambda shape: pl.BlockSpec(shape, lambda i: (0, 0))
    tok = pl.BlockSpec((TOP_K, tm), lambda i: (0, i))
    return pl.pallas_call(
        _post_mix_kernel,
        grid=(t // tm,),
        in_specs=[row(d), row(GDN_VW), row(RET_VW), const(wo_bf.shape), const((1, d)), const((1, d)),
                  const(wr_t.shape), const((n_exp, 1))],
        out_specs=[pl.BlockSpec((tm * SUBLANES, LANES), lambda i: (i, 0)), tok, tok, const((n_exp, 1))],
        out_shape=[jax.ShapeDtypeStruct((t * SUBLANES, LANES), F32),
                   jax.ShapeDtypeStruct((TOP_K, t), jnp.int32),
                   jax.ShapeDtypeStruct((TOP_K, t), F32),
                   jax.ShapeDtypeStruct((n_exp, 1), F32)],
        compiler_params=_params(("arbitrary",)),
        name="post_mix",
    )(x2, oa, ob, wo_bf, g1, b1, wr_t, rbias)


def _dest_kernel(idx_ref, base_ref, dest_ref, carry_ref):
    n_exp = base_ref.shape[0]
    tm = idx_ref.shape[1]

    @pl.when(pl.program_id(0) == 0)
    def _():
        carry_ref[...] = jnp.zeros_like(carry_ref)

    io_e = _iota((n_exp, tm), 0)
    idx = idx_ref[...]
    hits = [io_e == idx[k:k + 1, :] for k in range(TOP_K)]
    sel = jnp.zeros((n_exp, tm), F32)
    for hit in hits:
        sel = jnp.where(hit, 1.0, sel)
    before = jnp.where(_iota((tm, tm), 0) < _iota((tm, tm), 1), 1.0, 0.0).astype(BF16)
    rank = _dot(sel.astype(BF16), before)
    tot = base_ref[...] + carry_ref[...] + rank
    dest = [jnp.sum(jnp.where(hit, tot, 0.0), axis=0, keepdims=True) for hit in hits]
    dest_ref[...] = jnp.concatenate(dest, axis=0).astype(jnp.int32) * SUBLANES
    carry_ref[...] += jnp.sum(sel, axis=1, keepdims=True)


def _dest(idx_t, base, tm):
    t = idx_t.shape[1]
    n_exp = base.shape[0]
    tok = pl.BlockSpec((TOP_K, tm), lambda i: (0, i))
    return pl.pallas_call(
        _dest_kernel,
        grid=(t // tm,),
        in_specs=[tok, pl.BlockSpec((n_exp, 1), lambda i: (0, 0))],
        out_specs=tok,
        out_shape=jax.ShapeDtypeStruct((TOP_K, t), jnp.int32),
        scratch_shapes=[pltpu.VMEM((n_exp, 1), F32)],
        compiler_params=_params(("arbitrary",)),
        name="dest",
    )(idx_t, base)


def _dispatch_kernel(dest_ref, h_ref, xs_ref, zero_ref, sem, zsem):
    tm = h_ref.shape[0] // SUBLANES
    last = pl.num_programs(0) - 1
    pad_row = xs_ref.shape[0] - zero_ref.shape[0]

    @pl.when(pl.program_id(0) == last)
    def _():
        zero_ref[...] = jnp.zeros_like(zero_ref)
        _tile_copy(zero_ref, 0, xs_ref, pad_row, zsem, EXPERT_CHUNK).start()

    def start(t, c):
        for k in range(TOP_K):
            _tile_copy(h_ref, t * SUBLANES, xs_ref, dest_ref[t * TOP_K + k], sem).start()
        return c

    lax.fori_loop(0, tm, start, 0)

    def wait(t, c):
        for k in range(TOP_K):
            _tile_copy(h_ref, t * SUBLANES, xs_ref, dest_ref[t * TOP_K + k], sem).wait()
        return c

    lax.fori_loop(0, tm, wait, 0)

    @pl.when(pl.program_id(0) == last)
    def _():
        _tile_copy(zero_ref, 0, xs_ref, pad_row, zsem, EXPERT_CHUNK).wait()


def _dispatch(dest_t, h_tt, tm):
    t = dest_t.shape[0] // TOP_K
    n_rows = (t * TOP_K + EXPERT_CHUNK) * SUBLANES
    return pl.pallas_call(
        _dispatch_kernel,
        grid=(t // tm,),
        in_specs=[pl.BlockSpec((tm * TOP_K,), lambda i: (i,), memory_space=pltpu.SMEM),
                  pl.BlockSpec((tm * SUBLANES, LANES), lambda i: (i, 0))],
        out_specs=pl.BlockSpec(memory_space=pl.ANY),
        out_shape=jax.ShapeDtypeStruct((n_rows, LANES), F32),
        scratch_shapes=[pltpu.VMEM((EXPERT_CHUNK * SUBLANES, LANES), F32),
                        pltpu.SemaphoreType.DMA(()), pltpu.SemaphoreType.DMA(())],
        compiler_params=_params(("arbitrary",)),
        name="dispatch",
    )(dest_t, h_tt)


def _prefix_pieces(n):
    pieces = []
    size = EXPERT_CHUNK
    while size >= 1:
        shift = int(math.log2(size)) + 1
        pieces.append(((n & size) != 0, (n >> shift) << shift, size))
        size //= 2
    return pieces


def _expert_kernel(start_ref, count_ref, wg_ref, wu_ref, wd_ref, xs_ref, eo_ref,
                   xbuf, obuf, wgb, wub, wdb, pend_ref, rsem, wsem):
    e = pl.program_id(0)
    ch = EXPERT_CHUNK
    s0 = start_ref[e]
    cnt = count_ref[e]
    n_ch = (cnt + ch - 1) // ch

    @pl.when(e == 0)
    def _():
        pend_ref[0] = 0
        pend_ref[1] = 0

    def read(i, slot):
        return _tile_copy(xs_ref, (s0 + i * ch) * SUBLANES, xbuf.at[slot], 0, rsem.at[slot], ch)

    def out_pieces(first_slot, n, wait):
        for pred, off, size in _prefix_pieces(n):
            @pl.when(pred)
            def _(off=off, size=size):
                cp = _tile_copy(obuf, off * SUBLANES, eo_ref, (first_slot + off) * SUBLANES, wsem, size)
                cp.wait() if wait else cp.start()

    def drain():
        out_pieces(pend_ref[0], pend_ref[1], True)
        pend_ref[1] = 0

    @pl.when(n_ch > 0)
    def _():
        read(0, 0).start()
        wgb[...] = wg_ref[0].astype(BF16)
        wub[...] = wu_ref[0].astype(BF16)
        wdb[...] = wd_ref[0].astype(BF16)

    def body(i, c):
        slot = lax.rem(i, 2)
        read(i, slot).wait()

        @pl.when(i + 1 < n_ch)
        def _():
            read(i + 1, 1 - slot).start()

        xb = _load_token_tiles(xbuf, ch, lead=(slot,)).astype(BF16)
        hid = _silu(_dot(xb, wgb[...])) * _dot(xb, wub[...])
        res = _dot(hid.astype(BF16), wdb[...])
        drain()
        _store_token_tiles(obuf, res)
        first_slot = s0 + i * ch
        n = jnp.minimum(ch, cnt - i * ch)
        out_pieces(first_slot, n, False)
        pend_ref[0] = first_slot
        pend_ref[1] = n
        return c

    lax.fori_loop(0, n_ch, body, 0)

    @pl.when(e == pl.num_programs(0) - 1)
    def _():
        drain()


def _experts(starts, counts, xs, wg, wu, wd):
    n_exp, d, d_e = wg.shape
    n_slot = xs.shape[0] // SUBLANES - EXPERT_CHUNK
    rows = EXPERT_CHUNK * SUBLANES
    grid_spec = pltpu.PrefetchScalarGridSpec(
        num_scalar_prefetch=2,
        grid=(n_exp,),
        in_specs=[pl.BlockSpec((1, d, d_e), lambda e, st, ct: (e, 0, 0)),
                  pl.BlockSpec((1, d, d_e), lambda e, st, ct: (e, 0, 0)),
                  pl.BlockSpec((1, d_e, d), lambda e, st, ct: (e, 0, 0)),
                  pl.BlockSpec(memory_space=pl.ANY)],
        out_specs=pl.BlockSpec(memory_space=pl.ANY),
        scratch_shapes=[pltpu.VMEM((2, rows, LANES), F32), pltpu.VMEM((rows, LANES), F32),
                        pltpu.VMEM((d, d_e), BF16), pltpu.VMEM((d, d_e), BF16), pltpu.VMEM((d_e, d), BF16),
                        pltpu.SMEM((2,), jnp.int32),
                        pltpu.SemaphoreType.DMA((2,)), pltpu.SemaphoreType.DMA(())],
    )
    return pl.pallas_call(
        _expert_kernel,
        grid_spec=grid_spec,
        out_shape=jax.ShapeDtypeStruct((n_slot * SUBLANES, LANES), F32),
        compiler_params=_params(("arbitrary",)),
        name="experts",
    )(starts, counts, wg, wu, wd, xs)


def _combine_kernel(dest_ref, wts_ref, h_ref, eo_ref, wgs_ref, wus_ref, wds_ref, g_ref, b_ref,
                    out_ref, gbuf, sem):
    tm = out_ref.shape[0]

    def start(t, c):
        for k in range(TOP_K):
            _tile_copy(eo_ref, dest_ref[t * TOP_K + k], gbuf.at[k], t * SUBLANES, sem).start()
        return c

    lax.fori_loop(0, tm, start, 0)
    hh = _load_token_tiles(h_ref, tm)
    hb = hh.astype(BF16)
    hid = _silu(_dot(hb, wgs_ref[...])) * _dot(hb, wus_ref[...])
    acc = DN_ALPHA * hh + _dot(hid.astype(BF16), wds_ref[...])
    w_t = jnp.concatenate([wts_ref[...], jnp.zeros((LANES - TOP_K, tm), F32)], axis=0).T

    def wait(t, c):
        for k in range(TOP_K):
            _tile_copy(eo_ref, dest_ref[t * TOP_K + k], gbuf.at[k], t * SUBLANES, sem).wait()
        return c

    lax.fori_loop(0, tm, wait, 0)
    for k in range(TOP_K):
        acc = acc + _load_token_tiles(gbuf, tm, lead=(k,)) * w_t[:, k:k + 1]
    out_ref[...] = _layer_norm(acc, g_ref[...], b_ref[...])


def _combine(dest_t, wts_t, h_tt, eo, wgs, wus, wds, g2, b2, tm):
    t = dest_t.shape[0] // TOP_K
    d = wgs.shape[0]
    const = lambda shape: pl.BlockSpec(shape, lambda i: (0, 0))
    return pl.pallas_call(
        _combine_kernel,
        grid=(t // tm,),
        in_specs=[pl.BlockSpec((tm * TOP_K,), lambda i: (i,), memory_space=pltpu.SMEM),
                  pl.BlockSpec((TOP_K, tm), lambda i: (0, i)),
                  pl.BlockSpec((tm * SUBLANES, LANES), lambda i: (i, 0)),
                  pl.BlockSpec(memory_space=pl.ANY),
                  const(wgs.shape), const(wus.shape), const(wds.shape), const((1, d)), const((1, d))],
        out_specs=pl.BlockSpec((tm, d), lambda i: (i, 0)),
        out_shape=jax.ShapeDtypeStruct((t, d), F32),
        scratch_shapes=[pltpu.VMEM((TOP_K, tm * SUBLANES, LANES), F32), pltpu.SemaphoreType.DMA(())],
        compiler_params=_params(("arbitrary",)),
        name="combine",
    )(dest_t, wts_t, h_tt, eo, wgs, wus, wds, g2, b2)


def _regroup_w_in(w):
    d = w.shape[0]
    splits = np.cumsum([GDN_QK, GDN_QK, GDN_VW, GDN_VW, GDN_HEADS, GDN_HEADS, RET_QK, RET_QK, RET_VW])
    qa, ka, va, za, ba, aa, qb, kb, vb, gb = jnp.split(w, splits.tolist(), axis=1)
    ba_blk = jnp.concatenate([ba, aa, jnp.zeros((d, LANES - 2 * GDN_HEADS), w.dtype)], axis=1)
    qb = qb.reshape(d, RET_HEADS, RET_DK // 2, 2)
    kb = kb.reshape(d, RET_HEADS, RET_DK // 2, 2)
    qkb = jnp.concatenate([qb[..., 0], qb[..., 1], kb[..., 0], kb[..., 1]], axis=-1).reshape(d, RET_HEADS * LANES)
    return jnp.concatenate([qa, ka, va, za, qkb, vb, gb, ba_blk], axis=1).astype(BF16)


def _rotary_tables(seq):
    inv = 1.0 / (ROPE_BASE ** jnp.linspace(0.0, 1.0, RET_DK // 2, dtype=F32))
    ang = jnp.arange(seq, dtype=F32)[:, None] * inv[None, :]
    c, s = jnp.cos(ang), jnp.sin(ang)
    ks = RET_DK ** -0.5
    rot_c = jnp.concatenate([c, c, c * ks, c * ks], axis=1)
    rot_s = jnp.concatenate([-s, s, -s * ks, s * ks], axis=1)
    return rot_c, rot_s


def _tile(n, pref):
    return pref if n % pref == 0 else n


def kernel(x, w_in, gdn_conv_w, gdn_a_log, gdn_dt_bias, gdn_norm_w, w_out, ln1_g, ln1_b, w_router, router_bias,
           w_gate_e, w_up_e, w_down_e, w_gate_s, w_up_s, w_down_s, ln2_g, ln2_b):
    batch, seq, d = x.shape
    t = batch * seq
    n_exp = w_router.shape[-1]
    hcur = x.reshape(t, d)
    for l in range(DEPTH):
        w_bf = _regroup_w_in(w_in[l])
        log_gamma = jnp.log(1.0 - 2.0 ** (-5.0 - jnp.arange(RET_HEADS, dtype=F32)))
        hp = jnp.stack([-jnp.exp(gdn_a_log[l].astype(F32)), gdn_dt_bias[l].astype(F32), log_gamma])
        rot_c, rot_s = _rotary_tables(seq)
        proj, bat = _proj(hcur, w_bf, _tile(t, 512))
        oa, ob = _mixer(proj, bat, hp, gdn_conv_w[l].astype(F32), gdn_norm_w[l].reshape(1, -1).astype(F32),
                        rot_c, rot_s, batch, seq)
        hh, idx_t, wts_t, counts = _post_mix(
            hcur, oa, ob, w_out[l].astype(BF16), ln1_g[l].reshape(1, d), ln1_b[l].reshape(1, d),
            w_router[l].T.astype(BF16), router_bias[l].reshape(n_exp, 1).astype(F32), _tile(t, 256))
        base = jnp.cumsum(counts, axis=0) - counts
        dest_t = _dest(idx_t, base, _tile(t, 512)).T.reshape(t * TOP_K)
        xs = _dispatch(dest_t, hh, _tile(t, 512))
        eo = _experts(base.reshape(n_exp).astype(jnp.int32), counts.reshape(n_exp).astype(jnp.int32),
                      xs, w_gate_e[l], w_up_e[l], w_down_e[l])
        hcur = _combine(dest_t, wts_t, hh, eo, w_gate_s[l].astype(BF16), w_up_s[l].astype(BF16),
                        w_down_s[l].astype(BF16), ln2_g[l].reshape(1, d), ln2_b[l].reshape(1, d), _tile(t, 256))
    return hcur.reshape(batch, seq, d)
```

```python
import math

import numpy as np
import jax
import jax.numpy as jnp
from jax import lax
from jax.experimental import pallas as pl
from jax.experimental.pallas import tpu as pltpu

GDN_HEADS = 4
GDN_DK = 128
GDN_DV = 128
GDN_CONV = 4
GDN_CHUNK = 64
RET_HEADS = 4
RET_DK = 64
RET_DV = 128
ROPE_BASE = 10000.0
N_GROUPS = 8
TOPK_GROUPS = 4
TOP_K = 8
ROUTED_SCALE = 2.5
DEPTH = 1
DN_ALPHA = (2.0 * DEPTH) ** 0.25
LN_EPS = 1e-5
NORM_EPS = 1e-6

GDN_QK = GDN_HEADS * GDN_DK
GDN_VW = GDN_HEADS * GDN_DV
RET_QK = RET_HEADS * RET_DK
RET_VW = RET_HEADS * RET_DV

LANES = 128
SUBLANES = 8
VMEM_LIMIT_BYTES = 56 * 1024 * 1024

SUPER = 256
NEG_INF = float("-inf")

COL_QA = 0
COL_KA = COL_QA + GDN_HEADS
COL_VA = COL_KA + GDN_HEADS
COL_ZA = COL_VA + GDN_HEADS
COL_QKB = COL_ZA + GDN_HEADS
COL_VB = COL_QKB + RET_HEADS
COL_GB = COL_VB + RET_HEADS
COL_BA = COL_GB + RET_HEADS
N_COLBLK = COL_BA + 1
PROJ_W = N_COLBLK * LANES
HEAD_W = GDN_HEADS * LANES
MIX_TS = 512
EXPERT_CHUNK = 256
EXPERT_PIECE = 64

BF16 = jnp.bfloat16
F32 = jnp.float32


def _dot(a, b):
    return jnp.dot(a, b, preferred_element_type=F32)


def _dot_nt(a, b):
    return lax.dot_general(a, b, (((1,), (1,)), ((), ())), preferred_element_type=F32)


def _silu(x):
    return x * jax.nn.sigmoid(x)


def _softplus(x):
    return jnp.maximum(x, 0.0) + jnp.log1p(jnp.exp(-jnp.abs(x)))


def _iota(shape, dim, dtype=jnp.int32):
    return lax.broadcasted_iota(dtype, shape, dim)


def _params(sem):
    return pltpu.CompilerParams(dimension_semantics=sem, vmem_limit_bytes=VMEM_LIMIT_BYTES)


def _load_token_tiles(ref, n, lead=()):
    parts = [ref[lead + (pl.ds(s, n, stride=SUBLANES), slice(None))] for s in range(SUBLANES)]
    return jnp.concatenate(parts, axis=1)


def _store_token_tiles(ref, val, lead=()):
    n = val.shape[0]
    for s in range(SUBLANES):
        ref[lead + (pl.ds(s, n, stride=SUBLANES), slice(None))] = val[:, s * LANES:(s + 1) * LANES]


def _tile_copy(src_ref, src_row, dst_ref, dst_row, sem, n_tiles=1):
    rows = n_tiles * SUBLANES
    aligned = lambda r: r if isinstance(r, int) else pl.multiple_of(r, SUBLANES)
    return pltpu.make_async_copy(src_ref.at[pl.ds(aligned(src_row), rows)],
                                 dst_ref.at[pl.ds(aligned(dst_row), rows)], sem)


def _proj_kernel(x_ref, w_ref, proj_ref, bat_ref):
    xb = x_ref[...].astype(BF16)
    step = 4 * LANES
    for j in range(0, PROJ_W, step):
        n = min(step, PROJ_W - j)
        proj_ref[:, j:j + n] = _dot(xb, w_ref[:, j:j + n])
    ba = proj_ref[:, COL_BA * LANES:(COL_BA + 1) * LANES]
    bat_ref[...] = ba.T[0:SUBLANES, :]


def _proj(x2, w_bf, tm):
    t, d = x2.shape
    return pl.pallas_call(
        _proj_kernel,
        grid=(t // tm,),
        in_specs=[pl.BlockSpec((tm, d), lambda i: (i, 0)),
                  pl.BlockSpec((d, PROJ_W), lambda i: (0, 0))],
        out_specs=[pl.BlockSpec((tm, PROJ_W), lambda i: (i, 0)),
                   pl.BlockSpec((SUBLANES, tm), lambda i: (0, i))],
        out_shape=[jax.ShapeDtypeStruct((t, PROJ_W), F32),
                   jax.ShapeDtypeStruct((SUBLANES, t), F32)],
        compiler_params=_params(("parallel",)),
        name="proj",
    )(x2, w_bf)


def _mixer_kernel(hp_ref,
                  q_ref, k_ref, v_ref, z_ref, qkb_ref, vb_ref, gb_ref, ba_ref, bat_ref,
                  qt_ref, kt_ref, vt_ref,
                  cw_ref, nw_ref, rc_ref, rs_ref,
                  oa_ref, ob_ref, st_ref, st2_ref):
    j = pl.program_id(1)
    ts = q_ref.shape[0]
    n_chunk = SUPER // GDN_CHUNK

    @pl.when(j == 0)
    def _():
        st_ref[...] = jnp.zeros_like(st_ref)
        st2_ref[...] = jnp.zeros_like(st2_ref)

    ri = _iota((SUPER, SUPER), 0)
    ci = _iota((SUPER, SUPER), 1)
    shift = int(math.log2(GDN_CHUNK))
    same = (ri >> shift) == (ci >> shift)
    causal = same & (ci <= ri)
    strict = same & (ci < ri)
    causal_t = same & (ri <= ci)
    eye = jnp.where(ri == ci, 1.0, 0.0)
    lane = _iota((SUPER, LANES), 1)
    dpos = jnp.maximum((ri - ci).astype(F32), 0.0)
    lower = ri >= ci
    pos = _iota((SUPER, 1), 0).astype(F32)
    half = lane < RET_DK
    pair_lo = (lane % RET_DK) < (RET_DK // 2)

    def conv_silu(ref, tail_ref, c0, h, r0, s):
        cols = slice(h * LANES, (h + 1) * LANES)
        cur = ref[pl.ds(r0, SUPER), cols]
        inside = ref[pl.ds(pl.multiple_of(jnp.maximum(r0 - SUBLANES, 0), SUBLANES), SUBLANES), cols]
        prev = jnp.where(s > 0, inside, tail_ref[:, cols])
        prev = jnp.where((s == 0) & (j == 0), 0.0, prev)
        ext = jnp.concatenate([prev, cur], axis=0)
        acc = jnp.zeros((SUPER, LANES), F32)
        for jj in range(GDN_CONV):
            off = SUBLANES - (GDN_CONV - 1) + jj
            acc = acc + cw_ref[jj:jj + 1, c0 + h * LANES:c0 + (h + 1) * LANES] * ext[off:off + SUPER, :]
        return _silu(acc)

    def l2n(u):
        return u * lax.rsqrt(jnp.sum(u * u, axis=-1, keepdims=True) + NORM_EPS)

    heads = range(GDN_HEADS)
    hcols = [slice(h * LANES, (h + 1) * LANES) for h in heads]

    def gdn_heads(r0, s):
        q = [l2n(conv_silu(q_ref, qt_ref, 0, h, r0, s)) * (GDN_DK ** -0.5) for h in heads]
        k = [l2n(conv_silu(k_ref, kt_ref, GDN_QK, h, r0, s)) for h in heads]
        v = [conv_silu(v_ref, vt_ref, 2 * GDN_QK, h, r0, s) for h in heads]
        ba = ba_ref[pl.ds(r0, SUPER), :]
        beta, g_col, gl_col, decay_c, decay_s = [], [], [], [], []
        for h in heads:
            neg_a = hp_ref[0, h]
            dt_b = hp_ref[1, h]
            b_col = jnp.sum(jnp.where(lane == h, ba, 0.0), axis=1, keepdims=True)
            a_col = jnp.sum(jnp.where(lane == h + GDN_HEADS, ba, 0.0), axis=1, keepdims=True)
            a_row = bat_ref[h + GDN_HEADS:h + GDN_HEADS + 1, pl.ds(r0, SUPER)]
            beta.append(jax.nn.sigmoid(b_col))
            la_col = neg_a * _softplus(a_col + dt_b)
            la_row = neg_a * _softplus(a_row + dt_b)
            gc = jnp.sum(jnp.where(causal, la_row, 0.0), axis=1, keepdims=True)
            gr = jnp.sum(jnp.where(causal_t, la_col, 0.0), axis=0, keepdims=True)
            g_col.append(gc)
            gl_col.append(jnp.sum(jnp.where(same, la_row, 0.0), axis=1, keepdims=True))
            dc = jnp.exp(jnp.where(causal, gc - gr, NEG_INF))
            decay_c.append(dc)
            decay_s.append(jnp.where(strict, dc, 0.0))
        kb = [k[h] * beta[h] for h in heads]
        k_bf = [k[h].astype(BF16) for h in heads]
        low = [_dot_nt(kb[h].astype(BF16), k_bf[h]) * decay_s[h] for h in heads]
        p = [eye - low[h] for h in heads]
        sq = low
        for _ in range(shift - 1):
            sq_bf = [sq[h].astype(BF16) for h in heads]
            sq = [_dot(sq_bf[h], sq_bf[h]) for h in heads]
            p = [p[h] + _dot(p[h].astype(BF16), sq[h].astype(BF16)) for h in heads]
        eg = [jnp.exp(g_col[h]) for h in heads]
        rhs = [jnp.concatenate([v[h] * beta[h], kb[h] * eg[h]], axis=1).astype(BF16) for h in heads]
        wk = [_dot(p[h].astype(BF16), rhs[h]) for h in heads]
        w_val = [wk[h][:, :GDN_DV] for h in heads]
        k_cum = [wk[h][:, GDN_DV:].astype(BF16) for h in heads]
        attn = [(_dot_nt(q[h].astype(BF16), k_bf[h]) * decay_c[h]).astype(BF16) for h in heads]
        q_dec = [(q[h] * eg[h]).astype(BF16) for h in heads]
        kd_t = [(k[h] * jnp.exp(gl_col[h] - g_col[h])).T.astype(BF16) for h in heads]
        gl = [jnp.exp(gl_col[h]) for h in heads]
        st = [st_ref[h] for h in heads]
        outs = [[] for _ in heads]
        for c in range(n_chunk):
            lo = c * GDN_CHUNK
            hi = lo + GDN_CHUNK
            st_bf = [st[h].astype(BF16) for h in heads]
            v_new = [w_val[h][lo:hi] - _dot(k_cum[h][lo:hi], st_bf[h]) for h in heads]
            v_pad = []
            for h in heads:
                pieces = [jnp.zeros((GDN_CHUNK, GDN_DV), F32)] * n_chunk
                pieces[c] = v_new[h]
                v_pad.append(jnp.concatenate(pieces, axis=0).astype(BF16))
            for h in heads:
                outs[h].append(_dot(q_dec[h][lo:hi], st_bf[h]) + _dot(attn[h][lo:hi], v_pad[h]))
            st = [st[h] * gl[h][lo:lo + 1, :] + _dot(kd_t[h], v_pad[h]) for h in heads]
        for h in heads:
            st_ref[h] = st[h]
            o = jnp.concatenate(outs[h], axis=0)
            o = o * lax.rsqrt(jnp.mean(o * o, axis=-1, keepdims=True) + NORM_EPS) * nw_ref[...]
            oa_ref[pl.ds(r0, SUPER), hcols[h]] = (o * _silu(z_ref[pl.ds(r0, SUPER), hcols[h]])).astype(oa_ref.dtype)

    def ret_heads(r0):
        rc = rc_ref[pl.ds(r0, SUPER), :]
        rs = rs_ref[pl.ds(r0, SUPER), :]
        lg = [hp_ref[2, h] for h in heads]
        qm, km, vb = [], [], []
        for h in heads:
            x = qkb_ref[pl.ds(r0, SUPER), hcols[h]]
            swapped = jnp.where(pair_lo, pltpu.roll(x, LANES - RET_DK // 2, 1), pltpu.roll(x, RET_DK // 2, 1))
            xr = x * rc + swapped * rs
            qm.append(jnp.where(half, xr, 0.0))
            km.append(jnp.where(half, pltpu.roll(xr, LANES - RET_DK, 1), 0.0))
            vb.append(vb_ref[pl.ds(r0, SUPER), hcols[h]].astype(BF16))
        inner = [(_dot_nt(qm[h].astype(BF16), km[h].astype(BF16))
                  * jnp.where(lower, jnp.exp(lg[h] * dpos), 0.0)).astype(BF16) for h in heads]
        st2 = [st2_ref[h] for h in heads]
        ob = [_dot(inner[h], vb[h])
              + _dot((qm[h] * jnp.exp(lg[h] * (pos + 1.0))).astype(BF16), st2[h].astype(BF16)) for h in heads]
        for h in heads:
            g_chunk = jnp.exp(jnp.full((1, 1), SUPER, F32) * lg[h])
            st2_ref[h] = st2[h] * g_chunk + _dot((km[h] * jnp.exp(lg[h] * (SUPER - 1.0 - pos))).T.astype(BF16), vb[h])
        for h in heads:
            mu = jnp.mean(ob[h], axis=-1, keepdims=True)
            oc = ob[h] - mu
            oc = oc * lax.rsqrt(jnp.mean(oc * oc, axis=-1, keepdims=True) + NORM_EPS)
            ob_ref[pl.ds(r0, SUPER), hcols[h]] = (oc * _silu(gb_ref[pl.ds(r0, SUPER), hcols[h]])).astype(ob_ref.dtype)

    def body(s, carry):
        r0 = pl.multiple_of(s * SUPER, SUPER)
        gdn_heads(r0, s)
        ret_heads(r0)
        return carry

    lax.fori_loop(0, ts // SUPER, body, 0)


def _mixer(proj, bat, hp, conv_w, norm_w, rot_c, rot_s, batch, seq):
    t = proj.shape[0]
    ts = MIX_TS if seq % MIX_TS == 0 else seq
    nj = seq // ts
    grp = lambda c0: pl.BlockSpec((ts, HEAD_W), lambda b, j, c=c0 // GDN_HEADS: (b * nj + j, c))
    tail = lambda c0: pl.BlockSpec(
        (SUBLANES, HEAD_W),
        lambda b, j, c=c0 // GDN_HEADS: (jnp.maximum((b * seq + j * ts) // SUBLANES - 1, 0), c))
    const2 = lambda shape: pl.BlockSpec(shape, lambda b, j: (0, 0))
    tab = pl.BlockSpec((ts, LANES), lambda b, j: (j, 0))
    in_specs = [
        pl.BlockSpec(memory_space=pltpu.SMEM),
        grp(COL_QA), grp(COL_KA), grp(COL_VA), grp(COL_ZA), grp(COL_QKB), grp(COL_VB), grp(COL_GB),
        pl.BlockSpec((ts, LANES), lambda b, j: (b * nj + j, COL_BA)),
        pl.BlockSpec((SUBLANES, ts), lambda b, j: (0, b * nj + j)),
        tail(COL_QA), tail(COL_KA), tail(COL_VA),
        const2(conv_w.shape), const2((1, GDN_DV)), tab, tab,
    ]
    out_spec = pl.BlockSpec((ts, HEAD_W), lambda b, j: (b * nj + j, 0))
    return pl.pallas_call(
        _mixer_kernel,
        grid=(batch, nj),
        in_specs=in_specs,
        out_specs=[out_spec, out_spec],
        out_shape=[jax.ShapeDtypeStruct((t, GDN_VW), BF16), jax.ShapeDtypeStruct((t, RET_VW), BF16)],
        scratch_shapes=[pltpu.VMEM((GDN_HEADS, GDN_DK, GDN_DV), F32),
                        pltpu.VMEM((RET_HEADS, LANES, RET_DV), F32)],
        compiler_params=_params(("parallel", "arbitrary")),
        name="mixer",
    )(hp, proj, proj, proj, proj, proj, proj, proj, proj, bat, proj, proj, proj, conv_w, norm_w, rot_c, rot_s)


def _layer_norm(u, g, b):
    mu = jnp.mean(u, axis=-1, keepdims=True)
    uc = u - mu
    var = jnp.mean(uc * uc, axis=-1, keepdims=True)
    return uc * lax.rsqrt(var + LN_EPS) * g + b


def _post_mix_kernel(x_ref, oa_ref, ob_ref, wo_ref, g_ref, b_ref, wr_ref, rb_ref,
                     h_ref, idx_ref, wts_ref, cnt_ref):
    n_exp = wr_ref.shape[0]
    tm = x_ref.shape[0]
    per_grp = n_exp // N_GROUPS
    mix = _dot(oa_ref[...], wo_ref[:GDN_VW, :]) + _dot(ob_ref[...], wo_ref[GDN_VW:, :])
    hh = _layer_norm(DN_ALPHA * x_ref[...] + mix, g_ref[...], b_ref[...])
    _store_token_tiles(h_ref, hh)
    scores = jax.nn.sigmoid(_dot_nt(wr_ref[...], hh.astype(BF16)))
    choice = scores + rb_ref[...]
    big = float(n_exp)
    io_g = _iota((per_grp, tm), 0).astype(F32)
    grp = []
    for g in range(N_GROUPS):
        blk = choice[g * per_grp:(g + 1) * per_grp, :]
        m1 = jnp.max(blk, axis=0, keepdims=True)
        i1 = jnp.min(jnp.where(blk == m1, io_g, big), axis=0, keepdims=True)
        m2 = jnp.max(jnp.where(io_g == i1, NEG_INF, blk), axis=0, keepdims=True)
        grp.append(m1 + m2)
    gsc = jnp.concatenate(grp, axis=0)
    io8 = _iota((N_GROUPS, tm), 0).astype(F32)
    gsel = jnp.zeros((N_GROUPS, tm), F32)
    for _ in range(TOPK_GROUPS):
        m = jnp.max(gsc, axis=0, keepdims=True)
        i = jnp.min(jnp.where(gsc == m, io8, big), axis=0, keepdims=True)
        hit = io8 == i
        gsel = jnp.where(hit, 1.0, gsel)
        gsc = jnp.where(hit, NEG_INF, gsc)
    masked = jnp.concatenate(
        [jnp.where(gsel[g:g + 1, :] > 0.0, choice[g * per_grp:(g + 1) * per_grp, :], NEG_INF)
         for g in range(N_GROUPS)], axis=0)
    io_e = _iota((n_exp, tm), 0).astype(F32)
    sel = jnp.zeros((n_exp, tm), F32)
    ids, ws = [], []
    for _ in range(TOP_K):
        m = jnp.max(masked, axis=0, keepdims=True)
        i = jnp.min(jnp.where(masked == m, io_e, big), axis=0, keepdims=True)
        hit = io_e == i
        ws.append(jnp.sum(jnp.where(hit, scores, 0.0), axis=0, keepdims=True))
        ids.append(i)
        masked = jnp.where(hit, NEG_INF, masked)
        sel = jnp.where(hit, 1.0, sel)
    w = jnp.concatenate(ws, axis=0)
    wts_ref[...] = w / jnp.sum(w, axis=0, keepdims=True) * ROUTED_SCALE
    idx_ref[...] = jnp.concatenate(ids, axis=0).astype(jnp.int32)

    @pl.when(pl.program_id(0) == 0)
    def _():
        cnt_ref[...] = jnp.zeros_like(cnt_ref)

    cnt_ref[...] += jnp.sum(sel, axis=1, keepdims=True)


def _post_mix(x2, oa, ob, wo_bf, g1, b1, wr_t, rbias, tm):
    t, d = x2.shape
    n_exp = wr_t.shape[0]
    row = lambda w: pl.BlockSpec((tm, w), lambda i: (i, 0))
    const = lambda shape: pl.BlockSpec(shape, lambda i: (0, 0))
    tok = pl.BlockSpec((TOP_K, tm), lambda i: (0, i))
    return pl.pallas_call(
        _post_mix_kernel,
        grid=(t // tm,),
        in_specs=[row(d), row(GDN_VW), row(RET_VW), const(wo_bf.shape), const((1, d)), const((1, d)),
                  const(wr_t.shape), const((n_exp, 1))],
        out_specs=[pl.BlockSpec((tm * SUBLANES, LANES), lambda i: (i, 0)), tok, tok, const((n_exp, 1))],
        out_shape=[jax.ShapeDtypeStruct((t * SUBLANES, LANES), F32),
                   jax.ShapeDtypeStruct((TOP_K, t), jnp.int32),
                   jax.ShapeDtypeStruct((TOP_K, t), F32),
                   jax.ShapeDtypeStruct((n_exp, 1), F32)],
        compiler_params=_params(("arbitrary",)),
        name="post_mix",
    )(x2, oa, ob, wo_bf, g1, b1, wr_t, rbias)


def _dest_kernel(idx_ref, base_ref, dest_ref, carry_ref):
    n_exp = base_ref.shape[0]
    tm = idx_ref.shape[1]

    @pl.when(pl.program_id(0) == 0)
    def _():
        carry_ref[...] = jnp.zeros_like(carry_ref)

    io_e = _iota((n_exp, tm), 0)
    idx = idx_ref[...]
    hits = [io_e == idx[k:k + 1, :] for k in range(TOP_K)]
    sel = jnp.zeros((n_exp, tm), F32)
    for hit in hits:
        sel = jnp.where(hit, 1.0, sel)
    before = jnp.where(_iota((tm, tm), 0) < _iota((tm, tm), 1), 1.0, 0.0).astype(BF16)
    rank = _dot(sel.astype(BF16), before)
    tot = base_ref[...] + carry_ref[...] + rank
    dest = [jnp.sum(jnp.where(hit, tot, 0.0), axis=0, keepdims=True) for hit in hits]
    dest_ref[...] = jnp.concatenate(dest, axis=0).astype(jnp.int32) * SUBLANES
    carry_ref[...] += jnp.sum(sel, axis=1, keepdims=True)


def _dest(idx_t, base, tm):
    t = idx_t.shape[1]
    n_exp = base.shape[0]
    tok = pl.BlockSpec((TOP_K, tm), lambda i: (0, i))
    return pl.pallas_call(
        _dest_kernel,
        grid=(t // tm,),
        in_specs=[tok, pl.BlockSpec((n_exp, 1), lambda i: (0, 0))],
        out_specs=tok,
        out_shape=jax.ShapeDtypeStruct((TOP_K, t), jnp.int32),
        scratch_shapes=[pltpu.VMEM((n_exp, 1), F32)],
        compiler_params=_params(("arbitrary",)),
        name="dest",
    )(idx_t, base)


def _dispatch_kernel(dest_ref, h_ref, xs_ref, zero_ref, sem, zsem):
    tm = h_ref.shape[0] // SUBLANES
    last = pl.num_programs(0) - 1
    pad_row = xs_ref.shape[0] - zero_ref.shape[0]

    @pl.when(pl.program_id(0) == last)
    def _():
        zero_ref[...] = jnp.zeros_like(zero_ref)
        _tile_copy(zero_ref, 0, xs_ref, pad_row, zsem, EXPERT_CHUNK).start()

    def start(t, c):
        for k in range(TOP_K):
            _tile_copy(h_ref, t * SUBLANES, xs_ref, dest_ref[t * TOP_K + k], sem).start(priority=k % 2)
        return c

    lax.fori_loop(0, tm, start, 0)

    def wait(t, c):
        for k in range(TOP_K):
            _tile_copy(h_ref, t * SUBLANES, xs_ref, dest_ref[t * TOP_K + k], sem).wait()
        return c

    lax.fori_loop(0, tm, wait, 0)

    @pl.when(pl.program_id(0) == last)
    def _():
        _tile_copy(zero_ref, 0, xs_ref, pad_row, zsem, EXPERT_CHUNK).wait()


def _dispatch(dest_t, h_tt, tm):
    t = dest_t.shape[0] // TOP_K
    n_rows = (t * TOP_K + EXPERT_CHUNK) * SUBLANES
    return pl.pallas_call(
        _dispatch_kernel,
        grid=(t // tm,),
        in_specs=[pl.BlockSpec((tm * TOP_K,), lambda i: (i,), memory_space=pltpu.SMEM),
                  pl.BlockSpec((tm * SUBLANES, LANES), lambda i: (i, 0))],
        out_specs=pl.BlockSpec(memory_space=pl.ANY),
        out_shape=jax.ShapeDtypeStruct((n_rows, LANES), F32),
        scratch_shapes=[pltpu.VMEM((EXPERT_CHUNK * SUBLANES, LANES), F32),
                        pltpu.SemaphoreType.DMA(()), pltpu.SemaphoreType.DMA(())],
        compiler_params=_params(("arbitrary",)),
        name="dispatch",
    )(dest_t, h_tt)


def _chunk_pieces(n):
    pieces = [(j < n // EXPERT_PIECE, j * EXPERT_PIECE, EXPERT_PIECE) for j in range(EXPERT_CHUNK // EXPERT_PIECE)]
    size = EXPERT_PIECE // 2
    while size >= 1:
        shift = int(math.log2(size)) + 1
        pieces.append(((n & size) != 0, (n >> shift) << shift, size))
        size //= 2
    return pieces


ST_CHUNKS = 0
ST_READ = 1
ST_WRITE = 2


def _expert_kernel(start_ref, count_ref, wg_ref, wu_ref, wd_ref, xs_ref, eo_ref,
                   xbuf, obuf, wgb, wub, wdb, st_ref, rsem, wsem):
    e = pl.program_id(0)
    n_exp = pl.num_programs(0)
    ch = EXPERT_CHUNK
    s0 = start_ref[e]
    cnt = count_ref[e]
    n_ch = (cnt + ch - 1) // ch
    next_start = start_ref[jnp.minimum(e + 1, n_exp - 1)]

    def read_parts(first_slot, buf):
        return [_tile_copy(xs_ref, (first_slot + j * EXPERT_PIECE) * SUBLANES, xbuf.at[buf],
                           j * EXPERT_PIECE * SUBLANES, rsem.at[buf], EXPERT_PIECE)
                for j in range(ch // EXPERT_PIECE)]

    def start_read(first_slot, buf):
        for j, cp in enumerate(read_parts(first_slot, buf)):
            cp.start(priority=j % 2)
        st_ref[ST_READ] = 1

    def wait_read(buf):
        for cp in read_parts(0, buf):
            cp.wait()
        st_ref[ST_READ] = 0

    def out_pieces(buf, first_slot, n, wait):
        for j, (pred, off, size) in enumerate(_chunk_pieces(n)):
            @pl.when(pred)
            def _(j=j, off=off, size=size):
                cp = _tile_copy(obuf.at[buf], off * SUBLANES, eo_ref, (first_slot + off) * SUBLANES,
                                wsem.at[buf], size)
                cp.wait() if wait else cp.start(priority=j % 2)

    def drain(buf):
        out_pieces(buf, st_ref[ST_WRITE + 2 * buf], st_ref[ST_WRITE + 2 * buf + 1], True)
        st_ref[ST_WRITE + 2 * buf + 1] = 0

    @pl.when(e == 0)
    def _():
        for i in range(st_ref.shape[0]):
            st_ref[i] = 0
        start_read(0, 0)

    @pl.when(n_ch > 0)
    def _():
        wgb[...] = wg_ref[0].astype(BF16)
        wub[...] = wu_ref[0].astype(BF16)
        wdb[...] = wd_ref[0].astype(BF16)

    def body(i, c):
        buf = lax.rem(st_ref[ST_CHUNKS], 2)
        wait_read(buf)
        last = i + 1 >= n_ch
        nxt = jnp.where(last, next_start, s0 + (i + 1) * ch)

        @pl.when(jnp.logical_or(jnp.logical_not(last), e + 1 < n_exp))
        def _():
            start_read(nxt, 1 - buf)

        xb = _load_token_tiles(xbuf, ch, lead=(buf,)).astype(BF16)
        hid = _silu(_dot(xb, wgb[...])) * _dot(xb, wub[...])
        res = _dot(hid.astype(BF16), wdb[...])
        drain(buf)
        _store_token_tiles(obuf, res, lead=(buf,))
        first_slot = s0 + i * ch
        n = jnp.minimum(ch, cnt - i * ch)
        out_pieces(buf, first_slot, n, False)
        st_ref[ST_WRITE + 2 * buf] = first_slot
        st_ref[ST_WRITE + 2 * buf + 1] = n
        st_ref[ST_CHUNKS] = st_ref[ST_CHUNKS] + 1
        return c

    lax.fori_loop(0, n_ch, body, 0)

    @pl.when(e == n_exp - 1)
    def _():
        drain(0)
        drain(1)

        @pl.when(st_ref[ST_READ] == 1)
        def _():
            wait_read(lax.rem(st_ref[ST_CHUNKS], 2))


def _experts(starts, counts, xs, wg, wu, wd):
    n_exp, d, d_e = wg.shape
    n_slot = xs.shape[0] // SUBLANES - EXPERT_CHUNK
    rows = EXPERT_CHUNK * SUBLANES
    grid_spec = pltpu.PrefetchScalarGridSpec(
        num_scalar_prefetch=2,
        grid=(n_exp,),
        in_specs=[pl.BlockSpec((1, d, d_e), lambda e, st, ct: (e, 0, 0)),
                  pl.BlockSpec((1, d, d_e), lambda e, st, ct: (e, 0, 0)),
                  pl.BlockSpec((1, d_e, d), lambda e, st, ct: (e, 0, 0)),
                  pl.BlockSpec(memory_space=pl.ANY)],
        out_specs=pl.BlockSpec(memory_space=pl.ANY),
        scratch_shapes=[pltpu.VMEM((2, rows, LANES), F32), pltpu.VMEM((2, rows, LANES), F32),
                        pltpu.VMEM((d, d_e), BF16), pltpu.VMEM((d, d_e), BF16), pltpu.VMEM((d_e, d), BF16),
                        pltpu.SMEM((ST_WRITE + 4,), jnp.int32),
                        pltpu.SemaphoreType.DMA((2,)), pltpu.SemaphoreType.DMA((2,))],
    )
    return pl.pallas_call(
        _expert_kernel,
        grid_spec=grid_spec,
        out_shape=jax.ShapeDtypeStruct((n_slot * SUBLANES, LANES), F32),
        compiler_params=_params(("arbitrary",)),
        name="experts",
    )(starts, counts, wg, wu, wd, xs)


def _combine_kernel(dest_ref, wts_ref, h_ref, eo_ref, wgs_ref, wus_ref, wds_ref, g_ref, b_ref,
                    out_ref, gbuf, sem):
    tm = out_ref.shape[0]

    def start(t, c):
        for k in range(TOP_K):
            _tile_copy(eo_ref, dest_ref[t * TOP_K + k], gbuf.at[k], t * SUBLANES, sem).start(priority=k % 2)
        return c

    lax.fori_loop(0, tm, start, 0)
    hh = _load_token_tiles(h_ref, tm)
    hb = hh.astype(BF16)
    hid = _silu(_dot(hb, wgs_ref[...])) * _dot(hb, wus_ref[...])
    acc = DN_ALPHA * hh + _dot(hid.astype(BF16), wds_ref[...])
    w_t = jnp.concatenate([wts_ref[...], jnp.zeros((LANES - TOP_K, tm), F32)], axis=0).T

    def wait(t, c):
        for k in range(TOP_K):
            _tile_copy(eo_ref, dest_ref[t * TOP_K + k], gbuf.at[k], t * SUBLANES, sem).wait()
        return c

    lax.fori_loop(0, tm, wait, 0)
    for k in range(TOP_K):
        acc = acc + _load_token_tiles(gbuf, tm, lead=(k,)) * w_t[:, k:k + 1]
    out_ref[...] = _layer_norm(acc, g_ref[...], b_ref[...])


def _combine(dest_t, wts_t, h_tt, eo, wgs, wus, wds, g2, b2, tm):
    t = dest_t.shape[0] // TOP_K
    d = wgs.shape[0]
    const = lambda shape: pl.BlockSpec(shape, lambda i: (0, 0))
    return pl.pallas_call(
        _combine_kernel,
        grid=(t // tm,),
        in_specs=[pl.BlockSpec((tm * TOP_K,), lambda i: (i,), memory_space=pltpu.SMEM),
                  pl.BlockSpec((TOP_K, tm), lambda i: (0, i)),
                  pl.BlockSpec((tm * SUBLANES, LANES), lambda i: (i, 0)),
                  pl.BlockSpec(memory_space=pl.ANY),
                  const(wgs.shape), const(wus.shape), const(wds.shape), const((1, d)), const((1, d))],
        out_specs=pl.BlockSpec((tm, d), lambda i: (i, 0)),
        out_shape=jax.ShapeDtypeStruct((t, d), F32),
        scratch_shapes=[pltpu.VMEM((TOP_K, tm * SUBLANES, LANES), F32), pltpu.SemaphoreType.DMA(())],
        compiler_params=_params(("arbitrary",)),
        name="combine",
    )(dest_t, wts_t, h_tt, eo, wgs, wus, wds, g2, b2)


def _regroup_w_in(w):
    d = w.shape[0]
    splits = np.cumsum([GDN_QK, GDN_QK, GDN_VW, GDN_VW, GDN_HEADS, GDN_HEADS, RET_QK, RET_QK, RET_VW])
    qa, ka, va, za, ba, aa, qb, kb, vb, gb = jnp.split(w, splits.tolist(), axis=1)
    ba_blk = jnp.concatenate([ba, aa, jnp.zeros((d, LANES - 2 * GDN_HEADS), w.dtype)], axis=1)
    qb = qb.reshape(d, RET_HEADS, RET_DK // 2, 2)
    kb = kb.reshape(d, RET_HEADS, RET_DK // 2, 2)
    qkb = jnp.concatenate([qb[..., 0], qb[..., 1], kb[..., 0], kb[..., 1]], axis=-1).reshape(d, RET_HEADS * LANES)
    return jnp.concatenate([qa, ka, va, za, qkb, vb, gb, ba_blk], axis=1).astype(BF16)


def _rotary_tables(seq):
    inv = 1.0 / (ROPE_BASE ** jnp.linspace(0.0, 1.0, RET_DK // 2, dtype=F32))
    ang = jnp.arange(seq, dtype=F32)[:, None] * inv[None, :]
    c, s = jnp.cos(ang), jnp.sin(ang)
    ks = RET_DK ** -0.5
    rot_c = jnp.concatenate([c, c, c * ks, c * ks], axis=1)
    rot_s = jnp.concatenate([-s, s, -s * ks, s * ks], axis=1)
    return rot_c, rot_s


def _tile(n, pref):
    return pref if n % pref == 0 else n


def kernel(x, w_in, gdn_conv_w, gdn_a_log, gdn_dt_bias, gdn_norm_w, w_out, ln1_g, ln1_b, w_router, router_bias,
           w_gate_e, w_up_e, w_down_e, w_gate_s, w_up_s, w_down_s, ln2_g, ln2_b):
    batch, seq, d = x.shape
    t = batch * seq
    n_exp = w_router.shape[-1]
    hcur = x.reshape(t, d)
    for l in range(DEPTH):
        w_bf = _regroup_w_in(w_in[l])
        log_gamma = jnp.log(1.0 - 2.0 ** (-5.0 - jnp.arange(RET_HEADS, dtype=F32)))
        hp = jnp.stack([-jnp.exp(gdn_a_log[l].astype(F32)), gdn_dt_bias[l].astype(F32), log_gamma])
        rot_c, rot_s = _rotary_tables(seq)
        proj, bat = _proj(hcur, w_bf, _tile(t, 512))
        oa, ob = _mixer(proj, bat, hp, gdn_conv_w[l].astype(F32), gdn_norm_w[l].reshape(1, -1).astype(F32),
                        rot_c, rot_s, batch, seq)
        hh, idx_t, wts_t, counts = _post_mix(
            hcur, oa, ob, w_out[l].astype(BF16), ln1_g[l].reshape(1, d), ln1_b[l].reshape(1, d),
            w_router[l].T.astype(BF16), router_bias[l].reshape(n_exp, 1).astype(F32), _tile(t, 256))
        base = jnp.cumsum(counts, axis=0) - counts
        dest_t = _dest(idx_t, base, _tile(t, 512)).T.reshape(t * TOP_K)
        xs = _dispatch(dest_t, hh, _tile(t, 512))
        eo = _experts(base.reshape(n_exp).astype(jnp.int32), counts.reshape(n_exp).astype(jnp.int32),
                      xs, w_gate_e[l], w_up_e[l], w_down_e[l])
        hcur = _combine(dest_t, wts_t, hh, eo, w_gate_s[l].astype(BF16), w_up_s[l].astype(BF16),
                        w_down_s[l].astype(BF16), ln2_g[l].reshape(1, d), ln2_b[l].reshape(1, d), _tile(t, 256))
    return hcur.reshape(batch, seq, d)
```

```python
import math

import numpy as np
import jax
import jax.numpy as jnp
from jax import lax
from jax.experimental import pallas as pl
from jax.experimental.pallas import tpu as pltpu

GDN_HEADS = 4
GDN_DK = 128
GDN_DV = 128
GDN_CONV = 4
GDN_CHUNK = 64
RET_HEADS = 4
RET_DK = 64
RET_DV = 128
ROPE_BASE = 10000.0
N_GROUPS = 8
TOPK_GROUPS = 4
TOP_K = 8
ROUTED_SCALE = 2.5
DEPTH = 1
DN_ALPHA = (2.0 * DEPTH) ** 0.25
LN_EPS = 1e-5
NORM_EPS = 1e-6

GDN_QK = GDN_HEADS * GDN_DK
GDN_VW = GDN_HEADS * GDN_DV
RET_QK = RET_HEADS * RET_DK
RET_VW = RET_HEADS * RET_DV

LANES = 128
SUBLANES = 8
VMEM_LIMIT_BYTES = 56 * 1024 * 1024

SUPER = 256
NEG_INF = float("-inf")

COL_QA = 0
COL_KA = COL_QA + GDN_HEADS
COL_VA = COL_KA + GDN_HEADS
COL_ZA = COL_VA + GDN_HEADS
COL_QKB = COL_ZA + GDN_HEADS
COL_VB = COL_QKB + RET_HEADS
COL_GB = COL_VB + RET_HEADS
COL_BA = COL_GB + RET_HEADS
N_COLBLK = COL_BA + 1
PROJ_W = N_COLBLK * LANES
HEAD_W = GDN_HEADS * LANES
MIX_TS = 512
EXPERT_CHUNK = 256
EXPERT_PIECE = 64
PACK_ROWS = 4

BF16 = jnp.bfloat16
F32 = jnp.float32


def _dot(a, b):
    return jnp.dot(a, b, preferred_element_type=F32)


def _dot_nt(a, b):
    return lax.dot_general(a, b, (((1,), (1,)), ((), ())), preferred_element_type=F32)


def _silu(x):
    return x * jax.nn.sigmoid(x)


def _softplus(x):
    return jnp.maximum(x, 0.0) + jnp.log1p(jnp.exp(-jnp.abs(x)))


def _iota(shape, dim, dtype=jnp.int32):
    return lax.broadcasted_iota(dtype, shape, dim)


def _params(sem):
    return pltpu.CompilerParams(dimension_semantics=sem, vmem_limit_bytes=VMEM_LIMIT_BYTES)


def _load_token_tiles(ref, n, lead=()):
    parts = [ref[lead + (pl.ds(s, n, stride=SUBLANES), slice(None))] for s in range(SUBLANES)]
    return jnp.concatenate(parts, axis=1)


def _store_token_tiles(ref, val, lead=()):
    n = val.shape[0]
    for s in range(SUBLANES):
        ref[lead + (pl.ds(s, n, stride=SUBLANES), slice(None))] = val[:, s * LANES:(s + 1) * LANES]


def _tile_copy(src_ref, src_row, dst_ref, dst_row, sem, n_slots=1, slot_rows=SUBLANES):
    rows = n_slots * slot_rows
    aligned = lambda r: r if isinstance(r, int) else pl.multiple_of(r, slot_rows)
    return pltpu.make_async_copy(src_ref.at[pl.ds(aligned(src_row), rows)],
                                 dst_ref.at[pl.ds(aligned(dst_row), rows)], sem)


def _store_packed(ref, val, lead=()):
    n, width = val.shape
    bits = pltpu.bitcast(val.astype(BF16).astype(F32), jnp.uint32)
    words = (bits[:, width // 2:] & jnp.uint32(0xFFFF0000)) | (bits[:, :width // 2] >> 16)
    for s in range(PACK_ROWS):
        ref[lead + (pl.ds(s, n, stride=PACK_ROWS), slice(None))] = words[:, s * LANES:(s + 1) * LANES]


def _load_packed(ref, n, lead=(), dtype=F32):
    words = [ref[lead + (pl.ds(s, n, stride=PACK_ROWS), slice(None))] for s in range(PACK_ROWS)]
    lo = [pltpu.bitcast(w << 16, F32) for w in words]
    hi = [pltpu.bitcast(w & jnp.uint32(0xFFFF0000), F32) for w in words]
    return jnp.concatenate(lo + hi, axis=1).astype(dtype)


def _proj_kernel(x_ref, w_ref, proj_ref, bat_ref):
    xb = x_ref[...].astype(BF16)
    step = 4 * LANES
    for j in range(0, PROJ_W, step):
        n = min(step, PROJ_W - j)
        proj_ref[:, j:j + n] = _dot(xb, w_ref[:, j:j + n])
    ba = proj_ref[:, COL_BA * LANES:(COL_BA + 1) * LANES]
    bat_ref[...] = ba.T[0:SUBLANES, :]


def _proj(x2, w_bf, tm):
    t, d = x2.shape
    return pl.pallas_call(
        _proj_kernel,
        grid=(t // tm,),
        in_specs=[pl.BlockSpec((tm, d), lambda i: (i, 0)),
                  pl.BlockSpec((d, PROJ_W), lambda i: (0, 0))],
        out_specs=[pl.BlockSpec((tm, PROJ_W), lambda i: (i, 0)),
                   pl.BlockSpec((SUBLANES, tm), lambda i: (0, i))],
        out_shape=[jax.ShapeDtypeStruct((t, PROJ_W), F32),
                   jax.ShapeDtypeStruct((SUBLANES, t), F32)],
        compiler_params=_params(("parallel",)),
        name="proj",
    )(x2, w_bf)


def _mixer_kernel(hp_ref,
                  q_ref, k_ref, v_ref, z_ref, qkb_ref, vb_ref, gb_ref, ba_ref, bat_ref,
                  qt_ref, kt_ref, vt_ref,
                  cw_ref, nw_ref, rc_ref, rs_ref,
                  oa_ref, ob_ref, st_ref, st2_ref):
    j = pl.program_id(1)
    ts = q_ref.shape[0]
    n_chunk = SUPER // GDN_CHUNK

    @pl.when(j == 0)
    def _():
        st_ref[...] = jnp.zeros_like(st_ref)
        st2_ref[...] = jnp.zeros_like(st2_ref)

    ri = _iota((SUPER, SUPER), 0)
    ci = _iota((SUPER, SUPER), 1)
    shift = int(math.log2(GDN_CHUNK))
    same = (ri >> shift) == (ci >> shift)
    causal = same & (ci <= ri)
    strict = same & (ci < ri)
    causal_t = same & (ri <= ci)
    eye = jnp.where(ri == ci, 1.0, 0.0)
    lane = _iota((SUPER, LANES), 1)
    dpos = jnp.maximum((ri - ci).astype(F32), 0.0)
    lower = ri >= ci
    pos = _iota((SUPER, 1), 0).astype(F32)
    half = lane < RET_DK
    pair_lo = (lane % RET_DK) < (RET_DK // 2)

    def conv_silu(ref, tail_ref, c0, h, r0, s):
        cols = slice(h * LANES, (h + 1) * LANES)
        cur = ref[pl.ds(r0, SUPER), cols]
        inside = ref[pl.ds(pl.multiple_of(jnp.maximum(r0 - SUBLANES, 0), SUBLANES), SUBLANES), cols]
        prev = jnp.where(s > 0, inside, tail_ref[:, cols])
        prev = jnp.where((s == 0) & (j == 0), 0.0, prev)
        ext = jnp.concatenate([prev, cur], axis=0)
        acc = jnp.zeros((SUPER, LANES), F32)
        for jj in range(GDN_CONV):
            off = SUBLANES - (GDN_CONV - 1) + jj
            acc = acc + cw_ref[jj:jj + 1, c0 + h * LANES:c0 + (h + 1) * LANES] * ext[off:off + SUPER, :]
        return _silu(acc)

    def l2n(u):
        return u * lax.rsqrt(jnp.sum(u * u, axis=-1, keepdims=True) + NORM_EPS)

    heads = range(GDN_HEADS)
    hcols = [slice(h * LANES, (h + 1) * LANES) for h in heads]

    def gdn_heads(r0, s):
        q = [l2n(conv_silu(q_ref, qt_ref, 0, h, r0, s)) * (GDN_DK ** -0.5) for h in heads]
        k = [l2n(conv_silu(k_ref, kt_ref, GDN_QK, h, r0, s)) for h in heads]
        v = [conv_silu(v_ref, vt_ref, 2 * GDN_QK, h, r0, s) for h in heads]
        ba = ba_ref[pl.ds(r0, SUPER), :]
        beta, g_col, gl_col, decay_c, decay_s = [], [], [], [], []
        for h in heads:
            neg_a = hp_ref[0, h]
            dt_b = hp_ref[1, h]
            b_col = jnp.sum(jnp.where(lane == h, ba, 0.0), axis=1, keepdims=True)
            a_col = jnp.sum(jnp.where(lane == h + GDN_HEADS, ba, 0.0), axis=1, keepdims=True)
            a_row = bat_ref[h + GDN_HEADS:h + GDN_HEADS + 1, pl.ds(r0, SUPER)]
            beta.append(jax.nn.sigmoid(b_col))
            la_col = neg_a * _softplus(a_col + dt_b)
            la_row = neg_a * _softplus(a_row + dt_b)
            gc = jnp.sum(jnp.where(causal, la_row, 0.0), axis=1, keepdims=True)
            gr = jnp.sum(jnp.where(causal_t, la_col, 0.0), axis=0, keepdims=True)
            g_col.append(gc)
            gl_col.append(jnp.sum(jnp.where(same, la_row, 0.0), axis=1, keepdims=True))
            dc = jnp.exp(jnp.where(causal, gc - gr, NEG_INF))
            decay_c.append(dc)
            decay_s.append(jnp.where(strict, dc, 0.0))
        kb = [k[h] * beta[h] for h in heads]
        k_bf = [k[h].astype(BF16) for h in heads]
        low = [_dot_nt(kb[h].astype(BF16), k_bf[h]) * decay_s[h] for h in heads]
        p = [eye - low[h] for h in heads]
        sq = low
        for _ in range(shift - 1):
            sq_bf = [sq[h].astype(BF16) for h in heads]
            sq = [_dot(sq_bf[h], sq_bf[h]) for h in heads]
            p = [p[h] + _dot(p[h].astype(BF16), sq[h].astype(BF16)) for h in heads]
        eg = [jnp.exp(g_col[h]) for h in heads]
        rhs = [jnp.concatenate([v[h] * beta[h], kb[h] * eg[h]], axis=1).astype(BF16) for h in heads]
        wk = [_dot(p[h].astype(BF16), rhs[h]) for h in heads]
        w_val = [wk[h][:, :GDN_DV] for h in heads]
        k_cum = [wk[h][:, GDN_DV:].astype(BF16) for h in heads]
        attn = [(_dot_nt(q[h].astype(BF16), k_bf[h]) * decay_c[h]).astype(BF16) for h in heads]
        q_dec = [(q[h] * eg[h]).astype(BF16) for h in heads]
        kd_t = [(k[h] * jnp.exp(gl_col[h] - g_col[h])).T.astype(BF16) for h in heads]
        gl = [jnp.exp(gl_col[h]) for h in heads]
        st = [st_ref[h] for h in heads]
        outs = [[] for _ in heads]
        for c in range(n_chunk):
            lo = c * GDN_CHUNK
            hi = lo + GDN_CHUNK
            st_bf = [st[h].astype(BF16) for h in heads]
            v_new = [w_val[h][lo:hi] - _dot(k_cum[h][lo:hi], st_bf[h]) for h in heads]
            v_pad = []
            for h in heads:
                pieces = [jnp.zeros((GDN_CHUNK, GDN_DV), F32)] * n_chunk
                pieces[c] = v_new[h]
                v_pad.append(jnp.concatenate(pieces, axis=0).astype(BF16))
            for h in heads:
                outs[h].append(_dot(q_dec[h][lo:hi], st_bf[h]) + _dot(attn[h][lo:hi], v_pad[h]))
            st = [st[h] * gl[h][lo:lo + 1, :] + _dot(kd_t[h], v_pad[h]) for h in heads]
        for h in heads:
            st_ref[h] = st[h]
            o = jnp.concatenate(outs[h], axis=0)
            o = o * lax.rsqrt(jnp.mean(o * o, axis=-1, keepdims=True) + NORM_EPS) * nw_ref[...]
            oa_ref[pl.ds(r0, SUPER), hcols[h]] = (o * _silu(z_ref[pl.ds(r0, SUPER), hcols[h]])).astype(oa_ref.dtype)

    def ret_heads(r0):
        rc = rc_ref[pl.ds(r0, SUPER), :]
        rs = rs_ref[pl.ds(r0, SUPER), :]
        lg = [hp_ref[2, h] for h in heads]
        qm, km, vb = [], [], []
        for h in heads:
            x = qkb_ref[pl.ds(r0, SUPER), hcols[h]]
            swapped = jnp.where(pair_lo, pltpu.roll(x, LANES - RET_DK // 2, 1), pltpu.roll(x, RET_DK // 2, 1))
            xr = x * rc + swapped * rs
            qm.append(jnp.where(half, xr, 0.0))
            km.append(jnp.where(half, pltpu.roll(xr, LANES - RET_DK, 1), 0.0))
            vb.append(vb_ref[pl.ds(r0, SUPER), hcols[h]].astype(BF16))
        inner = [(_dot_nt(qm[h].astype(BF16), km[h].astype(BF16))
                  * jnp.where(lower, jnp.exp(lg[h] * dpos), 0.0)).astype(BF16) for h in heads]
        st2 = [st2_ref[h] for h in heads]
        ob = [_dot(inner[h], vb[h])
              + _dot((qm[h] * jnp.exp(lg[h] * (pos + 1.0))).astype(BF16), st2[h].astype(BF16)) for h in heads]
        for h in heads:
            g_chunk = jnp.exp(jnp.full((1, 1), SUPER, F32) * lg[h])
            st2_ref[h] = st2[h] * g_chunk + _dot((km[h] * jnp.exp(lg[h] * (SUPER - 1.0 - pos))).T.astype(BF16), vb[h])
        for h in heads:
            mu = jnp.mean(ob[h], axis=-1, keepdims=True)
            oc = ob[h] - mu
            oc = oc * lax.rsqrt(jnp.mean(oc * oc, axis=-1, keepdims=True) + NORM_EPS)
            ob_ref[pl.ds(r0, SUPER), hcols[h]] = (oc * _silu(gb_ref[pl.ds(r0, SUPER), hcols[h]])).astype(ob_ref.dtype)

    def body(s, carry):
        r0 = pl.multiple_of(s * SUPER, SUPER)
        gdn_heads(r0, s)
        ret_heads(r0)
        return carry

    lax.fori_loop(0, ts // SUPER, body, 0)


def _mixer(proj, bat, hp, conv_w, norm_w, rot_c, rot_s, batch, seq):
    t = proj.shape[0]
    ts = MIX_TS if seq % MIX_TS == 0 else seq
    nj = seq // ts
    grp = lambda c0: pl.BlockSpec((ts, HEAD_W), lambda b, j, c=c0 // GDN_HEADS: (b * nj + j, c))
    tail = lambda c0: pl.BlockSpec(
        (SUBLANES, HEAD_W),
        lambda b, j, c=c0 // GDN_HEADS: (jnp.maximum((b * seq + j * ts) // SUBLANES - 1, 0), c))
    const2 = lambda shape: pl.BlockSpec(shape, lambda b, j: (0, 0))
    tab = pl.BlockSpec((ts, LANES), lambda b, j: (j, 0))
    in_specs = [
        pl.BlockSpec(memory_space=pltpu.SMEM),
        grp(COL_QA), grp(COL_KA), grp(COL_VA), grp(COL_ZA), grp(COL_QKB), grp(COL_VB), grp(COL_GB),
        pl.BlockSpec((ts, LANES), lambda b, j: (b * nj + j, COL_BA)),
        pl.BlockSpec((SUBLANES, ts), lambda b, j: (0, b * nj + j)),
        tail(COL_QA), tail(COL_KA), tail(COL_VA),
        const2(conv_w.shape), const2((1, GDN_DV)), tab, tab,
    ]
    out_spec = pl.BlockSpec((ts, HEAD_W), lambda b, j: (b * nj + j, 0))
    return pl.pallas_call(
        _mixer_kernel,
        grid=(batch, nj),
        in_specs=in_specs,
        out_specs=[out_spec, out_spec],
        out_shape=[jax.ShapeDtypeStruct((t, GDN_VW), BF16), jax.ShapeDtypeStruct((t, RET_VW), BF16)],
        scratch_shapes=[pltpu.VMEM((GDN_HEADS, GDN_DK, GDN_DV), F32),
                        pltpu.VMEM((RET_HEADS, LANES, RET_DV), F32)],
        compiler_params=_params(("parallel", "arbitrary")),
        name="mixer",
    )(hp, proj, proj, proj, proj, proj, proj, proj, proj, bat, proj, proj, proj, conv_w, norm_w, rot_c, rot_s)


def _layer_norm(u, g, b):
    mu = jnp.mean(u, axis=-1, keepdims=True)
    uc = u - mu
    var = jnp.mean(uc * uc, axis=-1, keepdims=True)
    return uc * lax.rsqrt(var + LN_EPS) * g + b


def _post_mix_kernel(x_ref, oa_ref, ob_ref, wo_ref, g_ref, b_ref, wr_ref, rb_ref,
                     h_ref, hp_ref, idx_ref, wts_ref, cnt_ref):
    n_exp = wr_ref.shape[0]
    tm = x_ref.shape[0]
    per_grp = n_exp // N_GROUPS
    mix = _dot(oa_ref[...], wo_ref[:GDN_VW, :]) + _dot(ob_ref[...], wo_ref[GDN_VW:, :])
    hh = _layer_norm(DN_ALPHA * x_ref[...] + mix, g_ref[...], b_ref[...])
    _store_token_tiles(h_ref, hh)
    _store_packed(hp_ref, hh)
    scores = jax.nn.sigmoid(_dot_nt(wr_ref[...], hh.astype(BF16)))
    choice = scores + rb_ref[...]
    big = float(n_exp)
    io_g = _iota((per_grp, tm), 0).astype(F32)
    grp = []
    for g in range(N_GROUPS):
        blk = choice[g * per_grp:(g + 1) * per_grp, :]
        m1 = jnp.max(blk, axis=0, keepdims=True)
        i1 = jnp.min(jnp.where(blk == m1, io_g, big), axis=0, keepdims=True)
        m2 = jnp.max(jnp.where(io_g == i1, NEG_INF, blk), axis=0, keepdims=True)
        grp.append(m1 + m2)
    gsc = jnp.concatenate(grp, axis=0)
    io8 = _iota((N_GROUPS, tm), 0).astype(F32)
    gsel = jnp.zeros((N_GROUPS, tm), F32)
    for _ in range(TOPK_GROUPS):
        m = jnp.max(gsc, axis=0, keepdims=True)
        i = jnp.min(jnp.where(gsc == m, io8, big), axis=0, keepdims=True)
        hit = io8 == i
        gsel = jnp.where(hit, 1.0, gsel)
        gsc = jnp.where(hit, NEG_INF, gsc)
    masked = jnp.concatenate(
        [jnp.where(gsel[g:g + 1, :] > 0.0, choice[g * per_grp:(g + 1) * per_grp, :], NEG_INF)
         for g in range(N_GROUPS)], axis=0)
    io_e = _iota((n_exp, tm), 0).astype(F32)
    sel = jnp.zeros((n_exp, tm), F32)
    ids, ws = [], []
    for _ in range(TOP_K):
        m = jnp.max(masked, axis=0, keepdims=True)
        i = jnp.min(jnp.where(masked == m, io_e, big), axis=0, keepdims=True)
        hit = io_e == i
        ws.append(jnp.sum(jnp.where(hit, scores, 0.0), axis=0, keepdims=True))
        ids.append(i)
        masked = jnp.where(hit, NEG_INF, masked)
        sel = jnp.where(hit, 1.0, sel)
    w = jnp.concatenate(ws, axis=0)
    wts_ref[...] = w / jnp.sum(w, axis=0, keepdims=True) * ROUTED_SCALE
    idx_ref[...] = jnp.concatenate(ids, axis=0).astype(jnp.int32)

    @pl.when(pl.program_id(0) == 0)
    def _():
        cnt_ref[...] = jnp.zeros_like(cnt_ref)

    cnt_ref[...] += jnp.sum(sel, axis=1, keepdims=True)


def _post_mix(x2, oa, ob, wo_bf, g1, b1, wr_t, rbias, tm):
    t, d = x2.shape
    n_exp = wr_t.shape[0]
    row = lambda w: pl.BlockSpec((tm, w), lambda i: (i, 0))
    const = lambda shape: pl.BlockSpec(shape, lambda i: (0, 0))
    tok = pl.BlockSpec((TOP_K, tm), lambda i: (0, i))
    return pl.pallas_call(
        _post_mix_kernel,
        grid=(t // tm,),
        in_specs=[row(d), row(GDN_VW), row(RET_VW), const(wo_bf.shape), const((1, d)), const((1, d)),
                  const(wr_t.shape), const((n_exp, 1))],
        out_specs=[pl.BlockSpec((tm * SUBLANES, LANES), lambda i: (i, 0)),
                   pl.BlockSpec((tm * PACK_ROWS, LANES), lambda i: (i, 0)), tok, tok, const((n_exp, 1))],
        out_shape=[jax.ShapeDtypeStruct((t * SUBLANES, LANES), F32),
                   jax.ShapeDtypeStruct((t * PACK_ROWS, LANES), jnp.uint32),
                   jax.ShapeDtypeStruct((TOP_K, t), jnp.int32),
                   jax.ShapeDtypeStruct((TOP_K, t), F32),
                   jax.ShapeDtypeStruct((n_exp, 1), F32)],
        compiler_params=_params(("arbitrary",)),
        name="post_mix",
    )(x2, oa, ob, wo_bf, g1, b1, wr_t, rbias)


def _dest_kernel(idx_ref, base_ref, dest_ref, carry_ref):
    n_exp = base_ref.shape[0]
    tm = idx_ref.shape[1]

    @pl.when(pl.program_id(0) == 0)
    def _():
        carry_ref[...] = jnp.zeros_like(carry_ref)

    io_e = _iota((n_exp, tm), 0)
    idx = idx_ref[...]
    hits = [io_e == idx[k:k + 1, :] for k in range(TOP_K)]
    sel = jnp.zeros((n_exp, tm), F32)
    for hit in hits:
        sel = jnp.where(hit, 1.0, sel)
    before = jnp.where(_iota((tm, tm), 0) < _iota((tm, tm), 1), 1.0, 0.0).astype(BF16)
    rank = _dot(sel.astype(BF16), before)
    tot = base_ref[...] + carry_ref[...] + rank
    dest = [jnp.sum(jnp.where(hit, tot, 0.0), axis=0, keepdims=True) for hit in hits]
    dest_ref[...] = jnp.concatenate(dest, axis=0).astype(jnp.int32) * PACK_ROWS
    carry_ref[...] += jnp.sum(sel, axis=1, keepdims=True)


def _dest(idx_t, base, tm):
    t = idx_t.shape[1]
    n_exp = base.shape[0]
    tok = pl.BlockSpec((TOP_K, tm), lambda i: (0, i))
    return pl.pallas_call(
        _dest_kernel,
        grid=(t // tm,),
        in_specs=[tok, pl.BlockSpec((n_exp, 1), lambda i: (0, 0))],
        out_specs=tok,
        out_shape=jax.ShapeDtypeStruct((TOP_K, t), jnp.int32),
        scratch_shapes=[pltpu.VMEM((n_exp, 1), F32)],
        compiler_params=_params(("arbitrary",)),
        name="dest",
    )(idx_t, base)


def _dispatch_kernel(dest_ref, h_ref, xs_ref, zero_ref, sem, zsem):
    tm = h_ref.shape[0] // PACK_ROWS
    last = pl.num_programs(0) - 1
    pad_row = xs_ref.shape[0] - zero_ref.shape[0]

    @pl.when(pl.program_id(0) == last)
    def _():
        zero_ref[...] = jnp.zeros_like(zero_ref)
        _tile_copy(zero_ref, 0, xs_ref, pad_row, zsem, EXPERT_CHUNK, PACK_ROWS).start()

    def start(t, c):
        for k in range(TOP_K):
            _tile_copy(h_ref, t * PACK_ROWS, xs_ref, dest_ref[t * TOP_K + k], sem, 1, PACK_ROWS).start(priority=k % 2)
        return c

    lax.fori_loop(0, tm, start, 0)

    def wait(t, c):
        for k in range(TOP_K):
            _tile_copy(h_ref, t * PACK_ROWS, xs_ref, dest_ref[t * TOP_K + k], sem, 1, PACK_ROWS).wait()
        return c

    lax.fori_loop(0, tm, wait, 0)

    @pl.when(pl.program_id(0) == last)
    def _():
        _tile_copy(zero_ref, 0, xs_ref, pad_row, zsem, EXPERT_CHUNK, PACK_ROWS).wait()


def _dispatch(dest_t, h_pk, tm):
    t = dest_t.shape[0] // TOP_K
    n_rows = (t * TOP_K + EXPERT_CHUNK) * PACK_ROWS
    return pl.pallas_call(
        _dispatch_kernel,
        grid=(t // tm,),
        in_specs=[pl.BlockSpec((tm * TOP_K,), lambda i: (i,), memory_space=pltpu.SMEM),
                  pl.BlockSpec((tm * PACK_ROWS, LANES), lambda i: (i, 0))],
        out_specs=pl.BlockSpec(memory_space=pl.ANY),
        out_shape=jax.ShapeDtypeStruct((n_rows, LANES), jnp.uint32),
        scratch_shapes=[pltpu.VMEM((EXPERT_CHUNK * PACK_ROWS, LANES), jnp.uint32),
                        pltpu.SemaphoreType.DMA(()), pltpu.SemaphoreType.DMA(())],
        compiler_params=_params(("arbitrary",)),
        name="dispatch",
    )(dest_t, h_pk)


def _chunk_pieces(n):
    pieces = [(j < n // EXPERT_PIECE, j * EXPERT_PIECE, EXPERT_PIECE) for j in range(EXPERT_CHUNK // EXPERT_PIECE)]
    size = EXPERT_PIECE // 2
    while size >= 1:
        shift = int(math.log2(size)) + 1
        pieces.append(((n & size) != 0, (n >> shift) << shift, size))
        size //= 2
    return pieces


ST_CHUNKS = 0
ST_READ = 1
ST_WRITE = 2


def _expert_kernel(start_ref, count_ref, wg_ref, wu_ref, wd_ref, xs_ref, eo_ref,
                   xbuf, obuf, wgb, wub, wdb, st_ref, rsem, wsem):
    e = pl.program_id(0)
    n_exp = pl.num_programs(0)
    ch = EXPERT_CHUNK
    s0 = start_ref[e]
    cnt = count_ref[e]
    n_ch = (cnt + ch - 1) // ch
    next_start = start_ref[jnp.minimum(e + 1, n_exp - 1)]

    def read_parts(first_slot, buf):
        return [_tile_copy(xs_ref, (first_slot + j * EXPERT_PIECE) * PACK_ROWS, xbuf.at[buf],
                           j * EXPERT_PIECE * PACK_ROWS, rsem.at[buf], EXPERT_PIECE, PACK_ROWS)
                for j in range(ch // EXPERT_PIECE)]

    def start_read(first_slot, buf):
        for j, cp in enumerate(read_parts(first_slot, buf)):
            cp.start(priority=j % 2)
        st_ref[ST_READ] = 1

    def wait_read(buf):
        for cp in read_parts(0, buf):
            cp.wait()
        st_ref[ST_READ] = 0

    def out_pieces(buf, first_slot, n, wait):
        for j, (pred, off, size) in enumerate(_chunk_pieces(n)):
            @pl.when(pred)
            def _(j=j, off=off, size=size):
                cp = _tile_copy(obuf.at[buf], off * PACK_ROWS, eo_ref, (first_slot + off) * PACK_ROWS,
                                wsem.at[buf], size, PACK_ROWS)
                cp.wait() if wait else cp.start(priority=j % 2)

    def drain(buf):
        out_pieces(buf, st_ref[ST_WRITE + 2 * buf], st_ref[ST_WRITE + 2 * buf + 1], True)
        st_ref[ST_WRITE + 2 * buf + 1] = 0

    @pl.when(e == 0)
    def _():
        for i in range(st_ref.shape[0]):
            st_ref[i] = 0
        start_read(0, 0)

    @pl.when(n_ch > 0)
    def _():
        wgb[...] = wg_ref[0].astype(BF16)
        wub[...] = wu_ref[0].astype(BF16)
        wdb[...] = wd_ref[0].astype(BF16)

    def body(i, c):
        buf = lax.rem(st_ref[ST_CHUNKS], 2)
        wait_read(buf)
        last = i + 1 >= n_ch
        nxt = jnp.where(last, next_start, s0 + (i + 1) * ch)

        @pl.when(jnp.logical_or(jnp.logical_not(last), e + 1 < n_exp))
        def _():
            start_read(nxt, 1 - buf)

        xb = _load_packed(xbuf, ch, lead=(buf,), dtype=BF16)
        hid = _silu(_dot(xb, wgb[...])) * _dot(xb, wub[...])
        res = _dot(hid.astype(BF16), wdb[...])
        drain(buf)
        _store_packed(obuf, res, lead=(buf,))
        first_slot = s0 + i * ch
        n = jnp.minimum(ch, cnt - i * ch)
        out_pieces(buf, first_slot, n, False)
        st_ref[ST_WRITE + 2 * buf] = first_slot
        st_ref[ST_WRITE + 2 * buf + 1] = n
        st_ref[ST_CHUNKS] = st_ref[ST_CHUNKS] + 1
        return c

    lax.fori_loop(0, n_ch, body, 0)

    @pl.when(e == n_exp - 1)
    def _():
        drain(0)
        drain(1)

        @pl.when(st_ref[ST_READ] == 1)
        def _():
            wait_read(lax.rem(st_ref[ST_CHUNKS], 2))


def _experts(starts, counts, xs, wg, wu, wd):
    n_exp, d, d_e = wg.shape
    n_slot = xs.shape[0] // PACK_ROWS - EXPERT_CHUNK
    rows = EXPERT_CHUNK * PACK_ROWS
    grid_spec = pltpu.PrefetchScalarGridSpec(
        num_scalar_prefetch=2,
        grid=(n_exp,),
        in_specs=[pl.BlockSpec((1, d, d_e), lambda e, st, ct: (e, 0, 0)),
                  pl.BlockSpec((1, d, d_e), lambda e, st, ct: (e, 0, 0)),
                  pl.BlockSpec((1, d_e, d), lambda e, st, ct: (e, 0, 0)),
                  pl.BlockSpec(memory_space=pl.ANY)],
        out_specs=pl.BlockSpec(memory_space=pl.ANY),
        scratch_shapes=[pltpu.VMEM((2, rows, LANES), jnp.uint32), pltpu.VMEM((2, rows, LANES), jnp.uint32),
                        pltpu.VMEM((d, d_e), BF16), pltpu.VMEM((d, d_e), BF16), pltpu.VMEM((d_e, d), BF16),
                        pltpu.SMEM((ST_WRITE + 4,), jnp.int32),
                        pltpu.SemaphoreType.DMA((2,)), pltpu.SemaphoreType.DMA((2,))],
    )
    return pl.pallas_call(
        _expert_kernel,
        grid_spec=grid_spec,
        out_shape=jax.ShapeDtypeStruct((n_slot * PACK_ROWS, LANES), jnp.uint32),
        compiler_params=_params(("arbitrary",)),
        name="experts",
    )(starts, counts, wg, wu, wd, xs)


def _combine_kernel(dest_ref, wts_ref, h_ref, eo_ref, wgs_ref, wus_ref, wds_ref, g_ref, b_ref,
                    out_ref, gbuf, sem):
    tm = out_ref.shape[0]

    def start(t, c):
        for k in range(TOP_K):
            _tile_copy(eo_ref, dest_ref[t * TOP_K + k], gbuf.at[k], t * PACK_ROWS, sem, 1, PACK_ROWS).start(priority=k % 2)
        return c

    lax.fori_loop(0, tm, start, 0)
    hh = _load_token_tiles(h_ref, tm)
    hb = hh.astype(BF16)
    hid = _silu(_dot(hb, wgs_ref[...])) * _dot(hb, wus_ref[...])
    acc = DN_ALPHA * hh + _dot(hid.astype(BF16), wds_ref[...])
    w_t = jnp.concatenate([wts_ref[...], jnp.zeros((LANES - TOP_K, tm), F32)], axis=0).T

    def wait(t, c):
        for k in range(TOP_K):
            _tile_copy(eo_ref, dest_ref[t * TOP_K + k], gbuf.at[k], t * PACK_ROWS, sem, 1, PACK_ROWS).wait()
        return c

    lax.fori_loop(0, tm, wait, 0)
    for k in range(TOP_K):
        acc = acc + _load_packed(gbuf, tm, lead=(k,)) * w_t[:, k:k + 1]
    out_ref[...] = _layer_norm(acc, g_ref[...], b_ref[...])


def _combine(dest_t, wts_t, h_tt, eo, wgs, wus, wds, g2, b2, tm):
    t = dest_t.shape[0] // TOP_K
    d = wgs.shape[0]
    const = lambda shape: pl.BlockSpec(shape, lambda i: (0, 0))
    return pl.pallas_call(
        _combine_kernel,
        grid=(t // tm,),
        in_specs=[pl.BlockSpec((tm * TOP_K,), lambda i: (i,), memory_space=pltpu.SMEM),
                  pl.BlockSpec((TOP_K, tm), lambda i: (0, i)),
                  pl.BlockSpec((tm * SUBLANES, LANES), lambda i: (i, 0)),
                  pl.BlockSpec(memory_space=pl.ANY),
                  const(wgs.shape), const(wus.shape), const(wds.shape), const((1, d)), const((1, d))],
        out_specs=pl.BlockSpec((tm, d), lambda i: (i, 0)),
        out_shape=jax.ShapeDtypeStruct((t, d), F32),
        scratch_shapes=[pltpu.VMEM((TOP_K, tm * PACK_ROWS, LANES), jnp.uint32), pltpu.SemaphoreType.DMA(())],
        compiler_params=_params(("arbitrary",)),
        name="combine",
    )(dest_t, wts_t, h_tt, eo, wgs, wus, wds, g2, b2)


def _regroup_w_in(w):
    d = w.shape[0]
    splits = np.cumsum([GDN_QK, GDN_QK, GDN_VW, GDN_VW, GDN_HEADS, GDN_HEADS, RET_QK, RET_QK, RET_VW])
    qa, ka, va, za, ba, aa, qb, kb, vb, gb = jnp.split(w, splits.tolist(), axis=1)
    ba_blk = jnp.concatenate([ba, aa, jnp.zeros((d, LANES - 2 * GDN_HEADS), w.dtype)], axis=1)
    qb = qb.reshape(d, RET_HEADS, RET_DK // 2, 2)
    kb = kb.reshape(d, RET_HEADS, RET_DK // 2, 2)
    qkb = jnp.concatenate([qb[..., 0], qb[..., 1], kb[..., 0], kb[..., 1]], axis=-1).reshape(d, RET_HEADS * LANES)
    return jnp.concatenate([qa, ka, va, za, qkb, vb, gb, ba_blk], axis=1).astype(BF16)


def _rotary_tables(seq):
    inv = 1.0 / (ROPE_BASE ** jnp.linspace(0.0, 1.0, RET_DK // 2, dtype=F32))
    ang = jnp.arange(seq, dtype=F32)[:, None] * inv[None, :]
    c, s = jnp.cos(ang), jnp.sin(ang)
    ks = RET_DK ** -0.5
    rot_c = jnp.concatenate([c, c, c * ks, c * ks], axis=1)
    rot_s = jnp.concatenate([-s, s, -s * ks, s * ks], axis=1)
    return rot_c, rot_s


def _tile(n, pref):
    return pref if n % pref == 0 else n


def kernel(x, w_in, gdn_conv_w, gdn_a_log, gdn_dt_bias, gdn_norm_w, w_out, ln1_g, ln1_b, w_router, router_bias,
           w_gate_e, w_up_e, w_down_e, w_gate_s, w_up_s, w_down_s, ln2_g, ln2_b):
    batch, seq, d = x.shape
    t = batch * seq
    n_exp = w_router.shape[-1]
    hcur = x.reshape(t, d)
    for l in range(DEPTH):
        w_bf = _regroup_w_in(w_in[l])
        log_gamma = jnp.log(1.0 - 2.0 ** (-5.0 - jnp.arange(RET_HEADS, dtype=F32)))
        hp = jnp.stack([-jnp.exp(gdn_a_log[l].astype(F32)), gdn_dt_bias[l].astype(F32), log_gamma])
        rot_c, rot_s = _rotary_tables(seq)
        proj, bat = _proj(hcur, w_bf, _tile(t, 512))
        oa, ob = _mixer(proj, bat, hp, gdn_conv_w[l].astype(F32), gdn_norm_w[l].reshape(1, -1).astype(F32),
                        rot_c, rot_s, batch, seq)
        hh, h_pk, idx_t, wts_t, counts = _post_mix(
            hcur, oa, ob, w_out[l].astype(BF16), ln1_g[l].reshape(1, d), ln1_b[l].reshape(1, d),
            w_router[l].T.astype(BF16), router_bias[l].reshape(n_exp, 1).astype(F32), _tile(t, 256))
        base = jnp.cumsum(counts, axis=0) - counts
        dest_t = _dest(idx_t, base, _tile(t, 512)).T.reshape(t * TOP_K)
        xs = _dispatch(dest_t, h_pk, _tile(t, 512))
        eo = _experts(base.reshape(n_exp).astype(jnp.int32), counts.reshape(n_exp).astype(jnp.int32),
                      xs, w_gate_e[l], w_up_e[l], w_down_e[l])
        hcur = _combine(dest_t, wts_t, hh, eo, w_gate_s[l].astype(BF16), w_up_s[l].astype(BF16),
                        w_down_s[l].astype(BF16), ln2_g[l].reshape(1, d), ln2_b[l].reshape(1, d), _tile(t, 256))
    return hcur.reshape(batch, seq, d)
```

```python
import math

import numpy as np
import jax
import jax.numpy as jnp
from jax import lax
from jax.experimental import pallas as pl
from jax.experimental.pallas import tpu as pltpu

GDN_HEADS = 4
GDN_DK = 128
GDN_DV = 128
GDN_CONV = 4
GDN_CHUNK = 64
RET_HEADS = 4
RET_DK = 64
RET_DV = 128
ROPE_BASE = 10000.0
N_GROUPS = 8
TOPK_GROUPS = 4
TOP_K = 8
ROUTED_SCALE = 2.5
DEPTH = 1
DN_ALPHA = (2.0 * DEPTH) ** 0.25
LN_EPS = 1e-5
NORM_EPS = 1e-6

GDN_QK = GDN_HEADS * GDN_DK
GDN_VW = GDN_HEADS * GDN_DV
RET_QK = RET_HEADS * RET_DK
RET_VW = RET_HEADS * RET_DV

LANES = 128
SUBLANES = 8
VMEM_LIMIT_BYTES = 56 * 1024 * 1024

SUPER = 256
NEG_INF = float("-inf")

COL_QA = 0
COL_KA = COL_QA + GDN_HEADS
COL_VA = COL_KA + GDN_HEADS
COL_ZA = COL_VA + GDN_HEADS
COL_QKB = COL_ZA + GDN_HEADS
COL_VB = COL_QKB + RET_HEADS
COL_GB = COL_VB + RET_HEADS
COL_BA = COL_GB + RET_HEADS
N_COLBLK = COL_BA + 1
PROJ_W = N_COLBLK * LANES
HEAD_W = GDN_HEADS * LANES
MIX_TS = 512
EXPERT_CHUNK = 256
EXPERT_PIECE = 64
PACK_ROWS = 4

BF16 = jnp.bfloat16
F32 = jnp.float32


def _dot(a, b):
    return jnp.dot(a, b, preferred_element_type=F32)


def _dot_nt(a, b):
    return lax.dot_general(a, b, (((1,), (1,)), ((), ())), preferred_element_type=F32)


def _silu(x):
    return x * jax.nn.sigmoid(x)


def _softplus(x):
    return jnp.maximum(x, 0.0) + jnp.log1p(jnp.exp(-jnp.abs(x)))


def _iota(shape, dim, dtype=jnp.int32):
    return lax.broadcasted_iota(dtype, shape, dim)


def _params(sem):
    return pltpu.CompilerParams(dimension_semantics=sem, vmem_limit_bytes=VMEM_LIMIT_BYTES)


def _load_token_tiles(ref, n, lead=()):
    parts = [ref[lead + (pl.ds(s, n, stride=SUBLANES), slice(None))] for s in range(SUBLANES)]
    return jnp.concatenate(parts, axis=1)


def _store_token_tiles(ref, val, lead=()):
    n = val.shape[0]
    for s in range(SUBLANES):
        ref[lead + (pl.ds(s, n, stride=SUBLANES), slice(None))] = val[:, s * LANES:(s + 1) * LANES]


def _tile_copy(src_ref, src_row, dst_ref, dst_row, sem, n_slots=1, slot_rows=SUBLANES):
    rows = n_slots * slot_rows
    aligned = lambda r: r if isinstance(r, int) else pl.multiple_of(r, slot_rows)
    return pltpu.make_async_copy(src_ref.at[pl.ds(aligned(src_row), rows)],
                                 dst_ref.at[pl.ds(aligned(dst_row), rows)], sem)


def _store_packed(ref, val, lead=()):
    n, width = val.shape
    bits = pltpu.bitcast(val.astype(BF16).astype(F32), jnp.uint32)
    words = (bits[:, width // 2:] & jnp.uint32(0xFFFF0000)) | (bits[:, :width // 2] >> 16)
    for s in range(PACK_ROWS):
        ref[lead + (pl.ds(s, n, stride=PACK_ROWS), slice(None))] = words[:, s * LANES:(s + 1) * LANES]


def _load_packed(ref, n, lead=(), dtype=F32):
    words = [ref[lead + (pl.ds(s, n, stride=PACK_ROWS), slice(None))] for s in range(PACK_ROWS)]
    lo = [pltpu.bitcast(w << 16, F32) for w in words]
    hi = [pltpu.bitcast(w & jnp.uint32(0xFFFF0000), F32) for w in words]
    return jnp.concatenate(lo + hi, axis=1).astype(dtype)


def _proj_kernel(x_ref, w_ref, proj_ref, bat_ref):
    xb = x_ref[...].astype(BF16)
    step = 4 * LANES
    for j in range(0, PROJ_W, step):
        n = min(step, PROJ_W - j)
        proj_ref[:, j:j + n] = _dot(xb, w_ref[:, j:j + n])
    ba = proj_ref[:, COL_BA * LANES:(COL_BA + 1) * LANES]
    bat_ref[...] = ba.T[0:SUBLANES, :]


def _proj(x2, w_bf, tm):
    t, d = x2.shape
    return pl.pallas_call(
        _proj_kernel,
        grid=(t // tm,),
        in_specs=[pl.BlockSpec((tm, d), lambda i: (i, 0)),
                  pl.BlockSpec((d, PROJ_W), lambda i: (0, 0))],
        out_specs=[pl.BlockSpec((tm, PROJ_W), lambda i: (i, 0)),
                   pl.BlockSpec((SUBLANES, tm), lambda i: (0, i))],
        out_shape=[jax.ShapeDtypeStruct((t, PROJ_W), F32),
                   jax.ShapeDtypeStruct((SUBLANES, t), F32)],
        compiler_params=_params(("parallel",)),
        name="proj",
    )(x2, w_bf)


def _mixer_kernel(hp_ref,
                  q_ref, k_ref, v_ref, z_ref, qkb_ref, vb_ref, gb_ref, ba_ref, bat_ref,
                  qt_ref, kt_ref, vt_ref,
                  cw_ref, nw_ref, rc_ref, rs_ref,
                  oa_ref, ob_ref, st_ref, st2_ref):
    j = pl.program_id(1)
    ts = q_ref.shape[0]
    n_chunk = SUPER // GDN_CHUNK

    @pl.when(j == 0)
    def _():
        st_ref[...] = jnp.zeros_like(st_ref)
        st2_ref[...] = jnp.zeros_like(st2_ref)

    ri = _iota((SUPER, SUPER), 0)
    ci = _iota((SUPER, SUPER), 1)
    shift = int(math.log2(GDN_CHUNK))
    same = (ri >> shift) == (ci >> shift)
    causal = same & (ci <= ri)
    strict = same & (ci < ri)
    causal_t = same & (ri <= ci)
    eye = jnp.where(ri == ci, 1.0, 0.0)
    lane = _iota((SUPER, LANES), 1)
    dpos = jnp.maximum((ri - ci).astype(F32), 0.0)
    lower = ri >= ci
    pos = _iota((SUPER, 1), 0).astype(F32)
    half = lane < RET_DK
    pair_lo = (lane % RET_DK) < (RET_DK // 2)

    def conv_silu(ref, tail_ref, c0, h, r0, s):
        cols = slice(h * LANES, (h + 1) * LANES)
        cur = ref[pl.ds(r0, SUPER), cols]
        inside = ref[pl.ds(pl.multiple_of(jnp.maximum(r0 - SUBLANES, 0), SUBLANES), SUBLANES), cols]
        prev = jnp.where(s > 0, inside, tail_ref[:, cols])
        prev = jnp.where((s == 0) & (j == 0), 0.0, prev)
        ext = jnp.concatenate([prev, cur], axis=0)
        acc = jnp.zeros((SUPER, LANES), F32)
        for jj in range(GDN_CONV):
            off = SUBLANES - (GDN_CONV - 1) + jj
            acc = acc + cw_ref[jj:jj + 1, c0 + h * LANES:c0 + (h + 1) * LANES] * ext[off:off + SUPER, :]
        return _silu(acc)

    def l2n(u):
        return u * lax.rsqrt(jnp.sum(u * u, axis=-1, keepdims=True) + NORM_EPS)

    heads = range(GDN_HEADS)
    hcols = [slice(h * LANES, (h + 1) * LANES) for h in heads]

    def gdn_heads(r0, s):
        q = [l2n(conv_silu(q_ref, qt_ref, 0, h, r0, s)) * (GDN_DK ** -0.5) for h in heads]
        k = [l2n(conv_silu(k_ref, kt_ref, GDN_QK, h, r0, s)) for h in heads]
        v = [conv_silu(v_ref, vt_ref, 2 * GDN_QK, h, r0, s) for h in heads]
        ba = ba_ref[pl.ds(r0, SUPER), :]
        beta, g_col, gl_col, decay_c, decay_s = [], [], [], [], []
        for h in heads:
            neg_a = hp_ref[0, h]
            dt_b = hp_ref[1, h]
            b_col = jnp.sum(jnp.where(lane == h, ba, 0.0), axis=1, keepdims=True)
            a_col = jnp.sum(jnp.where(lane == h + GDN_HEADS, ba, 0.0), axis=1, keepdims=True)
            a_row = bat_ref[h + GDN_HEADS:h + GDN_HEADS + 1, pl.ds(r0, SUPER)]
            beta.append(jax.nn.sigmoid(b_col))
            la_col = neg_a * _softplus(a_col + dt_b)
            la_row = neg_a * _softplus(a_row + dt_b)
            gc = jnp.sum(jnp.where(causal, la_row, 0.0), axis=1, keepdims=True)
            gr = jnp.sum(jnp.where(causal_t, la_col, 0.0), axis=0, keepdims=True)
            g_col.append(gc)
            gl_col.append(jnp.sum(jnp.where(same, la_row, 0.0), axis=1, keepdims=True))
            dc = jnp.exp(jnp.where(causal, gc - gr, NEG_INF))
            decay_c.append(dc)
            decay_s.append(jnp.where(strict, dc, 0.0))
        kb = [k[h] * beta[h] for h in heads]
        k_bf = [k[h].astype(BF16) for h in heads]
        low = [_dot_nt(kb[h].astype(BF16), k_bf[h]) * decay_s[h] for h in heads]
        p = [eye - low[h] for h in heads]
        sq = low
        for _ in range(shift - 1):
            sq_bf = [sq[h].astype(BF16) for h in heads]
            sq = [_dot(sq_bf[h], sq_bf[h]) for h in heads]
            p = [p[h] + _dot(p[h].astype(BF16), sq[h].astype(BF16)) for h in heads]
        eg = [jnp.exp(g_col[h]) for h in heads]
        rhs = [jnp.concatenate([v[h] * beta[h], kb[h] * eg[h]], axis=1).astype(BF16) for h in heads]
        wk = [_dot(p[h].astype(BF16), rhs[h]) for h in heads]
        w_val = [wk[h][:, :GDN_DV] for h in heads]
        k_cum = [wk[h][:, GDN_DV:].astype(BF16) for h in heads]
        attn = [(_dot_nt(q[h].astype(BF16), k_bf[h]) * decay_c[h]).astype(BF16) for h in heads]
        q_dec = [(q[h] * eg[h]).astype(BF16) for h in heads]
        kd_t = [(k[h] * jnp.exp(gl_col[h] - g_col[h])).T.astype(BF16) for h in heads]
        gl = [jnp.exp(gl_col[h]) for h in heads]
        st = [st_ref[h] for h in heads]
        outs = [[] for _ in heads]
        for c in range(n_chunk):
            lo = c * GDN_CHUNK
            hi = lo + GDN_CHUNK
            st_bf = [st[h].astype(BF16) for h in heads]
            v_new = [w_val[h][lo:hi] - _dot(k_cum[h][lo:hi], st_bf[h]) for h in heads]
            v_pad = []
            for h in heads:
                pieces = [jnp.zeros((GDN_CHUNK, GDN_DV), F32)] * n_chunk
                pieces[c] = v_new[h]
                v_pad.append(jnp.concatenate(pieces, axis=0).astype(BF16))
            for h in heads:
                outs[h].append(_dot(q_dec[h][lo:hi], st_bf[h]) + _dot(attn[h][lo:hi], v_pad[h]))
            st = [st[h] * gl[h][lo:lo + 1, :] + _dot(kd_t[h], v_pad[h]) for h in heads]
        for h in heads:
            st_ref[h] = st[h]
            o = jnp.concatenate(outs[h], axis=0)
            o = o * lax.rsqrt(jnp.mean(o * o, axis=-1, keepdims=True) + NORM_EPS) * nw_ref[...]
            oa_ref[pl.ds(r0, SUPER), hcols[h]] = (o * _silu(z_ref[pl.ds(r0, SUPER), hcols[h]])).astype(oa_ref.dtype)

    def ret_heads(r0):
        rc = rc_ref[pl.ds(r0, SUPER), :]
        rs = rs_ref[pl.ds(r0, SUPER), :]
        lg = [hp_ref[2, h] for h in heads]
        qm, km, vb = [], [], []
        for h in heads:
            x = qkb_ref[pl.ds(r0, SUPER), hcols[h]]
            swapped = jnp.where(pair_lo, pltpu.roll(x, LANES - RET_DK // 2, 1), pltpu.roll(x, RET_DK // 2, 1))
            xr = x * rc + swapped * rs
            qm.append(jnp.where(half, xr, 0.0))
            km.append(jnp.where(half, pltpu.roll(xr, LANES - RET_DK, 1), 0.0))
            vb.append(vb_ref[pl.ds(r0, SUPER), hcols[h]].astype(BF16))
        inner = [(_dot_nt(qm[h].astype(BF16), km[h].astype(BF16))
                  * jnp.where(lower, jnp.exp(lg[h] * dpos), 0.0)).astype(BF16) for h in heads]
        st2 = [st2_ref[h] for h in heads]
        ob = [_dot(inner[h], vb[h])
              + _dot((qm[h] * jnp.exp(lg[h] * (pos + 1.0))).astype(BF16), st2[h].astype(BF16)) for h in heads]
        for h in heads:
            g_chunk = jnp.exp(jnp.full((1, 1), SUPER, F32) * lg[h])
            st2_ref[h] = st2[h] * g_chunk + _dot((km[h] * jnp.exp(lg[h] * (SUPER - 1.0 - pos))).T.astype(BF16), vb[h])
        for h in heads:
            mu = jnp.mean(ob[h], axis=-1, keepdims=True)
            oc = ob[h] - mu
            oc = oc * lax.rsqrt(jnp.mean(oc * oc, axis=-1, keepdims=True) + NORM_EPS)
            ob_ref[pl.ds(r0, SUPER), hcols[h]] = (oc * _silu(gb_ref[pl.ds(r0, SUPER), hcols[h]])).astype(ob_ref.dtype)

    def body(s, carry):
        r0 = pl.multiple_of(s * SUPER, SUPER)
        gdn_heads(r0, s)
        ret_heads(r0)
        return carry

    lax.fori_loop(0, ts // SUPER, body, 0)


def _mixer(proj, bat, hp, conv_w, norm_w, rot_c, rot_s, batch, seq):
    t = proj.shape[0]
    ts = MIX_TS if seq % MIX_TS == 0 else seq
    nj = seq // ts
    grp = lambda c0: pl.BlockSpec((ts, HEAD_W), lambda b, j, c=c0 // GDN_HEADS: (b * nj + j, c))
    tail = lambda c0: pl.BlockSpec(
        (SUBLANES, HEAD_W),
        lambda b, j, c=c0 // GDN_HEADS: (jnp.maximum((b * seq + j * ts) // SUBLANES - 1, 0), c))
    const2 = lambda shape: pl.BlockSpec(shape, lambda b, j: (0, 0))
    tab = pl.BlockSpec((ts, LANES), lambda b, j: (j, 0))
    in_specs = [
        pl.BlockSpec(memory_space=pltpu.SMEM),
        grp(COL_QA), grp(COL_KA), grp(COL_VA), grp(COL_ZA), grp(COL_QKB), grp(COL_VB), grp(COL_GB),
        pl.BlockSpec((ts, LANES), lambda b, j: (b * nj + j, COL_BA)),
        pl.BlockSpec((SUBLANES, ts), lambda b, j: (0, b * nj + j)),
        tail(COL_QA), tail(COL_KA), tail(COL_VA),
        const2(conv_w.shape), const2((1, GDN_DV)), tab, tab,
    ]
    out_spec = pl.BlockSpec((ts, HEAD_W), lambda b, j: (b * nj + j, 0))
    return pl.pallas_call(
        _mixer_kernel,
        grid=(batch, nj),
        in_specs=in_specs,
        out_specs=[out_spec, out_spec],
        out_shape=[jax.ShapeDtypeStruct((t, GDN_VW), BF16), jax.ShapeDtypeStruct((t, RET_VW), BF16)],
        scratch_shapes=[pltpu.VMEM((GDN_HEADS, GDN_DK, GDN_DV), F32),
                        pltpu.VMEM((RET_HEADS, LANES, RET_DV), F32)],
        compiler_params=_params(("parallel", "arbitrary")),
        name="mixer",
    )(hp, proj, proj, proj, proj, proj, proj, proj, proj, bat, proj, proj, proj, conv_w, norm_w, rot_c, rot_s)


def _layer_norm(u, g, b):
    mu = jnp.mean(u, axis=-1, keepdims=True)
    uc = u - mu
    var = jnp.mean(uc * uc, axis=-1, keepdims=True)
    return uc * lax.rsqrt(var + LN_EPS) * g + b


def _post_mix_kernel(x_ref, oa_ref, ob_ref, wo_ref, g_ref, b_ref, wr_ref, rb_ref,
                     h_ref, hp_ref, idx_ref, wts_ref, cnt_ref):
    n_exp = wr_ref.shape[0]
    tm = x_ref.shape[0]
    per_grp = n_exp // N_GROUPS
    mix = _dot(oa_ref[...], wo_ref[:GDN_VW, :]) + _dot(ob_ref[...], wo_ref[GDN_VW:, :])
    hh = _layer_norm(DN_ALPHA * x_ref[...] + mix, g_ref[...], b_ref[...])
    _store_token_tiles(h_ref, hh)
    _store_packed(hp_ref, hh)
    scores = jax.nn.sigmoid(_dot_nt(wr_ref[...], hh.astype(BF16)))
    choice = scores + rb_ref[...]
    big = float(n_exp)
    io_g = _iota((per_grp, tm), 0).astype(F32)
    grp = []
    for g in range(N_GROUPS):
        blk = choice[g * per_grp:(g + 1) * per_grp, :]
        m1 = jnp.max(blk, axis=0, keepdims=True)
        i1 = jnp.min(jnp.where(blk == m1, io_g, big), axis=0, keepdims=True)
        m2 = jnp.max(jnp.where(io_g == i1, NEG_INF, blk), axis=0, keepdims=True)
        grp.append(m1 + m2)
    gsc = jnp.concatenate(grp, axis=0)
    io8 = _iota((N_GROUPS, tm), 0).astype(F32)
    gsel = jnp.zeros((N_GROUPS, tm), F32)
    for _ in range(TOPK_GROUPS):
        m = jnp.max(gsc, axis=0, keepdims=True)
        i = jnp.min(jnp.where(gsc == m, io8, big), axis=0, keepdims=True)
        hit = io8 == i
        gsel = jnp.where(hit, 1.0, gsel)
        gsc = jnp.where(hit, NEG_INF, gsc)
    masked = jnp.concatenate(
        [jnp.where(gsel[g:g + 1, :] > 0.0, choice[g * per_grp:(g + 1) * per_grp, :], NEG_INF)
         for g in range(N_GROUPS)], axis=0)
    io_e = _iota((n_exp, tm), 0).astype(F32)
    sel = jnp.zeros((n_exp, tm), F32)
    ids, ws = [], []
    for _ in range(TOP_K):
        m = jnp.max(masked, axis=0, keepdims=True)
        i = jnp.min(jnp.where(masked == m, io_e, big), axis=0, keepdims=True)
        hit = io_e == i
        ws.append(jnp.sum(jnp.where(hit, scores, 0.0), axis=0, keepdims=True))
        ids.append(i)
        masked = jnp.where(hit, NEG_INF, masked)
        sel = jnp.where(hit, 1.0, sel)
    w = jnp.concatenate(ws, axis=0)
    wts_ref[...] = w / jnp.sum(w, axis=0, keepdims=True) * ROUTED_SCALE
    idx_ref[...] = jnp.concatenate(ids, axis=0).astype(jnp.int32)

    @pl.when(pl.program_id(0) == 0)
    def _():
        cnt_ref[...] = jnp.zeros_like(cnt_ref)

    cnt_ref[...] += jnp.sum(sel, axis=1, keepdims=True)


def _post_mix(x2, oa, ob, wo_bf, g1, b1, wr_t, rbias, tm):
    t, d = x2.shape
    n_exp = wr_t.shape[0]
    row = lambda w: pl.BlockSpec((tm, w), lambda i: (i, 0))
    const = lambda shape: pl.BlockSpec(shape, lambda i: (0, 0))
    tok = pl.BlockSpec((TOP_K, tm), lambda i: (0, i))
    return pl.pallas_call(
        _post_mix_kernel,
        grid=(t // tm,),
        in_specs=[row(d), row(GDN_VW), row(RET_VW), const(wo_bf.shape), const((1, d)), const((1, d)),
                  const(wr_t.shape), const((n_exp, 1))],
        out_specs=[pl.BlockSpec((tm * SUBLANES, LANES), lambda i: (i, 0)),
                   pl.BlockSpec((tm * PACK_ROWS, LANES), lambda i: (i, 0)), tok, tok, const((n_exp, 1))],
        out_shape=[jax.ShapeDtypeStruct((t * SUBLANES, LANES), F32),
                   jax.ShapeDtypeStruct((t * PACK_ROWS, LANES), jnp.uint32),
                   jax.ShapeDtypeStruct((TOP_K, t), jnp.int32),
                   jax.ShapeDtypeStruct((TOP_K, t), F32),
                   jax.ShapeDtypeStruct((n_exp, 1), F32)],
        compiler_params=_params(("arbitrary",)),
        name="post_mix",
    )(x2, oa, ob, wo_bf, g1, b1, wr_t, rbias)


def _dest_kernel(idx_ref, base_ref, dest_ref, carry_ref):
    n_exp = base_ref.shape[0]
    tm = idx_ref.shape[1]

    @pl.when(pl.program_id(0) == 0)
    def _():
        carry_ref[...] = jnp.zeros_like(carry_ref)

    io_e = _iota((n_exp, tm), 0)
    idx = idx_ref[...]
    hits = [io_e == idx[k:k + 1, :] for k in range(TOP_K)]
    sel = jnp.zeros((n_exp, tm), F32)
    for hit in hits:
        sel = jnp.where(hit, 1.0, sel)
    before = jnp.where(_iota((tm, tm), 0) < _iota((tm, tm), 1), 1.0, 0.0).astype(BF16)
    rank = _dot(sel.astype(BF16), before)
    tot = base_ref[...] + carry_ref[...] + rank
    dest = [jnp.sum(jnp.where(hit, tot, 0.0), axis=0, keepdims=True) for hit in hits]
    dest_ref[...] = jnp.concatenate(dest, axis=0).astype(jnp.int32) * PACK_ROWS
    carry_ref[...] += jnp.sum(sel, axis=1, keepdims=True)


def _dest(idx_t, base, tm):
    t = idx_t.shape[1]
    n_exp = base.shape[0]
    tok = pl.BlockSpec((TOP_K, tm), lambda i: (0, i))
    return pl.pallas_call(
        _dest_kernel,
        grid=(t // tm,),
        in_specs=[tok, pl.BlockSpec((n_exp, 1), lambda i: (0, 0))],
        out_specs=tok,
        out_shape=jax.ShapeDtypeStruct((TOP_K, t), jnp.int32),
        scratch_shapes=[pltpu.VMEM((n_exp, 1), F32)],
        compiler_params=_params(("arbitrary",)),
        name="dest",
    )(idx_t, base)


def _dispatch_kernel(dest_ref, h_ref, xs_ref, zero_ref, sem, zsem):
    tm = h_ref.shape[0] // PACK_ROWS
    last = pl.num_programs(0) - 1
    pad_row = xs_ref.shape[0] - zero_ref.shape[0]

    @pl.when(pl.program_id(0) == last)
    def _():
        zero_ref[...] = jnp.zeros_like(zero_ref)
        _tile_copy(zero_ref, 0, xs_ref, pad_row, zsem, EXPERT_CHUNK, PACK_ROWS).start()

    def start(t, c):
        for k in range(TOP_K):
            _tile_copy(h_ref, t * PACK_ROWS, xs_ref, dest_ref[t * TOP_K + k], sem, 1, PACK_ROWS).start(priority=k % 2)
        return c

    lax.fori_loop(0, tm, start, 0)

    for k in range(TOP_K):
        _tile_copy(h_ref, 0, xs_ref, 0, sem, tm, PACK_ROWS).wait()

    @pl.when(pl.program_id(0) == last)
    def _():
        _tile_copy(zero_ref, 0, xs_ref, pad_row, zsem, EXPERT_CHUNK, PACK_ROWS).wait()


def _dispatch(dest_t, h_pk, tm):
    t = dest_t.shape[0] // TOP_K
    n_rows = (t * TOP_K + EXPERT_CHUNK) * PACK_ROWS
    return pl.pallas_call(
        _dispatch_kernel,
        grid=(t // tm,),
        in_specs=[pl.BlockSpec((tm * TOP_K,), lambda i: (i,), memory_space=pltpu.SMEM),
                  pl.BlockSpec((tm * PACK_ROWS, LANES), lambda i: (i, 0))],
        out_specs=pl.BlockSpec(memory_space=pl.ANY),
        out_shape=jax.ShapeDtypeStruct((n_rows, LANES), jnp.uint32),
        scratch_shapes=[pltpu.VMEM((EXPERT_CHUNK * PACK_ROWS, LANES), jnp.uint32),
                        pltpu.SemaphoreType.DMA(()), pltpu.SemaphoreType.DMA(())],
        compiler_params=_params(("arbitrary",)),
        name="dispatch",
    )(dest_t, h_pk)


def _chunk_pieces(n):
    pieces = [(j < n // EXPERT_PIECE, j * EXPERT_PIECE, EXPERT_PIECE) for j in range(EXPERT_CHUNK // EXPERT_PIECE)]
    size = EXPERT_PIECE // 2
    while size >= 1:
        shift = int(math.log2(size)) + 1
        pieces.append(((n & size) != 0, (n >> shift) << shift, size))
        size //= 2
    return pieces


ST_CHUNKS = 0
ST_ISSUED = 1
ST_CUR_E = 2
ST_CUR_I = 3
ST_WRITE = 4
N_XBUF = 4
N_OBUF = 3


def _expert_kernel(start_ref, count_ref, wg_ref, wu_ref, wd_ref, xs_ref, eo_ref,
                   xbuf, obuf, wgb, wub, wdb, st_ref, rsem, wsem):
    e = pl.program_id(0)
    n_exp = pl.num_programs(0)
    ch = EXPERT_CHUNK
    s0 = start_ref[e]
    cnt = count_ref[e]
    n_ch = (cnt + ch - 1) // ch

    def read_parts(first_slot, buf):
        return [_tile_copy(xs_ref, (first_slot + j * EXPERT_PIECE) * PACK_ROWS, xbuf.at[buf],
                           j * EXPERT_PIECE * PACK_ROWS, rsem.at[buf], EXPERT_PIECE, PACK_ROWS)
                for j in range(ch // EXPERT_PIECE)]

    def read_next():
        def exhausted(c):
            cnt_c = count_ref[jnp.minimum(c[0], n_exp - 1)]
            return jnp.logical_and(c[0] < n_exp, c[1] * ch >= cnt_c)

        ce, ci = lax.while_loop(exhausted, lambda c: (c[0] + 1, jnp.int32(0)),
                                (st_ref[ST_CUR_E], st_ref[ST_CUR_I]))

        @pl.when(ce < n_exp)
        def _():
            first_slot = start_ref[jnp.minimum(ce, n_exp - 1)] + ci * ch
            buf = lax.rem(st_ref[ST_ISSUED], N_XBUF)
            for j, cp in enumerate(read_parts(first_slot, buf)):
                cp.start(priority=j % 2)
            st_ref[ST_ISSUED] = st_ref[ST_ISSUED] + 1

        st_ref[ST_CUR_E] = ce
        st_ref[ST_CUR_I] = ci + 1

    def wait_read(buf):
        for cp in read_parts(0, buf):
            cp.wait()

    def out_pieces(buf, first_slot, n, wait):
        for j, (pred, off, size) in enumerate(_chunk_pieces(n)):
            @pl.when(pred)
            def _(j=j, off=off, size=size):
                cp = _tile_copy(obuf.at[buf], off * PACK_ROWS, eo_ref, (first_slot + off) * PACK_ROWS,
                                wsem.at[buf], size, PACK_ROWS)
                cp.wait() if wait else cp.start(priority=j % 2)

    def drain(buf):
        out_pieces(buf, st_ref[ST_WRITE + 2 * buf], st_ref[ST_WRITE + 2 * buf + 1], True)
        st_ref[ST_WRITE + 2 * buf + 1] = 0

    @pl.when(e == 0)
    def _():
        for i in range(st_ref.shape[0]):
            st_ref[i] = 0
        for _ in range(N_XBUF - 1):
            read_next()

    @pl.when(n_ch > 0)
    def _():
        wgb[...] = wg_ref[0].astype(BF16)
        wub[...] = wu_ref[0].astype(BF16)
        wdb[...] = wd_ref[0].astype(BF16)

    def body(i, c):
        g = st_ref[ST_CHUNKS]
        buf = lax.rem(g, N_XBUF)
        wait_read(buf)
        read_next()
        xb = _load_packed(xbuf, ch, lead=(buf,), dtype=BF16)
        hid = _silu(_dot(xb, wgb[...])) * _dot(xb, wub[...])
        res = _dot(hid.astype(BF16), wdb[...])
        obf = lax.rem(g, N_OBUF)
        drain(obf)
        _store_packed(obuf, res, lead=(obf,))
        first_slot = s0 + i * ch
        n = jnp.minimum(ch, cnt - i * ch)
        out_pieces(obf, first_slot, n, False)
        st_ref[ST_WRITE + 2 * obf] = first_slot
        st_ref[ST_WRITE + 2 * obf + 1] = n
        st_ref[ST_CHUNKS] = g + 1
        return c

    lax.fori_loop(0, n_ch, body, 0)

    @pl.when(e == n_exp - 1)
    def _():
        for b in range(N_OBUF):
            drain(b)


def _experts(starts, counts, xs, wg, wu, wd):
    n_exp, d, d_e = wg.shape
    n_slot = xs.shape[0] // PACK_ROWS - EXPERT_CHUNK
    rows = EXPERT_CHUNK * PACK_ROWS
    grid_spec = pltpu.PrefetchScalarGridSpec(
        num_scalar_prefetch=2,
        grid=(n_exp,),
        in_specs=[pl.BlockSpec((1, d, d_e), lambda e, st, ct: (e, 0, 0)),
                  pl.BlockSpec((1, d, d_e), lambda e, st, ct: (e, 0, 0)),
                  pl.BlockSpec((1, d_e, d), lambda e, st, ct: (e, 0, 0)),
                  pl.BlockSpec(memory_space=pl.ANY)],
        out_specs=pl.BlockSpec(memory_space=pl.ANY),
        scratch_shapes=[pltpu.VMEM((N_XBUF, rows, LANES), jnp.uint32), pltpu.VMEM((N_OBUF, rows, LANES), jnp.uint32),
                        pltpu.VMEM((d, d_e), BF16), pltpu.VMEM((d, d_e), BF16), pltpu.VMEM((d_e, d), BF16),
                        pltpu.SMEM((ST_WRITE + 2 * N_OBUF,), jnp.int32),
                        pltpu.SemaphoreType.DMA((N_XBUF,)), pltpu.SemaphoreType.DMA((N_OBUF,))],
    )
    return pl.pallas_call(
        _expert_kernel,
        grid_spec=grid_spec,
        out_shape=jax.ShapeDtypeStruct((n_slot * PACK_ROWS, LANES), jnp.uint32),
        compiler_params=_params(("arbitrary",)),
        name="experts",
    )(starts, counts, wg, wu, wd, xs)


def _combine_kernel(dest_ref, dnext_ref, wts_ref, h_ref, eo_ref, wgs_ref, wus_ref, wds_ref, g_ref, b_ref,
                    out_ref, gbuf, sem):
    i = pl.program_id(0)
    tm = out_ref.shape[0]
    buf = lax.rem(i, 2)

    def gather(d_ref, bf):
        def start(t, c):
            for k in range(TOP_K):
                _tile_copy(eo_ref, d_ref[t * TOP_K + k], gbuf.at[bf, k], t * PACK_ROWS, sem.at[bf],
                           1, PACK_ROWS).start(priority=k % 2)
            return c

        lax.fori_loop(0, tm, start, 0)

    @pl.when(i == 0)
    def _():
        gather(dest_ref, 0)

    @pl.when(i + 1 < pl.num_programs(0))
    def _():
        gather(dnext_ref, 1 - buf)

    hh = _load_token_tiles(h_ref, tm)
    hb = hh.astype(BF16)
    hid = _silu(_dot(hb, wgs_ref[...])) * _dot(hb, wus_ref[...])
    acc = DN_ALPHA * hh + _dot(hid.astype(BF16), wds_ref[...])
    w_t = jnp.concatenate([wts_ref[...], jnp.zeros((LANES - TOP_K, tm), F32)], axis=0).T

    for k in range(TOP_K):
        _tile_copy(eo_ref, 0, gbuf.at[buf, k], 0, sem.at[buf], tm, PACK_ROWS).wait()
    for k in range(TOP_K):
        acc = acc + _load_packed(gbuf, tm, lead=(buf, k)) * w_t[:, k:k + 1]
    out_ref[...] = _layer_norm(acc, g_ref[...], b_ref[...])


def _combine(dest_t, wts_t, h_tt, eo, wgs, wus, wds, g2, b2, tm):
    t = dest_t.shape[0] // TOP_K
    d = wgs.shape[0]
    const = lambda shape: pl.BlockSpec(shape, lambda i: (0, 0))
    n_tiles = t // tm
    return pl.pallas_call(
        _combine_kernel,
        grid=(n_tiles,),
        in_specs=[pl.BlockSpec((tm * TOP_K,), lambda i: (i,), memory_space=pltpu.SMEM),
                  pl.BlockSpec((tm * TOP_K,), lambda i: (jnp.minimum(i + 1, n_tiles - 1),), memory_space=pltpu.SMEM),
                  pl.BlockSpec((TOP_K, tm), lambda i: (0, i)),
                  pl.BlockSpec((tm * SUBLANES, LANES), lambda i: (i, 0)),
                  pl.BlockSpec(memory_space=pl.ANY),
                  const(wgs.shape), const(wus.shape), const(wds.shape), const((1, d)), const((1, d))],
        out_specs=pl.BlockSpec((tm, d), lambda i: (i, 0)),
        out_shape=jax.ShapeDtypeStruct((t, d), F32),
        scratch_shapes=[pltpu.VMEM((2, TOP_K, tm * PACK_ROWS, LANES), jnp.uint32), pltpu.SemaphoreType.DMA((2,))],
        compiler_params=_params(("arbitrary",)),
        name="combine",
    )(dest_t, dest_t, wts_t, h_tt, eo, wgs, wus, wds, g2, b2)


def _regroup_w_in(w):
    d = w.shape[0]
    splits = np.cumsum([GDN_QK, GDN_QK, GDN_VW, GDN_VW, GDN_HEADS, GDN_HEADS, RET_QK, RET_QK, RET_VW])
    qa, ka, va, za, ba, aa, qb, kb, vb, gb = jnp.split(w, splits.tolist(), axis=1)
    ba_blk = jnp.concatenate([ba, aa, jnp.zeros((d, LANES - 2 * GDN_HEADS), w.dtype)], axis=1)
    qb = qb.reshape(d, RET_HEADS, RET_DK // 2, 2)
    kb = kb.reshape(d, RET_HEADS, RET_DK // 2, 2)
    qkb = jnp.concatenate([qb[..., 0], qb[..., 1], kb[..., 0], kb[..., 1]], axis=-1).reshape(d, RET_HEADS * LANES)
    return jnp.concatenate([qa, ka, va, za, qkb, vb, gb, ba_blk], axis=1).astype(BF16)


def _rotary_tables(seq):
    inv = 1.0 / (ROPE_BASE ** jnp.linspace(0.0, 1.0, RET_DK // 2, dtype=F32))
    ang = jnp.arange(seq, dtype=F32)[:, None] * inv[None, :]
    c, s = jnp.cos(ang), jnp.sin(ang)
    ks = RET_DK ** -0.5
    rot_c = jnp.concatenate([c, c, c * ks, c * ks], axis=1)
    rot_s = jnp.concatenate([-s, s, -s * ks, s * ks], axis=1)
    return rot_c, rot_s


def _tile(n, pref):
    return pref if n % pref == 0 else n


def kernel(x, w_in, gdn_conv_w, gdn_a_log, gdn_dt_bias, gdn_norm_w, w_out, ln1_g, ln1_b, w_router, router_bias,
           w_gate_e, w_up_e, w_down_e, w_gate_s, w_up_s, w_down_s, ln2_g, ln2_b):
    batch, seq, d = x.shape
    t = batch * seq
    n_exp = w_router.shape[-1]
    hcur = x.reshape(t, d)
    for l in range(DEPTH):
        w_bf = _regroup_w_in(w_in[l])
        log_gamma = jnp.log(1.0 - 2.0 ** (-5.0 - jnp.arange(RET_HEADS, dtype=F32)))
        hp = jnp.stack([-jnp.exp(gdn_a_log[l].astype(F32)), gdn_dt_bias[l].astype(F32), log_gamma])
        rot_c, rot_s = _rotary_tables(seq)
        proj, bat = _proj(hcur, w_bf, _tile(t, 512))
        oa, ob = _mixer(proj, bat, hp, gdn_conv_w[l].astype(F32), gdn_norm_w[l].reshape(1, -1).astype(F32),
                        rot_c, rot_s, batch, seq)
        hh, h_pk, idx_t, wts_t, counts = _post_mix(
            hcur, oa, ob, w_out[l].astype(BF16), ln1_g[l].reshape(1, d), ln1_b[l].reshape(1, d),
            w_router[l].T.astype(BF16), router_bias[l].reshape(n_exp, 1).astype(F32), _tile(t, 256))
        base = jnp.cumsum(counts, axis=0) - counts
        dest_t = _dest(idx_t, base, _tile(t, 512)).T.reshape(t * TOP_K)
        xs = _dispatch(dest_t, h_pk, _tile(t, 512))
        eo = _experts(base.reshape(n_exp).astype(jnp.int32), counts.reshape(n_exp).astype(jnp.int32),
                      xs, w_gate_e[l], w_up_e[l], w_down_e[l])
        hcur = _combine(dest_t, wts_t, hh, eo, w_gate_s[l].astype(BF16), w_up_s[l].astype(BF16),
                        w_down_s[l].astype(BF16), ln2_g[l].reshape(1, d), ln2_b[l].reshape(1, d), _tile(t, 256))
    return hcur.reshape(batch, seq, d)
```

```python
import math

import numpy as np
import jax
import jax.numpy as jnp
from jax import lax
from jax.experimental import pallas as pl
from jax.experimental.pallas import tpu as pltpu

GDN_HEADS = 4
GDN_DK = 128
GDN_DV = 128
GDN_CONV = 4
GDN_CHUNK = 64
RET_HEADS = 4
RET_DK = 64
RET_DV = 128
ROPE_BASE = 10000.0
N_GROUPS = 8
TOPK_GROUPS = 4
TOP_K = 8
ROUTED_SCALE = 2.5
DEPTH = 1
DN_ALPHA = (2.0 * DEPTH) ** 0.25
LN_EPS = 1e-5
NORM_EPS = 1e-6

GDN_QK = GDN_HEADS * GDN_DK
GDN_VW = GDN_HEADS * GDN_DV
RET_QK = RET_HEADS * RET_DK
RET_VW = RET_HEADS * RET_DV

LANES = 128
SUBLANES = 8
VMEM_LIMIT_BYTES = 56 * 1024 * 1024

SUPER = 256
NEG_INF = float("-inf")

COL_QA = 0
COL_KA = COL_QA + GDN_HEADS
COL_VA = COL_KA + GDN_HEADS
COL_ZA = COL_VA + GDN_HEADS
COL_QKB = COL_ZA + GDN_HEADS
COL_VB = COL_QKB + RET_HEADS
COL_GB = COL_VB + RET_HEADS
COL_BA = COL_GB + RET_HEADS
N_COLBLK = COL_BA + 1
PROJ_W = N_COLBLK * LANES
HEAD_W = GDN_HEADS * LANES
MIX_TS = 512
EXPERT_CHUNK = 512
EXPERT_PIECE = 64
PACK_ROWS = 4

BF16 = jnp.bfloat16
F32 = jnp.float32


def _dot(a, b):
    return jnp.dot(a, b, preferred_element_type=F32)


def _dot_nt(a, b):
    return lax.dot_general(a, b, (((1,), (1,)), ((), ())), preferred_element_type=F32)


def _silu(x):
    return x * jax.nn.sigmoid(x)


def _softplus(x):
    return jnp.maximum(x, 0.0) + jnp.log1p(jnp.exp(-jnp.abs(x)))


def _iota(shape, dim, dtype=jnp.int32):
    return lax.broadcasted_iota(dtype, shape, dim)


def _params(sem):
    return pltpu.CompilerParams(dimension_semantics=sem, vmem_limit_bytes=VMEM_LIMIT_BYTES)


def _load_token_tiles(ref, n, lead=()):
    parts = [ref[lead + (pl.ds(s, n, stride=SUBLANES), slice(None))] for s in range(SUBLANES)]
    return jnp.concatenate(parts, axis=1)


def _store_token_tiles(ref, val, lead=()):
    n = val.shape[0]
    for s in range(SUBLANES):
        ref[lead + (pl.ds(s, n, stride=SUBLANES), slice(None))] = val[:, s * LANES:(s + 1) * LANES]


def _tile_copy(src_ref, src_row, dst_ref, dst_row, sem, n_slots=1, slot_rows=SUBLANES):
    rows = n_slots * slot_rows
    aligned = lambda r: r if isinstance(r, int) else pl.multiple_of(r, slot_rows)
    return pltpu.make_async_copy(src_ref.at[pl.ds(aligned(src_row), rows)],
                                 dst_ref.at[pl.ds(aligned(dst_row), rows)], sem)


def _store_packed(ref, val, lead=()):
    n, width = val.shape
    bits = pltpu.bitcast(val.astype(BF16).astype(F32), jnp.uint32)
    words = (bits[:, width // 2:] & jnp.uint32(0xFFFF0000)) | (bits[:, :width // 2] >> 16)
    for s in range(PACK_ROWS):
        ref[lead + (pl.ds(s, n, stride=PACK_ROWS), slice(None))] = words[:, s * LANES:(s + 1) * LANES]


def _load_packed(ref, n, lead=(), dtype=F32):
    words = [ref[lead + (pl.ds(s, n, stride=PACK_ROWS), slice(None))] for s in range(PACK_ROWS)]
    lo = [pltpu.bitcast(w << 16, F32) for w in words]
    hi = [pltpu.bitcast(w & jnp.uint32(0xFFFF0000), F32) for w in words]
    return jnp.concatenate(lo + hi, axis=1).astype(dtype)


def _proj_kernel(x_ref, w_ref, proj_ref, bat_ref):
    xb = x_ref[...].astype(BF16)
    step = 4 * LANES
    for j in range(0, PROJ_W, step):
        n = min(step, PROJ_W - j)
        proj_ref[:, j:j + n] = _dot(xb, w_ref[:, j:j + n])
    ba = proj_ref[:, COL_BA * LANES:(COL_BA + 1) * LANES]
    bat_ref[...] = ba.T[0:SUBLANES, :]


def _proj(x2, w_bf, tm):
    t, d = x2.shape
    return pl.pallas_call(
        _proj_kernel,
        grid=(t // tm,),
        in_specs=[pl.BlockSpec((tm, d), lambda i: (i, 0)),
                  pl.BlockSpec((d, PROJ_W), lambda i: (0, 0))],
        out_specs=[pl.BlockSpec((tm, PROJ_W), lambda i: (i, 0)),
                   pl.BlockSpec((SUBLANES, tm), lambda i: (0, i))],
        out_shape=[jax.ShapeDtypeStruct((t, PROJ_W), F32),
                   jax.ShapeDtypeStruct((SUBLANES, t), F32)],
        compiler_params=_params(("parallel",)),
        name="proj",
    )(x2, w_bf)


def _mixer_kernel(hp_ref,
                  q_ref, k_ref, v_ref, z_ref, qkb_ref, vb_ref, gb_ref, ba_ref, bat_ref,
                  qt_ref, kt_ref, vt_ref,
                  cw_ref, nw_ref, rc_ref, rs_ref,
                  oa_ref, ob_ref, st_ref, st2_ref):
    j = pl.program_id(1)
    ts = q_ref.shape[0]
    n_chunk = SUPER // GDN_CHUNK

    @pl.when(j == 0)
    def _():
        st_ref[...] = jnp.zeros_like(st_ref)
        st2_ref[...] = jnp.zeros_like(st2_ref)

    ri = _iota((SUPER, SUPER), 0)
    ci = _iota((SUPER, SUPER), 1)
    shift = int(math.log2(GDN_CHUNK))
    same = (ri >> shift) == (ci >> shift)
    causal = same & (ci <= ri)
    strict = same & (ci < ri)
    causal_t = same & (ri <= ci)
    eye = jnp.where(ri == ci, 1.0, 0.0)
    lane = _iota((SUPER, LANES), 1)
    dpos = jnp.maximum((ri - ci).astype(F32), 0.0)
    lower = ri >= ci
    pos = _iota((SUPER, 1), 0).astype(F32)
    half = lane < RET_DK
    pair_lo = (lane % RET_DK) < (RET_DK // 2)

    def conv_silu(ref, tail_ref, c0, h, r0, s):
        cols = slice(h * LANES, (h + 1) * LANES)
        cur = ref[pl.ds(r0, SUPER), cols]
        inside = ref[pl.ds(pl.multiple_of(jnp.maximum(r0 - SUBLANES, 0), SUBLANES), SUBLANES), cols]
        prev = jnp.where(s > 0, inside, tail_ref[:, cols])
        prev = jnp.where((s == 0) & (j == 0), 0.0, prev)
        ext = jnp.concatenate([prev, cur], axis=0)
        acc = jnp.zeros((SUPER, LANES), F32)
        for jj in range(GDN_CONV):
            off = SUBLANES - (GDN_CONV - 1) + jj
            acc = acc + cw_ref[jj:jj + 1, c0 + h * LANES:c0 + (h + 1) * LANES] * ext[off:off + SUPER, :]
        return _silu(acc)

    def l2n(u):
        return u * lax.rsqrt(jnp.sum(u * u, axis=-1, keepdims=True) + NORM_EPS)

    heads = range(GDN_HEADS)
    hcols = [slice(h * LANES, (h + 1) * LANES) for h in heads]

    def gdn_heads(r0, s):
        q = [l2n(conv_silu(q_ref, qt_ref, 0, h, r0, s)) * (GDN_DK ** -0.5) for h in heads]
        k = [l2n(conv_silu(k_ref, kt_ref, GDN_QK, h, r0, s)) for h in heads]
        v = [conv_silu(v_ref, vt_ref, 2 * GDN_QK, h, r0, s) for h in heads]
        ba = ba_ref[pl.ds(r0, SUPER), :]
        beta, g_col, gl_col, decay_c, decay_s = [], [], [], [], []
        for h in heads:
            neg_a = hp_ref[0, h]
            dt_b = hp_ref[1, h]
            b_col = jnp.sum(jnp.where(lane == h, ba, 0.0), axis=1, keepdims=True)
            a_col = jnp.sum(jnp.where(lane == h + GDN_HEADS, ba, 0.0), axis=1, keepdims=True)
            a_row = bat_ref[h + GDN_HEADS:h + GDN_HEADS + 1, pl.ds(r0, SUPER)]
            beta.append(jax.nn.sigmoid(b_col))
            la_col = neg_a * _softplus(a_col + dt_b)
            la_row = neg_a * _softplus(a_row + dt_b)
            gc = jnp.sum(jnp.where(causal, la_row, 0.0), axis=1, keepdims=True)
            gr = jnp.sum(jnp.where(causal_t, la_col, 0.0), axis=0, keepdims=True)
            g_col.append(gc)
            gl_col.append(jnp.sum(jnp.where(same, la_row, 0.0), axis=1, keepdims=True))
            dc = jnp.exp(jnp.where(causal, gc - gr, NEG_INF))
            decay_c.append(dc)
            decay_s.append(jnp.where(strict, dc, 0.0))
        kb = [k[h] * beta[h] for h in heads]
        k_bf = [k[h].astype(BF16) for h in heads]
        low = [_dot_nt(kb[h].astype(BF16), k_bf[h]) * decay_s[h] for h in heads]
        p = [eye - low[h] for h in heads]
        sq = low
        for _ in range(shift - 1):
            sq_bf = [sq[h].astype(BF16) for h in heads]
            sq = [_dot(sq_bf[h], sq_bf[h]) for h in heads]
            p = [p[h] + _dot(p[h].astype(BF16), sq[h].astype(BF16)) for h in heads]
        eg = [jnp.exp(g_col[h]) for h in heads]
        rhs = [jnp.concatenate([v[h] * beta[h], kb[h] * eg[h]], axis=1).astype(BF16) for h in heads]
        wk = [_dot(p[h].astype(BF16), rhs[h]) for h in heads]
        w_val = [wk[h][:, :GDN_DV] for h in heads]
        k_cum = [wk[h][:, GDN_DV:].astype(BF16) for h in heads]
        attn = [(_dot_nt(q[h].astype(BF16), k_bf[h]) * decay_c[h]).astype(BF16) for h in heads]
        q_dec = [(q[h] * eg[h]).astype(BF16) for h in heads]
        kd_t = [(k[h] * jnp.exp(gl_col[h] - g_col[h])).T.astype(BF16) for h in heads]
        gl = [jnp.exp(gl_col[h]) for h in heads]
        st = [st_ref[h] for h in heads]
        outs = [[] for _ in heads]
        for c in range(n_chunk):
            lo = c * GDN_CHUNK
            hi = lo + GDN_CHUNK
            st_bf = [st[h].astype(BF16) for h in heads]
            v_new = [w_val[h][lo:hi] - _dot(k_cum[h][lo:hi], st_bf[h]) for h in heads]
            v_pad = []
            for h in heads:
                pieces = [jnp.zeros((GDN_CHUNK, GDN_DV), F32)] * n_chunk
                pieces[c] = v_new[h]
                v_pad.append(jnp.concatenate(pieces, axis=0).astype(BF16))
            for h in heads:
                outs[h].append(_dot(q_dec[h][lo:hi], st_bf[h]) + _dot(attn[h][lo:hi], v_pad[h]))
            st = [st[h] * gl[h][lo:lo + 1, :] + _dot(kd_t[h], v_pad[h]) for h in heads]
        for h in heads:
            st_ref[h] = st[h]
            o = jnp.concatenate(outs[h], axis=0)
            o = o * lax.rsqrt(jnp.mean(o * o, axis=-1, keepdims=True) + NORM_EPS) * nw_ref[...]
            oa_ref[pl.ds(r0, SUPER), hcols[h]] = (o * _silu(z_ref[pl.ds(r0, SUPER), hcols[h]])).astype(oa_ref.dtype)

    def ret_heads(r0):
        rc = rc_ref[pl.ds(r0, SUPER), :]
        rs = rs_ref[pl.ds(r0, SUPER), :]
        lg = [hp_ref[2, h] for h in heads]
        qm, km, vb = [], [], []
        for h in heads:
            x = qkb_ref[pl.ds(r0, SUPER), hcols[h]]
            swapped = jnp.where(pair_lo, pltpu.roll(x, LANES - RET_DK // 2, 1), pltpu.roll(x, RET_DK // 2, 1))
            xr = x * rc + swapped * rs
            qm.append(jnp.where(half, xr, 0.0))
            km.append(jnp.where(half, pltpu.roll(xr, LANES - RET_DK, 1), 0.0))
            vb.append(vb_ref[pl.ds(r0, SUPER), hcols[h]].astype(BF16))
        inner = [(_dot_nt(qm[h].astype(BF16), km[h].astype(BF16))
                  * jnp.where(lower, jnp.exp(lg[h] * dpos), 0.0)).astype(BF16) for h in heads]
        st2 = [st2_ref[h] for h in heads]
        ob = [_dot(inner[h], vb[h])
              + _dot((qm[h] * jnp.exp(lg[h] * (pos + 1.0))).astype(BF16), st2[h].astype(BF16)) for h in heads]
        for h in heads:
            g_chunk = jnp.exp(jnp.full((1, 1), SUPER, F32) * lg[h])
            st2_ref[h] = st2[h] * g_chunk + _dot((km[h] * jnp.exp(lg[h] * (SUPER - 1.0 - pos))).T.astype(BF16), vb[h])
        for h in heads:
            mu = jnp.mean(ob[h], axis=-1, keepdims=True)
            oc = ob[h] - mu
            oc = oc * lax.rsqrt(jnp.mean(oc * oc, axis=-1, keepdims=True) + NORM_EPS)
            ob_ref[pl.ds(r0, SUPER), hcols[h]] = (oc * _silu(gb_ref[pl.ds(r0, SUPER), hcols[h]])).astype(ob_ref.dtype)

    def body(s, carry):
        r0 = pl.multiple_of(s * SUPER, SUPER)
        gdn_heads(r0, s)
        ret_heads(r0)
        return carry

    lax.fori_loop(0, ts // SUPER, body, 0)


def _mixer(proj, bat, hp, conv_w, norm_w, rot_c, rot_s, batch, seq):
    t = proj.shape[0]
    ts = MIX_TS if seq % MIX_TS == 0 else seq
    nj = seq // ts
    grp = lambda c0: pl.BlockSpec((ts, HEAD_W), lambda b, j, c=c0 // GDN_HEADS: (b * nj + j, c))
    tail = lambda c0: pl.BlockSpec(
        (SUBLANES, HEAD_W),
        lambda b, j, c=c0 // GDN_HEADS: (jnp.maximum((b * seq + j * ts) // SUBLANES - 1, 0), c))
    const2 = lambda shape: pl.BlockSpec(shape, lambda b, j: (0, 0))
    tab = pl.BlockSpec((ts, LANES), lambda b, j: (j, 0))
    in_specs = [
        pl.BlockSpec(memory_space=pltpu.SMEM),
        grp(COL_QA), grp(COL_KA), grp(COL_VA), grp(COL_ZA), grp(COL_QKB), grp(COL_VB), grp(COL_GB),
        pl.BlockSpec((ts, LANES), lambda b, j: (b * nj + j, COL_BA)),
        pl.BlockSpec((SUBLANES, ts), lambda b, j: (0, b * nj + j)),
        tail(COL_QA), tail(COL_KA), tail(COL_VA),
        const2(conv_w.shape), const2((1, GDN_DV)), tab, tab,
    ]
    out_spec = pl.BlockSpec((ts, HEAD_W), lambda b, j: (b * nj + j, 0))
    return pl.pallas_call(
        _mixer_kernel,
        grid=(batch, nj),
        in_specs=in_specs,
        out_specs=[out_spec, out_spec],
        out_shape=[jax.ShapeDtypeStruct((t, GDN_VW), BF16), jax.ShapeDtypeStruct((t, RET_VW), BF16)],
        scratch_shapes=[pltpu.VMEM((GDN_HEADS, GDN_DK, GDN_DV), F32),
                        pltpu.VMEM((RET_HEADS, LANES, RET_DV), F32)],
        compiler_params=_params(("parallel", "arbitrary")),
        name="mixer",
    )(hp, proj, proj, proj, proj, proj, proj, proj, proj, bat, proj, proj, proj, conv_w, norm_w, rot_c, rot_s)


def _layer_norm(u, g, b):
    mu = jnp.mean(u, axis=-1, keepdims=True)
    uc = u - mu
    var = jnp.mean(uc * uc, axis=-1, keepdims=True)
    return uc * lax.rsqrt(var + LN_EPS) * g + b


def _post_mix_kernel(x_ref, oa_ref, ob_ref, wo_ref, g_ref, b_ref, wr_ref, rb_ref,
                     h_ref, hp_ref, idx_ref, wts_ref, cnt_ref):
    n_exp = wr_ref.shape[0]
    tm = x_ref.shape[0]
    per_grp = n_exp // N_GROUPS
    mix = _dot(oa_ref[...], wo_ref[:GDN_VW, :]) + _dot(ob_ref[...], wo_ref[GDN_VW:, :])
    hh = _layer_norm(DN_ALPHA * x_ref[...] + mix, g_ref[...], b_ref[...])
    _store_token_tiles(h_ref, hh)
    _store_packed(hp_ref, hh)
    scores = jax.nn.sigmoid(_dot_nt(wr_ref[...], hh.astype(BF16)))
    choice = scores + rb_ref[...]
    big = float(n_exp)
    io_g = _iota((per_grp, tm), 0).astype(F32)
    grp = []
    for g in range(N_GROUPS):
        blk = choice[g * per_grp:(g + 1) * per_grp, :]
        m1 = jnp.max(blk, axis=0, keepdims=True)
        i1 = jnp.min(jnp.where(blk == m1, io_g, big), axis=0, keepdims=True)
        m2 = jnp.max(jnp.where(io_g == i1, NEG_INF, blk), axis=0, keepdims=True)
        grp.append(m1 + m2)
    gsc = jnp.concatenate(grp, axis=0)
    io8 = _iota((N_GROUPS, tm), 0).astype(F32)
    gsel = jnp.zeros((N_GROUPS, tm), F32)
    for _ in range(TOPK_GROUPS):
        m = jnp.max(gsc, axis=0, keepdims=True)
        i = jnp.min(jnp.where(gsc == m, io8, big), axis=0, keepdims=True)
        hit = io8 == i
        gsel = jnp.where(hit, 1.0, gsel)
        gsc = jnp.where(hit, NEG_INF, gsc)
    masked = jnp.concatenate(
        [jnp.where(gsel[g:g + 1, :] > 0.0, choice[g * per_grp:(g + 1) * per_grp, :], NEG_INF)
         for g in range(N_GROUPS)], axis=0)
    io_e = _iota((n_exp, tm), 0).astype(F32)
    sel = jnp.zeros((n_exp, tm), F32)
    ids, ws = [], []
    for _ in range(TOP_K):
        m = jnp.max(masked, axis=0, keepdims=True)
        i = jnp.min(jnp.where(masked == m, io_e, big), axis=0, keepdims=True)
        hit = io_e == i
        ws.append(jnp.sum(jnp.where(hit, scores, 0.0), axis=0, keepdims=True))
        ids.append(i)
        masked = jnp.where(hit, NEG_INF, masked)
        sel = jnp.where(hit, 1.0, sel)
    w = jnp.concatenate(ws, axis=0)
    wts_ref[...] = w / jnp.sum(w, axis=0, keepdims=True) * ROUTED_SCALE
    idx_ref[...] = jnp.concatenate(ids, axis=0).astype(jnp.int32)

    @pl.when(pl.program_id(0) == 0)
    def _():
        cnt_ref[...] = jnp.zeros_like(cnt_ref)

    cnt_ref[...] += jnp.sum(sel, axis=1, keepdims=True)


def _post_mix(x2, oa, ob, wo_bf, g1, b1, wr_t, rbias, tm):
    t, d = x2.shape
    n_exp = wr_t.shape[0]
    row = lambda w: pl.BlockSpec((tm, w), lambda i: (i, 0))
    const = lambda shape: pl.BlockSpec(shape, lambda i: (0, 0))
    tok = pl.BlockSpec((TOP_K, tm), lambda i: (0, i))
    return pl.pallas_call(
        _post_mix_kernel,
        grid=(t // tm,),
        in_specs=[row(d), row(GDN_VW), row(RET_VW), const(wo_bf.shape), const((1, d)), const((1, d)),
                  const(wr_t.shape), const((n_exp, 1))],
        out_specs=[pl.BlockSpec((tm * SUBLANES, LANES), lambda i: (i, 0)),
                   pl.BlockSpec((tm * PACK_ROWS, LANES), lambda i: (i, 0)), tok, tok, const((n_exp, 1))],
        out_shape=[jax.ShapeDtypeStruct((t * SUBLANES, LANES), F32),
                   jax.ShapeDtypeStruct((t * PACK_ROWS, LANES), jnp.uint32),
                   jax.ShapeDtypeStruct((TOP_K, t), jnp.int32),
                   jax.ShapeDtypeStruct((TOP_K, t), F32),
                   jax.ShapeDtypeStruct((n_exp, 1), F32)],
        compiler_params=_params(("arbitrary",)),
        name="post_mix",
    )(x2, oa, ob, wo_bf, g1, b1, wr_t, rbias)


def _dest_kernel(idx_ref, base_ref, dest_ref, carry_ref):
    n_exp = base_ref.shape[0]
    tm = idx_ref.shape[1]

    @pl.when(pl.program_id(0) == 0)
    def _():
        carry_ref[...] = jnp.zeros_like(carry_ref)

    io_e = _iota((n_exp, tm), 0)
    idx = idx_ref[...]
    hits = [io_e == idx[k:k + 1, :] for k in range(TOP_K)]
    sel = jnp.zeros((n_exp, tm), F32)
    for hit in hits:
        sel = jnp.where(hit, 1.0, sel)
    before = jnp.where(_iota((tm, tm), 0) < _iota((tm, tm), 1), 1.0, 0.0).astype(BF16)
    rank = _dot(sel.astype(BF16), before)
    tot = base_ref[...] + carry_ref[...] + rank
    dest = [jnp.sum(jnp.where(hit, tot, 0.0), axis=0, keepdims=True) for hit in hits]
    dest_ref[...] = jnp.concatenate(dest, axis=0).astype(jnp.int32) * PACK_ROWS
    carry_ref[...] += jnp.sum(sel, axis=1, keepdims=True)


def _dest(idx_t, base, tm):
    t = idx_t.shape[1]
    n_exp = base.shape[0]
    tok = pl.BlockSpec((TOP_K, tm), lambda i: (0, i))
    return pl.pallas_call(
        _dest_kernel,
        grid=(t // tm,),
        in_specs=[tok, pl.BlockSpec((n_exp, 1), lambda i: (0, 0))],
        out_specs=tok,
        out_shape=jax.ShapeDtypeStruct((TOP_K, t), jnp.int32),
        scratch_shapes=[pltpu.VMEM((n_exp, 1), F32)],
        compiler_params=_params(("arbitrary",)),
        name="dest",
    )(idx_t, base)


def _dispatch_kernel(dest_ref, h_ref, xs_ref, zero_ref, sem, zsem):
    tm = h_ref.shape[0] // PACK_ROWS
    last = pl.num_programs(0) - 1
    pad_row = xs_ref.shape[0] - zero_ref.shape[0]

    @pl.when(pl.program_id(0) == last)
    def _():
        zero_ref[...] = jnp.zeros_like(zero_ref)
        _tile_copy(zero_ref, 0, xs_ref, pad_row, zsem, EXPERT_CHUNK, PACK_ROWS).start()

    def start(t, c):
        for k in range(TOP_K):
            _tile_copy(h_ref, t * PACK_ROWS, xs_ref, dest_ref[t * TOP_K + k], sem, 1, PACK_ROWS).start(priority=k % 2)
        return c

    lax.fori_loop(0, tm, start, 0)

    for k in range(TOP_K):
        _tile_copy(h_ref, 0, xs_ref, 0, sem, tm, PACK_ROWS).wait()

    @pl.when(pl.program_id(0) == last)
    def _():
        _tile_copy(zero_ref, 0, xs_ref, pad_row, zsem, EXPERT_CHUNK, PACK_ROWS).wait()


def _dispatch(dest_t, h_pk, tm):
    t = dest_t.shape[0] // TOP_K
    n_rows = (t * TOP_K + EXPERT_CHUNK) * PACK_ROWS
    return pl.pallas_call(
        _dispatch_kernel,
        grid=(t // tm,),
        in_specs=[pl.BlockSpec((tm * TOP_K,), lambda i: (i,), memory_space=pltpu.SMEM),
                  pl.BlockSpec((tm * PACK_ROWS, LANES), lambda i: (i, 0))],
        out_specs=pl.BlockSpec(memory_space=pl.ANY),
        out_shape=jax.ShapeDtypeStruct((n_rows, LANES), jnp.uint32),
        scratch_shapes=[pltpu.VMEM((EXPERT_CHUNK * PACK_ROWS, LANES), jnp.uint32),
                        pltpu.SemaphoreType.DMA(()), pltpu.SemaphoreType.DMA(())],
        compiler_params=_params(("arbitrary",)),
        name="dispatch",
    )(dest_t, h_pk)


def _chunk_pieces(n):
    pieces = [(j < n // EXPERT_PIECE, j * EXPERT_PIECE, EXPERT_PIECE) for j in range(EXPERT_CHUNK // EXPERT_PIECE)]
    size = EXPERT_PIECE // 2
    while size >= 1:
        shift = int(math.log2(size)) + 1
        pieces.append(((n & size) != 0, (n >> shift) << shift, size))
        size //= 2
    return pieces


ST_CHUNKS = 0
ST_ISSUED = 1
ST_CUR_E = 2
ST_CUR_I = 3
ST_WRITE = 4
N_XBUF = 4
N_OBUF = 3


def _expert_kernel(start_ref, count_ref, wg_ref, wu_ref, wd_ref, xs_ref, eo_ref,
                   xbuf, obuf, wgb, wub, wdb, st_ref, rsem, wsem):
    e = pl.program_id(0)
    n_exp = pl.num_programs(0)
    ch = EXPERT_CHUNK
    s0 = start_ref[e]
    cnt = count_ref[e]
    n_ch = (cnt + ch - 1) // ch

    def read_parts(first_slot, buf):
        return [_tile_copy(xs_ref, (first_slot + j * EXPERT_PIECE) * PACK_ROWS, xbuf.at[buf],
                           j * EXPERT_PIECE * PACK_ROWS, rsem.at[buf], EXPERT_PIECE, PACK_ROWS)
                for j in range(ch // EXPERT_PIECE)]

    def read_next():
        def exhausted(c):
            cnt_c = count_ref[jnp.minimum(c[0], n_exp - 1)]
            return jnp.logical_and(c[0] < n_exp, c[1] * ch >= cnt_c)

        ce, ci = lax.while_loop(exhausted, lambda c: (c[0] + 1, jnp.int32(0)),
                                (st_ref[ST_CUR_E], st_ref[ST_CUR_I]))

        @pl.when(ce < n_exp)
        def _():
            first_slot = start_ref[jnp.minimum(ce, n_exp - 1)] + ci * ch
            buf = lax.rem(st_ref[ST_ISSUED], N_XBUF)
            for j, cp in enumerate(read_parts(first_slot, buf)):
                cp.start(priority=j % 2)
            st_ref[ST_ISSUED] = st_ref[ST_ISSUED] + 1

        st_ref[ST_CUR_E] = ce
        st_ref[ST_CUR_I] = ci + 1

    def wait_read(buf):
        for cp in read_parts(0, buf):
            cp.wait()

    def out_pieces(buf, first_slot, n, wait):
        for j, (pred, off, size) in enumerate(_chunk_pieces(n)):
            @pl.when(pred)
            def _(j=j, off=off, size=size):
                cp = _tile_copy(obuf.at[buf], off * PACK_ROWS, eo_ref, (first_slot + off) * PACK_ROWS,
                                wsem.at[buf], size, PACK_ROWS)
                cp.wait() if wait else cp.start(priority=j % 2)

    def drain(buf):
        out_pieces(buf, st_ref[ST_WRITE + 2 * buf], st_ref[ST_WRITE + 2 * buf + 1], True)
        st_ref[ST_WRITE + 2 * buf + 1] = 0

    @pl.when(e == 0)
    def _():
        for i in range(st_ref.shape[0]):
            st_ref[i] = 0
        for _ in range(N_XBUF - 1):
            read_next()

    @pl.when(n_ch > 0)
    def _():
        wgb[...] = wg_ref[0].astype(BF16)
        wub[...] = wu_ref[0].astype(BF16)
        wdb[...] = wd_ref[0].astype(BF16)

    def body(i, c):
        g = st_ref[ST_CHUNKS]
        buf = lax.rem(g, N_XBUF)
        wait_read(buf)
        read_next()
        obf = lax.rem(g, N_OBUF)
        drain(obf)
        first_slot = s0 + i * ch
        n = jnp.minimum(ch, cnt - i * ch)

        def swiglu(rows):
            xb = _load_packed(xbuf, rows, lead=(buf,), dtype=BF16)
            hid = _silu(_dot(xb, wgb[...])) * _dot(xb, wub[...])
            _store_packed(obuf, _dot(hid.astype(BF16), wdb[...]), lead=(obf,))

        pl.when(n > ch // 2)(lambda: swiglu(ch))
        pl.when(n <= ch // 2)(lambda: swiglu(ch // 2))
        out_pieces(obf, first_slot, n, False)
        st_ref[ST_WRITE + 2 * obf] = first_slot
        st_ref[ST_WRITE + 2 * obf + 1] = n
        st_ref[ST_CHUNKS] = g + 1
        return c

    lax.fori_loop(0, n_ch, body, 0)

    @pl.when(e == n_exp - 1)
    def _():
        for b in range(N_OBUF):
            drain(b)


def _experts(starts, counts, xs, wg, wu, wd):
    n_exp, d, d_e = wg.shape
    n_slot = xs.shape[0] // PACK_ROWS - EXPERT_CHUNK
    rows = EXPERT_CHUNK * PACK_ROWS
    grid_spec = pltpu.PrefetchScalarGridSpec(
        num_scalar_prefetch=2,
        grid=(n_exp,),
        in_specs=[pl.BlockSpec((1, d, d_e), lambda e, st, ct: (e, 0, 0)),
                  pl.BlockSpec((1, d, d_e), lambda e, st, ct: (e, 0, 0)),
                  pl.BlockSpec((1, d_e, d), lambda e, st, ct: (e, 0, 0)),
                  pl.BlockSpec(memory_space=pl.ANY)],
        out_specs=pl.BlockSpec(memory_space=pl.ANY),
        scratch_shapes=[pltpu.VMEM((N_XBUF, rows, LANES), jnp.uint32), pltpu.VMEM((N_OBUF, rows, LANES), jnp.uint32),
                        pltpu.VMEM((d, d_e), BF16), pltpu.VMEM((d, d_e), BF16), pltpu.VMEM((d_e, d), BF16),
                        pltpu.SMEM((ST_WRITE + 2 * N_OBUF,), jnp.int32),
                        pltpu.SemaphoreType.DMA((N_XBUF,)), pltpu.SemaphoreType.DMA((N_OBUF,))],
    )
    return pl.pallas_call(
        _expert_kernel,
        grid_spec=grid_spec,
        out_shape=jax.ShapeDtypeStruct((n_slot * PACK_ROWS, LANES), jnp.uint32),
        compiler_params=_params(("arbitrary",)),
        name="experts",
    )(starts, counts, wg, wu, wd, xs)


def _combine_kernel(dest_ref, dnext_ref, wts_ref, h_ref, eo_ref, wgs_ref, wus_ref, wds_ref, g_ref, b_ref,
                    out_ref, gbuf, sem):
    i = pl.program_id(0)
    tm = out_ref.shape[0]
    buf = lax.rem(i, 2)

    def gather(d_ref, bf):
        def start(t, c):
            for k in range(TOP_K):
                _tile_copy(eo_ref, d_ref[t * TOP_K + k], gbuf.at[bf, k], t * PACK_ROWS, sem.at[bf],
                           1, PACK_ROWS).start(priority=k % 2)
            return c

        lax.fori_loop(0, tm, start, 0)

    @pl.when(i == 0)
    def _():
        gather(dest_ref, 0)

    def step(bf):
        for t in range(tm):
            for k in range(TOP_K):
                _tile_copy(eo_ref, dnext_ref[t * TOP_K + k], gbuf.at[1 - bf, k], t * PACK_ROWS, sem.at[1 - bf],
                           1, PACK_ROWS).start(priority=k % 2)
        hh = _load_token_tiles(h_ref, tm)
        hb = hh.astype(BF16)
        hid = _silu(_dot(hb, wgs_ref[...])) * _dot(hb, wus_ref[...])
        acc = DN_ALPHA * hh + _dot(hid.astype(BF16), wds_ref[...])
        w_t = jnp.concatenate([wts_ref[...], jnp.zeros((LANES - TOP_K, tm), F32)], axis=0).T
        for k in range(TOP_K):
            _tile_copy(eo_ref, 0, gbuf.at[bf, k], 0, sem.at[bf], tm, PACK_ROWS).wait()
        for k in range(TOP_K):
            acc = acc + _load_packed(gbuf, tm, lead=(bf, k)) * w_t[:, k:k + 1]
        out_ref[...] = _layer_norm(acc, g_ref[...], b_ref[...])

        @pl.when(i == pl.num_programs(0) - 1)
        def _():
            for k in range(TOP_K):
                _tile_copy(eo_ref, 0, gbuf.at[1 - bf, k], 0, sem.at[1 - bf], tm, PACK_ROWS).wait()

    for bf in range(2):
        pl.when(buf == bf)(lambda bf=bf: step(bf))


def _combine(dest_t, wts_t, h_tt, eo, wgs, wus, wds, g2, b2, tm):
    t = dest_t.shape[0] // TOP_K
    d = wgs.shape[0]
    const = lambda shape: pl.BlockSpec(shape, lambda i: (0, 0))
    n_tiles = t // tm
    return pl.pallas_call(
        _combine_kernel,
        grid=(n_tiles,),
        in_specs=[pl.BlockSpec((tm * TOP_K,), lambda i: (i,), memory_space=pltpu.SMEM),
                  pl.BlockSpec((tm * TOP_K,), lambda i: (jnp.minimum(i + 1, n_tiles - 1),), memory_space=pltpu.SMEM),
                  pl.BlockSpec((TOP_K, tm), lambda i: (0, i)),
                  pl.BlockSpec((tm * SUBLANES, LANES), lambda i: (i, 0)),
                  pl.BlockSpec(memory_space=pl.ANY),
                  const(wgs.shape), const(wus.shape), const(wds.shape), const((1, d)), const((1, d))],
        out_specs=pl.BlockSpec((tm, d), lambda i: (i, 0)),
        out_shape=jax.ShapeDtypeStruct((t, d), F32),
        scratch_shapes=[pltpu.VMEM((2, TOP_K, tm * PACK_ROWS, LANES), jnp.uint32), pltpu.SemaphoreType.DMA((2,))],
        compiler_params=_params(("arbitrary",)),
        name="combine",
    )(dest_t, dest_t, wts_t, h_tt, eo, wgs, wus, wds, g2, b2)


def _regroup_w_in(w):
    d = w.shape[0]
    splits = np.cumsum([GDN_QK, GDN_QK, GDN_VW, GDN_VW, GDN_HEADS, GDN_HEADS, RET_QK, RET_QK, RET_VW])
    qa, ka, va, za, ba, aa, qb, kb, vb, gb = jnp.split(w, splits.tolist(), axis=1)
    ba_blk = jnp.concatenate([ba, aa, jnp.zeros((d, LANES - 2 * GDN_HEADS), w.dtype)], axis=1)
    qb = qb.reshape(d, RET_HEADS, RET_DK // 2, 2)
    kb = kb.reshape(d, RET_HEADS, RET_DK // 2, 2)
    qkb = jnp.concatenate([qb[..., 0], qb[..., 1], kb[..., 0], kb[..., 1]], axis=-1).reshape(d, RET_HEADS * LANES)
    return jnp.concatenate([qa, ka, va, za, qkb, vb, gb, ba_blk], axis=1).astype(BF16)


def _rotary_tables(seq):
    inv = 1.0 / (ROPE_BASE ** jnp.linspace(0.0, 1.0, RET_DK // 2, dtype=F32))
    ang = jnp.arange(seq, dtype=F32)[:, None] * inv[None, :]
    c, s = jnp.cos(ang), jnp.sin(ang)
    ks = RET_DK ** -0.5
    rot_c = jnp.concatenate([c, c, c * ks, c * ks], axis=1)
    rot_s = jnp.concatenate([-s, s, -s * ks, s * ks], axis=1)
    return rot_c, rot_s


def _tile(n, pref):
    return pref if n % pref == 0 else n


def kernel(x, w_in, gdn_conv_w, gdn_a_log, gdn_dt_bias, gdn_norm_w, w_out, ln1_g, ln1_b, w_router, router_bias,
           w_gate_e, w_up_e, w_down_e, w_gate_s, w_up_s, w_down_s, ln2_g, ln2_b):
    batch, seq, d = x.shape
    t = batch * seq
    n_exp = w_router.shape[-1]
    hcur = x.reshape(t, d)
    for l in range(DEPTH):
        w_bf = _regroup_w_in(w_in[l])
        log_gamma = jnp.log(1.0 - 2.0 ** (-5.0 - jnp.arange(RET_HEADS, dtype=F32)))
        hp = jnp.stack([-jnp.exp(gdn_a_log[l].astype(F32)), gdn_dt_bias[l].astype(F32), log_gamma])
        rot_c, rot_s = _rotary_tables(seq)
        proj, bat = _proj(hcur, w_bf, _tile(t, 512))
        oa, ob = _mixer(proj, bat, hp, gdn_conv_w[l].astype(F32), gdn_norm_w[l].reshape(1, -1).astype(F32),
                        rot_c, rot_s, batch, seq)
        hh, h_pk, idx_t, wts_t, counts = _post_mix(
            hcur, oa, ob, w_out[l].astype(BF16), ln1_g[l].reshape(1, d), ln1_b[l].reshape(1, d),
            w_router[l].T.astype(BF16), router_bias[l].reshape(n_exp, 1).astype(F32), _tile(t, 256))
        base = jnp.cumsum(counts, axis=0) - counts
        dest_t = _dest(idx_t, base, _tile(t, 512)).T.reshape(t * TOP_K)
        xs = _dispatch(dest_t, h_pk, _tile(t, 512))
        eo = _experts(base.reshape(n_exp).astype(jnp.int32), counts.reshape(n_exp).astype(jnp.int32),
                      xs, w_gate_e[l], w_up_e[l], w_down_e[l])
        hcur = _combine(dest_t, wts_t, hh, eo, w_gate_s[l].astype(BF16), w_up_s[l].astype(BF16),
                        w_down_s[l].astype(BF16), ln2_g[l].reshape(1, d), ln2_b[l].reshape(1, d), _tile(t, 128))
    return hcur.reshape(batch, seq, d)
```

```python
import math

import numpy as np
import jax
import jax.numpy as jnp
from jax import lax
from jax.experimental import pallas as pl
from jax.experimental.pallas import tpu as pltpu

GDN_HEADS = 4
GDN_DK = 128
GDN_DV = 128
GDN_CONV = 4
GDN_CHUNK = 64
RET_HEADS = 4
RET_DK = 64
RET_DV = 128
ROPE_BASE = 10000.0
N_GROUPS = 8
TOPK_GROUPS = 4
TOP_K = 8
ROUTED_SCALE = 2.5
DEPTH = 1
DN_ALPHA = (2.0 * DEPTH) ** 0.25
LN_EPS = 1e-5
NORM_EPS = 1e-6

GDN_QK = GDN_HEADS * GDN_DK
GDN_VW = GDN_HEADS * GDN_DV
RET_QK = RET_HEADS * RET_DK
RET_VW = RET_HEADS * RET_DV

LANES = 128
SUBLANES = 8
VMEM_LIMIT_BYTES = 56 * 1024 * 1024

SUPER = 256
NEG_INF = float("-inf")

COL_QA = 0
COL_KA = COL_QA + GDN_HEADS
COL_VA = COL_KA + GDN_HEADS
COL_ZA = COL_VA + GDN_HEADS
COL_QKB = COL_ZA + GDN_HEADS
COL_VB = COL_QKB + RET_HEADS
COL_GB = COL_VB + RET_HEADS
COL_BA = COL_GB + RET_HEADS
N_COLBLK = COL_BA + 1
PROJ_W = N_COLBLK * LANES
HEAD_W = GDN_HEADS * LANES
MIX_TS = 512
EXPERT_CHUNK = 512
EXPERT_PIECE = 64
PACK_ROWS = 4

BF16 = jnp.bfloat16
F32 = jnp.float32


def _dot(a, b):
    return jnp.dot(a, b, preferred_element_type=F32)


def _dot_nt(a, b):
    return lax.dot_general(a, b, (((1,), (1,)), ((), ())), preferred_element_type=F32)


def _silu(x):
    return x * jax.nn.sigmoid(x)


def _softplus(x):
    return jnp.maximum(x, 0.0) + jnp.log1p(jnp.exp(-jnp.abs(x)))


def _iota(shape, dim, dtype=jnp.int32):
    return lax.broadcasted_iota(dtype, shape, dim)


def _params(sem):
    return pltpu.CompilerParams(dimension_semantics=sem, vmem_limit_bytes=VMEM_LIMIT_BYTES)


def _load_token_tiles(ref, n, lead=()):
    parts = [ref[lead + (pl.ds(s, n, stride=SUBLANES), slice(None))] for s in range(SUBLANES)]
    return jnp.concatenate(parts, axis=1)


def _store_token_tiles(ref, val, lead=()):
    n = val.shape[0]
    for s in range(SUBLANES):
        ref[lead + (pl.ds(s, n, stride=SUBLANES), slice(None))] = val[:, s * LANES:(s + 1) * LANES]


def _tile_copy(src_ref, src_row, dst_ref, dst_row, sem, n_slots=1, slot_rows=SUBLANES):
    rows = n_slots * slot_rows
    aligned = lambda r: r if isinstance(r, int) else pl.multiple_of(r, slot_rows)
    return pltpu.make_async_copy(src_ref.at[pl.ds(aligned(src_row), rows)],
                                 dst_ref.at[pl.ds(aligned(dst_row), rows)], sem)


def _store_packed(ref, val, lead=()):
    n, width = val.shape
    bits = pltpu.bitcast(val.astype(BF16).astype(F32), jnp.uint32)
    words = (bits[:, width // 2:] & jnp.uint32(0xFFFF0000)) | (bits[:, :width // 2] >> 16)
    for s in range(PACK_ROWS):
        ref[lead + (pl.ds(s, n, stride=PACK_ROWS), slice(None))] = words[:, s * LANES:(s + 1) * LANES]


def _load_packed(ref, n, lead=(), dtype=F32):
    words = [ref[lead + (pl.ds(s, n, stride=PACK_ROWS), slice(None))] for s in range(PACK_ROWS)]
    lo = [pltpu.bitcast(w << 16, F32) for w in words]
    hi = [pltpu.bitcast(w & jnp.uint32(0xFFFF0000), F32) for w in words]
    return jnp.concatenate(lo + hi, axis=1).astype(dtype)


def _mixer_kernel(hp_ref,
                  x_ref, xn_ref,
                  w_ref, cw_ref, nw_ref, rc_ref, rs_ref,
                  oa_ref, ob_ref, proj_ref, st_ref, st2_ref):
    j = pl.program_id(1)
    ts = x_ref.shape[0]
    n_super = ts // SUPER
    n_chunk = SUPER // GDN_CHUNK

    def project(rows, slot):
        xb = rows.astype(BF16)
        step = 4 * LANES
        for c in range(0, PROJ_W, step):
            n = min(step, PROJ_W - c)
            proj_ref[slot, :, c:c + n] = _dot(xb, w_ref[:, c:c + n])

    @pl.when(j == 0)
    def _():
        st_ref[...] = jnp.zeros_like(st_ref)
        st2_ref[...] = jnp.zeros_like(st2_ref)

    @pl.when((j == 0) & (pl.program_id(0) == 0))
    def _():
        proj_ref[1] = jnp.zeros(proj_ref.shape[1:], F32)
        project(x_ref[0:SUPER, :], 0)

    ri = _iota((SUPER, SUPER), 0)
    ci = _iota((SUPER, SUPER), 1)
    shift = int(math.log2(GDN_CHUNK))
    same = (ri >> shift) == (ci >> shift)
    causal = same & (ci <= ri)
    strict = same & (ci < ri)
    causal_t = same & (ri <= ci)
    eye = jnp.where(ri == ci, 1.0, 0.0)
    lane = _iota((SUPER, LANES), 1)
    dpos = jnp.maximum((ri - ci).astype(F32), 0.0)
    lower = ri >= ci
    pos = _iota((SUPER, 1), 0).astype(F32)
    half = lane < RET_DK
    pair_lo = (lane % RET_DK) < (RET_DK // 2)

    def pcols(col_blk, h):
        return slice((col_blk + h) * LANES, (col_blk + h + 1) * LANES)

    def conv_silu(col_blk, h, slot, s, tail):
        cur = proj_ref[slot, :, pcols(col_blk, h)]
        prev = jnp.where((s == 0) & (j == 0), 0.0, tail[:, pcols(col_blk, h)])
        ext = jnp.concatenate([prev, cur], axis=0)
        acc = jnp.zeros((SUPER, LANES), F32)
        for jj in range(GDN_CONV):
            off = SUBLANES - (GDN_CONV - 1) + jj
            acc = acc + cw_ref[jj:jj + 1, pcols(col_blk, h)] * ext[off:off + SUPER, :]
        return _silu(acc)

    def l2n(u):
        return u * lax.rsqrt(jnp.sum(u * u, axis=-1, keepdims=True) + NORM_EPS)

    heads = range(GDN_HEADS)
    hcols = [slice(h * LANES, (h + 1) * LANES) for h in heads]

    def gdn_heads(r0, s, slot, tail):
        q = [l2n(conv_silu(COL_QA, h, slot, s, tail)) * (GDN_DK ** -0.5) for h in heads]
        k = [l2n(conv_silu(COL_KA, h, slot, s, tail)) for h in heads]
        v = [conv_silu(COL_VA, h, slot, s, tail) for h in heads]
        ba = proj_ref[slot, :, pcols(COL_BA, 0)]
        ba_t = ba.T
        beta, g_col, gl_col, decay_c, decay_s = [], [], [], [], []
        for h in heads:
            neg_a = hp_ref[0, h]
            dt_b = hp_ref[1, h]
            b_col = jnp.sum(jnp.where(lane == h, ba, 0.0), axis=1, keepdims=True)
            a_col = jnp.sum(jnp.where(lane == h + GDN_HEADS, ba, 0.0), axis=1, keepdims=True)
            a_row = ba_t[h + GDN_HEADS:h + GDN_HEADS + 1, :]
            beta.append(jax.nn.sigmoid(b_col))
            la_col = neg_a * _softplus(a_col + dt_b)
            la_row = neg_a * _softplus(a_row + dt_b)
            gc = jnp.sum(jnp.where(causal, la_row, 0.0), axis=1, keepdims=True)
            gr = jnp.sum(jnp.where(causal_t, la_col, 0.0), axis=0, keepdims=True)
            g_col.append(gc)
            gl_col.append(jnp.sum(jnp.where(same, la_row, 0.0), axis=1, keepdims=True))
            dc = jnp.exp(jnp.where(causal, gc - gr, NEG_INF))
            decay_c.append(dc)
            decay_s.append(jnp.where(strict, dc, 0.0))
        kb = [k[h] * beta[h] for h in heads]
        k_bf = [k[h].astype(BF16) for h in heads]
        low = [_dot_nt(kb[h].astype(BF16), k_bf[h]) * decay_s[h] for h in heads]
        p = [eye - low[h] for h in heads]
        sq = low
        for _ in range(shift - 1):
            sq_bf = [sq[h].astype(BF16) for h in heads]
            sq = [_dot(sq_bf[h], sq_bf[h]) for h in heads]
            p = [p[h] + _dot(p[h].astype(BF16), sq[h].astype(BF16)) for h in heads]
        eg = [jnp.exp(g_col[h]) for h in heads]
        rhs = [jnp.concatenate([v[h] * beta[h], kb[h] * eg[h]], axis=1).astype(BF16) for h in heads]
        wk = [_dot(p[h].astype(BF16), rhs[h]) for h in heads]
        w_val = [wk[h][:, :GDN_DV] for h in heads]
        k_cum = [wk[h][:, GDN_DV:].astype(BF16) for h in heads]
        attn = [(_dot_nt(q[h].astype(BF16), k_bf[h]) * decay_c[h]).astype(BF16) for h in heads]
        q_dec = [(q[h] * eg[h]).astype(BF16) for h in heads]
        kd_t = [(k[h] * jnp.exp(gl_col[h] - g_col[h])).T.astype(BF16) for h in heads]
        gl = [jnp.exp(gl_col[h]) for h in heads]
        st = [st_ref[h] for h in heads]
        outs = [[] for _ in heads]
        for c in range(n_chunk):
            lo = c * GDN_CHUNK
            hi = lo + GDN_CHUNK
            st_bf = [st[h].astype(BF16) for h in heads]
            v_new = [w_val[h][lo:hi] - _dot(k_cum[h][lo:hi], st_bf[h]) for h in heads]
            v_pad = []
            for h in heads:
                pieces = [jnp.zeros((GDN_CHUNK, GDN_DV), F32)] * n_chunk
                pieces[c] = v_new[h]
                v_pad.append(jnp.concatenate(pieces, axis=0).astype(BF16))
            for h in heads:
                outs[h].append(_dot(q_dec[h][lo:hi], st_bf[h]) + _dot(attn[h][lo:hi], v_pad[h]))
            st = [st[h] * gl[h][lo:lo + 1, :] + _dot(kd_t[h], v_pad[h]) for h in heads]
        for h in heads:
            st_ref[h] = st[h]
            o = jnp.concatenate(outs[h], axis=0)
            o = o * lax.rsqrt(jnp.mean(o * o, axis=-1, keepdims=True) + NORM_EPS) * nw_ref[...]
            z = proj_ref[slot, :, pcols(COL_ZA, h)]
            oa_ref[pl.ds(r0, SUPER), hcols[h]] = (o * _silu(z)).astype(oa_ref.dtype)

    def ret_heads(r0, slot):
        rc = rc_ref[pl.ds(r0, SUPER), :]
        rs = rs_ref[pl.ds(r0, SUPER), :]
        lg = [hp_ref[2, h] for h in heads]
        qm, km, vb = [], [], []
        for h in heads:
            x = proj_ref[slot, :, pcols(COL_QKB, h)]
            swapped = jnp.where(pair_lo, pltpu.roll(x, LANES - RET_DK // 2, 1), pltpu.roll(x, RET_DK // 2, 1))
            xr = x * rc + swapped * rs
            qm.append(jnp.where(half, xr, 0.0))
            km.append(jnp.where(half, pltpu.roll(xr, LANES - RET_DK, 1), 0.0))
            vb.append(proj_ref[slot, :, pcols(COL_VB, h)].astype(BF16))
        inner = [(_dot_nt(qm[h].astype(BF16), km[h].astype(BF16))
                  * jnp.where(lower, jnp.exp(lg[h] * dpos), 0.0)).astype(BF16) for h in heads]
        st2 = [st2_ref[h] for h in heads]
        ob = [_dot(inner[h], vb[h])
              + _dot((qm[h] * jnp.exp(lg[h] * (pos + 1.0))).astype(BF16), st2[h].astype(BF16)) for h in heads]
        for h in heads:
            g_chunk = jnp.exp(jnp.full((1, 1), SUPER, F32) * lg[h])
            st2_ref[h] = st2[h] * g_chunk + _dot((km[h] * jnp.exp(lg[h] * (SUPER - 1.0 - pos))).T.astype(BF16), vb[h])
        for h in heads:
            mu = jnp.mean(ob[h], axis=-1, keepdims=True)
            oc = ob[h] - mu
            oc = oc * lax.rsqrt(jnp.mean(oc * oc, axis=-1, keepdims=True) + NORM_EPS)
            gate = proj_ref[slot, :, pcols(COL_GB, h)]
            ob_ref[pl.ds(r0, SUPER), hcols[h]] = (oc * _silu(gate)).astype(ob_ref.dtype)

    for s in range(n_super):
        r0 = s * SUPER
        slot = s % 2
        tail = proj_ref[1 - slot, SUPER - SUBLANES:SUPER, 0:COL_ZA * LANES]
        nxt = x_ref[r0 + SUPER:r0 + 2 * SUPER, :] if s + 1 < n_super else xn_ref[...]
        project(nxt, 1 - slot)
        gdn_heads(r0, s, slot, tail)
        ret_heads(r0, slot)


def _mixer(x2, w_bf, hp, conv_w, norm_w, rot_c, rot_s, batch, seq):
    t, d = x2.shape
    ts = MIX_TS if seq % MIX_TS == 0 else seq
    nj = seq // ts
    per_blk = ts // SUPER
    assert per_blk % 2 == 0
    last_chunk = t // SUPER - 1
    const2 = lambda shape: pl.BlockSpec(shape, lambda b, j: (0, 0))
    tab = pl.BlockSpec((ts, LANES), lambda b, j: (j, 0))
    in_specs = [
        pl.BlockSpec(memory_space=pltpu.SMEM),
        pl.BlockSpec((ts, d), lambda b, j: (b * nj + j, 0)),
        pl.BlockSpec((SUPER, d), lambda b, j: (jnp.minimum((b * nj + j + 1) * per_blk, last_chunk), 0)),
        const2(w_bf.shape), const2(conv_w.shape), const2((1, GDN_DV)), tab, tab,
    ]
    out_spec = pl.BlockSpec((ts, HEAD_W), lambda b, j: (b * nj + j, 0))
    return pl.pallas_call(
        _mixer_kernel,
        grid=(batch, nj),
        in_specs=in_specs,
        out_specs=[out_spec, out_spec],
        out_shape=[jax.ShapeDtypeStruct((t, GDN_VW), BF16), jax.ShapeDtypeStruct((t, RET_VW), BF16)],
        scratch_shapes=[pltpu.VMEM((2, SUPER, PROJ_W), F32),
                        pltpu.VMEM((GDN_HEADS, GDN_DK, GDN_DV), F32),
                        pltpu.VMEM((RET_HEADS, LANES, RET_DV), F32)],
        compiler_params=_params(("arbitrary", "arbitrary")),
        name="mixer",
    )(hp, x2, x2, w_bf, conv_w, norm_w, rot_c, rot_s)


def _layer_norm(u, g, b):
    mu = jnp.mean(u, axis=-1, keepdims=True)
    uc = u - mu
    var = jnp.mean(uc * uc, axis=-1, keepdims=True)
    return uc * lax.rsqrt(var + LN_EPS) * g + b


def _post_mix_kernel(x_ref, oa_ref, ob_ref, wo_ref, g_ref, b_ref, wr_ref, rb_ref,
                     h_ref, hp_ref, idx_ref, wts_ref, cnt_ref):
    n_exp = wr_ref.shape[0]
    tm = x_ref.shape[0]
    per_grp = n_exp // N_GROUPS
    mix = _dot(oa_ref[...], wo_ref[:GDN_VW, :]) + _dot(ob_ref[...], wo_ref[GDN_VW:, :])
    hh = _layer_norm(DN_ALPHA * x_ref[...] + mix, g_ref[...], b_ref[...])
    _store_token_tiles(h_ref, hh)
    _store_packed(hp_ref, hh)
    scores = jax.nn.sigmoid(_dot_nt(wr_ref[...], hh.astype(BF16)))
    choice = scores + rb_ref[...]
    big = float(n_exp)
    io_g = _iota((per_grp, tm), 0).astype(F32)
    grp = []
    for g in range(N_GROUPS):
        blk = choice[g * per_grp:(g + 1) * per_grp, :]
        m1 = jnp.max(blk, axis=0, keepdims=True)
        i1 = jnp.min(jnp.where(blk == m1, io_g, big), axis=0, keepdims=True)
        m2 = jnp.max(jnp.where(io_g == i1, NEG_INF, blk), axis=0, keepdims=True)
        grp.append(m1 + m2)
    gsc = jnp.concatenate(grp, axis=0)
    io8 = _iota((N_GROUPS, tm), 0).astype(F32)
    gsel = jnp.zeros((N_GROUPS, tm), F32)
    for _ in range(TOPK_GROUPS):
        m = jnp.max(gsc, axis=0, keepdims=True)
        i = jnp.min(jnp.where(gsc == m, io8, big), axis=0, keepdims=True)
        hit = io8 == i
        gsel = jnp.where(hit, 1.0, gsel)
        gsc = jnp.where(hit, NEG_INF, gsc)
    masked = jnp.concatenate(
        [jnp.where(gsel[g:g + 1, :] > 0.0, choice[g * per_grp:(g + 1) * per_grp, :], NEG_INF)
         for g in range(N_GROUPS)], axis=0)
    io_e = _iota((n_exp, tm), 0).astype(F32)
    sel = jnp.zeros((n_exp, tm), F32)
    ids, ws = [], []
    for _ in range(TOP_K):
        m = jnp.max(masked, axis=0, keepdims=True)
        i = jnp.min(jnp.where(masked == m, io_e, big), axis=0, keepdims=True)
        hit = io_e == i
        ws.append(jnp.sum(jnp.where(hit, scores, 0.0), axis=0, keepdims=True))
        ids.append(i)
        masked = jnp.where(hit, NEG_INF, masked)
        sel = jnp.where(hit, 1.0, sel)
    w = jnp.concatenate(ws, axis=0)
    wts_ref[...] = w / jnp.sum(w, axis=0, keepdims=True) * ROUTED_SCALE
    idx_ref[...] = jnp.concatenate(ids, axis=0).astype(jnp.int32)

    @pl.when(pl.program_id(0) == 0)
    def _():
        cnt_ref[...] = jnp.zeros_like(cnt_ref)

    cnt_ref[...] += jnp.sum(sel, axis=1, keepdims=True)


def _post_mix(x2, oa, ob, wo_bf, g1, b1, wr_t, rbias, tm):
    t, d = x2.shape
    n_exp = wr_t.shape[0]
    row = lambda w: pl.BlockSpec((tm, w), lambda i: (i, 0))
    const = lambda shape: pl.BlockSpec(shape, lambda i: (0, 0))
    tok = pl.BlockSpec((TOP_K, tm), lambda i: (0, i))
    return pl.pallas_call(
        _post_mix_kernel,
        grid=(t // tm,),
        in_specs=[row(d), row(GDN_VW), row(RET_VW), const(wo_bf.shape), const((1, d)), const((1, d)),
                  const(wr_t.shape), const((n_exp, 1))],
        out_specs=[pl.BlockSpec((tm * SUBLANES, LANES), lambda i: (i, 0)),
                   pl.BlockSpec((tm * PACK_ROWS, LANES), lambda i: (i, 0)), tok, tok, const((n_exp, 1))],
        out_shape=[jax.ShapeDtypeStruct((t * SUBLANES, LANES), F32),
                   jax.ShapeDtypeStruct((t * PACK_ROWS, LANES), jnp.uint32),
                   jax.ShapeDtypeStruct((TOP_K, t), jnp.int32),
                   jax.ShapeDtypeStruct((TOP_K, t), F32),
                   jax.ShapeDtypeStruct((n_exp, 1), F32)],
        compiler_params=_params(("arbitrary",)),
        name="post_mix",
    )(x2, oa, ob, wo_bf, g1, b1, wr_t, rbias)


def _dest_kernel(idx_ref, base_ref, dest_ref, carry_ref):
    n_exp = base_ref.shape[0]
    tm = idx_ref.shape[1]

    @pl.when(pl.program_id(0) == 0)
    def _():
        carry_ref[...] = jnp.zeros_like(carry_ref)

    io_e = _iota((n_exp, tm), 0)
    idx = idx_ref[...]
    hits = [io_e == idx[k:k + 1, :] for k in range(TOP_K)]
    sel = jnp.zeros((n_exp, tm), F32)
    for hit in hits:
        sel = jnp.where(hit, 1.0, sel)
    before = jnp.where(_iota((tm, tm), 0) < _iota((tm, tm), 1), 1.0, 0.0).astype(BF16)
    rank = _dot(sel.astype(BF16), before)
    tot = base_ref[...] + carry_ref[...] + rank
    dest = [jnp.sum(jnp.where(hit, tot, 0.0), axis=0, keepdims=True) for hit in hits]
    dest_ref[...] = jnp.concatenate(dest, axis=0).astype(jnp.int32) * PACK_ROWS
    carry_ref[...] += jnp.sum(sel, axis=1, keepdims=True)


def _dest(idx_t, base, tm):
    t = idx_t.shape[1]
    n_exp = base.shape[0]
    tok = pl.BlockSpec((TOP_K, tm), lambda i: (0, i))
    return pl.pallas_call(
        _dest_kernel,
        grid=(t // tm,),
        in_specs=[tok, pl.BlockSpec((n_exp, 1), lambda i: (0, 0))],
        out_specs=tok,
        out_shape=jax.ShapeDtypeStruct((TOP_K, t), jnp.int32),
        scratch_shapes=[pltpu.VMEM((n_exp, 1), F32)],
        compiler_params=_params(("arbitrary",)),
        name="dest",
    )(idx_t, base)


def _dispatch_kernel(dest_ref, h_ref, xs_ref, zero_ref, sem, zsem):
    tm = h_ref.shape[0] // PACK_ROWS
    last = pl.num_programs(0) - 1
    pad_row = xs_ref.shape[0] - zero_ref.shape[0]

    @pl.when(pl.program_id(0) == last)
    def _():
        zero_ref[...] = jnp.zeros_like(zero_ref)
        _tile_copy(zero_ref, 0, xs_ref, pad_row, zsem, EXPERT_CHUNK, PACK_ROWS).start()

    def start(t, c):
        for k in range(TOP_K):
            _tile_copy(h_ref, t * PACK_ROWS, xs_ref, dest_ref[t * TOP_K + k], sem, 1, PACK_ROWS).start(priority=k % 2)
        return c

    lax.fori_loop(0, tm, start, 0)

    for k in range(TOP_K):
        _tile_copy(h_ref, 0, xs_ref, 0, sem, tm, PACK_ROWS).wait()

    @pl.when(pl.program_id(0) == last)
    def _():
        _tile_copy(zero_ref, 0, xs_ref, pad_row, zsem, EXPERT_CHUNK, PACK_ROWS).wait()


def _dispatch(dest_t, h_pk, tm):
    t = dest_t.shape[0] // TOP_K
    n_rows = (t * TOP_K + EXPERT_CHUNK) * PACK_ROWS
    return pl.pallas_call(
        _dispatch_kernel,
        grid=(t // tm,),
        in_specs=[pl.BlockSpec((tm * TOP_K,), lambda i: (i,), memory_space=pltpu.SMEM),
                  pl.BlockSpec((tm * PACK_ROWS, LANES), lambda i: (i, 0))],
        out_specs=pl.BlockSpec(memory_space=pl.ANY),
        out_shape=jax.ShapeDtypeStruct((n_rows, LANES), jnp.uint32),
        scratch_shapes=[pltpu.VMEM((EXPERT_CHUNK * PACK_ROWS, LANES), jnp.uint32),
                        pltpu.SemaphoreType.DMA(()), pltpu.SemaphoreType.DMA(())],
        compiler_params=_params(("arbitrary",)),
        name="dispatch",
    )(dest_t, h_pk)


def _chunk_pieces(n):
    pieces = [(j < n // EXPERT_PIECE, j * EXPERT_PIECE, EXPERT_PIECE) for j in range(EXPERT_CHUNK // EXPERT_PIECE)]
    size = EXPERT_PIECE // 2
    while size >= 1:
        shift = int(math.log2(size)) + 1
        pieces.append(((n & size) != 0, (n >> shift) << shift, size))
        size //= 2
    return pieces


ST_CHUNKS = 0
ST_ISSUED = 1
ST_CUR_E = 2
ST_CUR_I = 3
ST_WRITE = 4
N_XBUF = 4
N_OBUF = 3


def _expert_kernel(start_ref, count_ref, wg_ref, wu_ref, wd_ref, xs_ref, eo_ref,
                   xbuf, obuf, wgb, wub, wdb, st_ref, rsem, wsem):
    e = pl.program_id(0)
    n_exp = pl.num_programs(0)
    ch = EXPERT_CHUNK
    s0 = start_ref[e]
    cnt = count_ref[e]
    n_ch = (cnt + ch - 1) // ch

    def read_parts(first_slot, buf):
        return [_tile_copy(xs_ref, (first_slot + j * EXPERT_PIECE) * PACK_ROWS, xbuf.at[buf],
                           j * EXPERT_PIECE * PACK_ROWS, rsem.at[buf], EXPERT_PIECE, PACK_ROWS)
                for j in range(ch // EXPERT_PIECE)]

    def read_next():
        def exhausted(c):
            cnt_c = count_ref[jnp.minimum(c[0], n_exp - 1)]
            return jnp.logical_and(c[0] < n_exp, c[1] * ch >= cnt_c)

        ce, ci = lax.while_loop(exhausted, lambda c: (c[0] + 1, jnp.int32(0)),
                                (st_ref[ST_CUR_E], st_ref[ST_CUR_I]))

        @pl.when(ce < n_exp)
        def _():
            first_slot = start_ref[jnp.minimum(ce, n_exp - 1)] + ci * ch
            buf = lax.rem(st_ref[ST_ISSUED], N_XBUF)
            for j, cp in enumerate(read_parts(first_slot, buf)):
                cp.start(priority=j % 2)
            st_ref[ST_ISSUED] = st_ref[ST_ISSUED] + 1

        st_ref[ST_CUR_E] = ce
        st_ref[ST_CUR_I] = ci + 1

    def wait_read(buf):
        for cp in read_parts(0, buf):
            cp.wait()

    def out_pieces(buf, first_slot, n, wait):
        for j, (pred, off, size) in enumerate(_chunk_pieces(n)):
            @pl.when(pred)
            def _(j=j, off=off, size=size):
                cp = _tile_copy(obuf.at[buf], off * PACK_ROWS, eo_ref, (first_slot + off) * PACK_ROWS,
                                wsem.at[buf], size, PACK_ROWS)
                cp.wait() if wait else cp.start(priority=j % 2)

    def drain(buf):
        out_pieces(buf, st_ref[ST_WRITE + 2 * buf], st_ref[ST_WRITE + 2 * buf + 1], True)
        st_ref[ST_WRITE + 2 * buf + 1] = 0

    @pl.when(e == 0)
    def _():
        for i in range(st_ref.shape[0]):
            st_ref[i] = 0
        for _ in range(N_XBUF - 1):
            read_next()

    @pl.when(n_ch > 0)
    def _():
        wgb[...] = wg_ref[0].astype(BF16)
        wub[...] = wu_ref[0].astype(BF16)
        wdb[...] = wd_ref[0].astype(BF16)

    def body(i, c):
        g = st_ref[ST_CHUNKS]
        buf = lax.rem(g, N_XBUF)
        wait_read(buf)
        read_next()
        obf = lax.rem(g, N_OBUF)
        drain(obf)
        first_slot = s0 + i * ch
        n = jnp.minimum(ch, cnt - i * ch)

        def swiglu(rows):
            xb = _load_packed(xbuf, rows, lead=(buf,), dtype=BF16)
            hid = _silu(_dot(xb, wgb[...])) * _dot(xb, wub[...])
            _store_packed(obuf, _dot(hid.astype(BF16), wdb[...]), lead=(obf,))

        pl.when(n > ch // 2)(lambda: swiglu(ch))
        pl.when(n <= ch // 2)(lambda: swiglu(ch // 2))
        out_pieces(obf, first_slot, n, False)
        st_ref[ST_WRITE + 2 * obf] = first_slot
        st_ref[ST_WRITE + 2 * obf + 1] = n
        st_ref[ST_CHUNKS] = g + 1
        return c

    lax.fori_loop(0, n_ch, body, 0)

    @pl.when(e == n_exp - 1)
    def _():
        for b in range(N_OBUF):
            drain(b)


def _experts(starts, counts, xs, wg, wu, wd):
    n_exp, d, d_e = wg.shape
    n_slot = xs.shape[0] // PACK_ROWS - EXPERT_CHUNK
    rows = EXPERT_CHUNK * PACK_ROWS
    grid_spec = pltpu.PrefetchScalarGridSpec(
        num_scalar_prefetch=2,
        grid=(n_exp,),
        in_specs=[pl.BlockSpec((1, d, d_e), lambda e, st, ct: (e, 0, 0)),
                  pl.BlockSpec((1, d, d_e), lambda e, st, ct: (e, 0, 0)),
                  pl.BlockSpec((1, d_e, d), lambda e, st, ct: (e, 0, 0)),
                  pl.BlockSpec(memory_space=pl.ANY)],
        out_specs=pl.BlockSpec(memory_space=pl.ANY),
        scratch_shapes=[pltpu.VMEM((N_XBUF, rows, LANES), jnp.uint32), pltpu.VMEM((N_OBUF, rows, LANES), jnp.uint32),
                        pltpu.VMEM((d, d_e), BF16), pltpu.VMEM((d, d_e), BF16), pltpu.VMEM((d_e, d), BF16),
                        pltpu.SMEM((ST_WRITE + 2 * N_OBUF,), jnp.int32),
                        pltpu.SemaphoreType.DMA((N_XBUF,)), pltpu.SemaphoreType.DMA((N_OBUF,))],
    )
    return pl.pallas_call(
        _expert_kernel,
        grid_spec=grid_spec,
        out_shape=jax.ShapeDtypeStruct((n_slot * PACK_ROWS, LANES), jnp.uint32),
        compiler_params=_params(("arbitrary",)),
        name="experts",
    )(starts, counts, wg, wu, wd, xs)


def _combine_kernel(dest_ref, dnext_ref, wts_ref, h_ref, eo_ref, wgs_ref, wus_ref, wds_ref, g_ref, b_ref,
                    out_ref, gbuf, sem):
    i = pl.program_id(0)
    tm = out_ref.shape[0]
    buf = lax.rem(i, 2)

    def gather(d_ref, bf):
        def start(t, c):
            for k in range(TOP_K):
                _tile_copy(eo_ref, d_ref[t * TOP_K + k], gbuf.at[bf, k], t * PACK_ROWS, sem.at[bf],
                           1, PACK_ROWS).start(priority=k % 2)
            return c

        lax.fori_loop(0, tm, start, 0)

    @pl.when(i == 0)
    def _():
        gather(dest_ref, 0)

    def step(bf):
        for t in range(tm):
            for k in range(TOP_K):
                _tile_copy(eo_ref, dnext_ref[t * TOP_K + k], gbuf.at[1 - bf, k], t * PACK_ROWS, sem.at[1 - bf],
                           1, PACK_ROWS).start(priority=k % 2)
        hh = _load_token_tiles(h_ref, tm)
        hb = hh.astype(BF16)
        hid = _silu(_dot(hb, wgs_ref[...])) * _dot(hb, wus_ref[...])
        acc = DN_ALPHA * hh + _dot(hid.astype(BF16), wds_ref[...])
        w_t = jnp.concatenate([wts_ref[...], jnp.zeros((LANES - TOP_K, tm), F32)], axis=0).T
        for k in range(TOP_K):
            _tile_copy(eo_ref, 0, gbuf.at[bf, k], 0, sem.at[bf], tm, PACK_ROWS).wait()
        for k in range(TOP_K):
            acc = acc + _load_packed(gbuf, tm, lead=(bf, k)) * w_t[:, k:k + 1]
        out_ref[...] = _layer_norm(acc, g_ref[...], b_ref[...])

        @pl.when(i == pl.num_programs(0) - 1)
        def _():
            for k in range(TOP_K):
                _tile_copy(eo_ref, 0, gbuf.at[1 - bf, k], 0, sem.at[1 - bf], tm, PACK_ROWS).wait()

    for bf in range(2):
        pl.when(buf == bf)(lambda bf=bf: step(bf))


def _combine(dest_t, wts_t, h_tt, eo, wgs, wus, wds, g2, b2, tm):
    t = dest_t.shape[0] // TOP_K
    d = wgs.shape[0]
    const = lambda shape: pl.BlockSpec(shape, lambda i: (0, 0))
    n_tiles = t // tm
    return pl.pallas_call(
        _combine_kernel,
        grid=(n_tiles,),
        in_specs=[pl.BlockSpec((tm * TOP_K,), lambda i: (i,), memory_space=pltpu.SMEM),
                  pl.BlockSpec((tm * TOP_K,), lambda i: (jnp.minimum(i + 1, n_tiles - 1),), memory_space=pltpu.SMEM),
                  pl.BlockSpec((TOP_K, tm), lambda i: (0, i)),
                  pl.BlockSpec((tm * SUBLANES, LANES), lambda i: (i, 0)),
                  pl.BlockSpec(memory_space=pl.ANY),
                  const(wgs.shape), const(wus.shape), const(wds.shape), const((1, d)), const((1, d))],
        out_specs=pl.BlockSpec((tm, d), lambda i: (i, 0)),
        out_shape=jax.ShapeDtypeStruct((t, d), F32),
        scratch_shapes=[pltpu.VMEM((2, TOP_K, tm * PACK_ROWS, LANES), jnp.uint32), pltpu.SemaphoreType.DMA((2,))],
        compiler_params=_params(("arbitrary",)),
        name="combine",
    )(dest_t, dest_t, wts_t, h_tt, eo, wgs, wus, wds, g2, b2)


def _regroup_w_in(w):
    d = w.shape[0]
    splits = np.cumsum([GDN_QK, GDN_QK, GDN_VW, GDN_VW, GDN_HEADS, GDN_HEADS, RET_QK, RET_QK, RET_VW])
    qa, ka, va, za, ba, aa, qb, kb, vb, gb = jnp.split(w, splits.tolist(), axis=1)
    ba_blk = jnp.concatenate([ba, aa, jnp.zeros((d, LANES - 2 * GDN_HEADS), w.dtype)], axis=1)
    qb = qb.reshape(d, RET_HEADS, RET_DK // 2, 2)
    kb = kb.reshape(d, RET_HEADS, RET_DK // 2, 2)
    qkb = jnp.concatenate([qb[..., 0], qb[..., 1], kb[..., 0], kb[..., 1]], axis=-1).reshape(d, RET_HEADS * LANES)
    return jnp.concatenate([qa, ka, va, za, qkb, vb, gb, ba_blk], axis=1).astype(BF16)


def _rotary_tables(seq):
    inv = 1.0 / (ROPE_BASE ** jnp.linspace(0.0, 1.0, RET_DK // 2, dtype=F32))
    ang = jnp.arange(seq, dtype=F32)[:, None] * inv[None, :]
    c, s = jnp.cos(ang), jnp.sin(ang)
    ks = RET_DK ** -0.5
    rot_c = jnp.concatenate([c, c, c * ks, c * ks], axis=1)
    rot_s = jnp.concatenate([-s, s, -s * ks, s * ks], axis=1)
    return rot_c, rot_s


def _tile(n, pref):
    return pref if n % pref == 0 else n


def kernel(x, w_in, gdn_conv_w, gdn_a_log, gdn_dt_bias, gdn_norm_w, w_out, ln1_g, ln1_b, w_router, router_bias,
           w_gate_e, w_up_e, w_down_e, w_gate_s, w_up_s, w_down_s, ln2_g, ln2_b):
    batch, seq, d = x.shape
    t = batch * seq
    n_exp = w_router.shape[-1]
    hcur = x.reshape(t, d)
    for l in range(DEPTH):
        w_bf = _regroup_w_in(w_in[l])
        log_gamma = jnp.log(1.0 - 2.0 ** (-5.0 - jnp.arange(RET_HEADS, dtype=F32)))
        hp = jnp.stack([-jnp.exp(gdn_a_log[l].astype(F32)), gdn_dt_bias[l].astype(F32), log_gamma])
        rot_c, rot_s = _rotary_tables(seq)
        oa, ob = _mixer(hcur, w_bf, hp,gdn_conv_w[l].astype(F32), gdn_norm_w[l].reshape(1, -1).astype(F32),
                        rot_c, rot_s, batch, seq)
        hh, h_pk, idx_t, wts_t, counts = _post_mix(
            hcur, oa, ob, w_out[l].astype(BF16), ln1_g[l].reshape(1, d), ln1_b[l].reshape(1, d),
            w_router[l].T.astype(BF16), router_bias[l].reshape(n_exp, 1).astype(F32), _tile(t, 256))
        base = jnp.cumsum(counts, axis=0) - counts
        dest_t = _dest(idx_t, base, _tile(t, 512)).T.reshape(t * TOP_K)
        xs = _dispatch(dest_t, h_pk, _tile(t, 512))
        eo = _experts(base.reshape(n_exp).astype(jnp.int32), counts.reshape(n_exp).astype(jnp.int32),
                      xs, w_gate_e[l], w_up_e[l], w_down_e[l])
        hcur = _combine(dest_t, wts_t, hh, eo, w_gate_s[l].astype(BF16), w_up_s[l].astype(BF16),
                        w_down_s[l].astype(BF16), ln2_g[l].reshape(1, d), ln2_b[l].reshape(1, d), _tile(t, 128))
    return hcur.reshape(batch, seq, d)
```

```python
import math

import numpy as np
import jax
import jax.numpy as jnp
from jax import lax
from jax.experimental import pallas as pl
from jax.experimental.pallas import tpu as pltpu

GDN_HEADS = 4
GDN_DK = 128
GDN_DV = 128
GDN_CONV = 4
GDN_CHUNK = 64
RET_HEADS = 4
RET_DK = 64
RET_DV = 128
ROPE_BASE = 10000.0
N_GROUPS = 8
TOPK_GROUPS = 4
TOP_K = 8
ROUTED_SCALE = 2.5
DEPTH = 1
DN_ALPHA = (2.0 * DEPTH) ** 0.25
LN_EPS = 1e-5
NORM_EPS = 1e-6

GDN_QK = GDN_HEADS * GDN_DK
GDN_VW = GDN_HEADS * GDN_DV
RET_QK = RET_HEADS * RET_DK
RET_VW = RET_HEADS * RET_DV

LANES = 128
SUBLANES = 8
VMEM_LIMIT_BYTES = 56 * 1024 * 1024

SUPER = 256
NEG_INF = float("-inf")

COL_QA = 0
COL_KA = COL_QA + GDN_HEADS
COL_VA = COL_KA + GDN_HEADS
COL_ZA = COL_VA + GDN_HEADS
COL_QKB = COL_ZA + GDN_HEADS
COL_VB = COL_QKB + RET_HEADS
COL_GB = COL_VB + RET_HEADS
COL_BA = COL_GB + RET_HEADS
N_COLBLK = COL_BA + 1
PROJ_W = N_COLBLK * LANES
HEAD_W = GDN_HEADS * LANES
MIX_TS = 512
POST_MIX_TM = 512
DEST_TM = 512
DISPATCH_TM = 1024
COMBINE_TM = 128
EXPERT_CHUNK = 512
EXPERT_PIECE = 64
PACK_ROWS = 4

BF16 = jnp.bfloat16
F32 = jnp.float32


def _dot(a, b):
    return jnp.dot(a, b, preferred_element_type=F32)


def _dot_nt(a, b):
    return lax.dot_general(a, b, (((1,), (1,)), ((), ())), preferred_element_type=F32)


def _silu(x):
    return x * jax.nn.sigmoid(x)


def _softplus(x):
    return jnp.maximum(x, 0.0) + jnp.log1p(jnp.exp(-jnp.abs(x)))


def _iota(shape, dim, dtype=jnp.int32):
    return lax.broadcasted_iota(dtype, shape, dim)


def _params(sem):
    return pltpu.CompilerParams(dimension_semantics=sem, vmem_limit_bytes=VMEM_LIMIT_BYTES)


def _load_token_tiles(ref, n, lead=()):
    parts = [ref[lead + (pl.ds(s, n, stride=SUBLANES), slice(None))] for s in range(SUBLANES)]
    return jnp.concatenate(parts, axis=1)


def _store_token_tiles(ref, val, lead=()):
    n = val.shape[0]
    for s in range(SUBLANES):
        ref[lead + (pl.ds(s, n, stride=SUBLANES), slice(None))] = val[:, s * LANES:(s + 1) * LANES]


def _tile_copy(src_ref, src_row, dst_ref, dst_row, sem, n_slots=1, slot_rows=SUBLANES):
    rows = n_slots * slot_rows
    aligned = lambda r: r if isinstance(r, int) else pl.multiple_of(r, slot_rows)
    return pltpu.make_async_copy(src_ref.at[pl.ds(aligned(src_row), rows)],
                                 dst_ref.at[pl.ds(aligned(dst_row), rows)], sem)


def _store_packed(ref, val, lead=()):
    n, width = val.shape
    bits = pltpu.bitcast(val.astype(BF16).astype(F32), jnp.uint32)
    words = (bits[:, width // 2:] & jnp.uint32(0xFFFF0000)) | (bits[:, :width // 2] >> 16)
    for s in range(PACK_ROWS):
        ref[lead + (pl.ds(s, n, stride=PACK_ROWS), slice(None))] = words[:, s * LANES:(s + 1) * LANES]


def _load_packed(ref, n, lead=(), dtype=F32):
    words = [ref[lead + (pl.ds(s, n, stride=PACK_ROWS), slice(None))] for s in range(PACK_ROWS)]
    lo = [pltpu.bitcast(w << 16, F32) for w in words]
    hi = [pltpu.bitcast(w & jnp.uint32(0xFFFF0000), F32) for w in words]
    return jnp.concatenate(lo + hi, axis=1).astype(dtype)


def _mixer_kernel(hp_ref,
                  x_ref, xn_ref,
                  w_ref, cw_ref, nw_ref, rc_ref, rs_ref,
                  oa_ref, ob_ref, proj_ref, st_ref, st2_ref):
    j = pl.program_id(1)
    ts = x_ref.shape[0]
    n_super = ts // SUPER
    n_chunk = SUPER // GDN_CHUNK

    def project(rows, slot):
        xb = rows.astype(BF16)
        step = 4 * LANES
        for c in range(0, PROJ_W, step):
            n = min(step, PROJ_W - c)
            proj_ref[slot, :, c:c + n] = _dot(xb, w_ref[:, c:c + n])

    @pl.when(j == 0)
    def _():
        st_ref[...] = jnp.zeros_like(st_ref)
        st2_ref[...] = jnp.zeros_like(st2_ref)

    @pl.when((j == 0) & (pl.program_id(0) == 0))
    def _():
        proj_ref[1] = jnp.zeros(proj_ref.shape[1:], F32)
        project(x_ref[0:SUPER, :], 0)

    ri = _iota((SUPER, SUPER), 0)
    ci = _iota((SUPER, SUPER), 1)
    shift = int(math.log2(GDN_CHUNK))
    same = (ri >> shift) == (ci >> shift)
    causal = same & (ci <= ri)
    strict = same & (ci < ri)
    causal_t = same & (ri <= ci)
    eye = jnp.where(ri == ci, 1.0, 0.0)
    lane = _iota((SUPER, LANES), 1)
    dpos = jnp.maximum((ri - ci).astype(F32), 0.0)
    lower = ri >= ci
    pos = _iota((SUPER, 1), 0).astype(F32)
    half = lane < RET_DK
    pair_lo = (lane % RET_DK) < (RET_DK // 2)
    lane1 = _iota((1, LANES), 1)
    neg_a_lane = jnp.zeros((1, LANES), F32)
    dt_b_lane = jnp.zeros((1, LANES), F32)
    for h in range(GDN_HEADS):
        neg_a_lane = jnp.where(lane1 == h + GDN_HEADS, hp_ref[0, h], neg_a_lane)
        dt_b_lane = jnp.where(lane1 == h + GDN_HEADS, hp_ref[1, h], dt_b_lane)

    def pcols(col_blk, h):
        return slice((col_blk + h) * LANES, (col_blk + h + 1) * LANES)

    def conv_silu(col_blk, h, slot, s, tail):
        cur = proj_ref[slot, :, pcols(col_blk, h)]
        prev = jnp.where((s == 0) & (j == 0), 0.0, tail[:, pcols(col_blk, h)])
        ext = jnp.concatenate([prev, cur], axis=0)
        acc = jnp.zeros((SUPER, LANES), F32)
        for jj in range(GDN_CONV):
            off = SUBLANES - (GDN_CONV - 1) + jj
            acc = acc + cw_ref[jj:jj + 1, pcols(col_blk, h)] * ext[off:off + SUPER, :]
        return _silu(acc)

    def l2n(u):
        return u * lax.rsqrt(jnp.sum(u * u, axis=-1, keepdims=True) + NORM_EPS)

    heads = range(GDN_HEADS)
    hcols = [slice(h * LANES, (h + 1) * LANES) for h in heads]

    def gdn_heads(r0, s, slot, tail):
        q = [l2n(conv_silu(COL_QA, h, slot, s, tail)) * (GDN_DK ** -0.5) for h in heads]
        k = [l2n(conv_silu(COL_KA, h, slot, s, tail)) for h in heads]
        v = [conv_silu(COL_VA, h, slot, s, tail) for h in heads]
        ba = proj_ref[slot, :, pcols(COL_BA, 0)]
        gates = jnp.where(lane < GDN_HEADS, jax.nn.sigmoid(ba), neg_a_lane * _softplus(ba + dt_b_lane))
        gates_t = gates.T
        beta, g_col, gl_col, decay_c, decay_s = [], [], [], [], []
        for h in heads:
            beta.append(jnp.sum(jnp.where(lane == h, gates, 0.0), axis=1, keepdims=True))
            la_col = jnp.sum(jnp.where(lane == h + GDN_HEADS, gates, 0.0), axis=1, keepdims=True)
            la_row = gates_t[h + GDN_HEADS:h + GDN_HEADS + 1, :]
            gc = jnp.sum(jnp.where(causal, la_row, 0.0), axis=1, keepdims=True)
            gr = jnp.sum(jnp.where(causal_t, la_col, 0.0), axis=0, keepdims=True)
            g_col.append(gc)
            gl_col.append(jnp.sum(jnp.where(same, la_row, 0.0), axis=1, keepdims=True))
            dc = jnp.exp(jnp.where(causal, gc - gr, NEG_INF))
            decay_c.append(dc)
            decay_s.append(jnp.where(strict, dc, 0.0))
        kb = [k[h] * beta[h] for h in heads]
        k_bf = [k[h].astype(BF16) for h in heads]
        low = [_dot_nt(kb[h].astype(BF16), k_bf[h]) * decay_s[h] for h in heads]
        p = [eye - low[h] for h in heads]
        sq = low
        for _ in range(shift - 1):
            sq_bf = [sq[h].astype(BF16) for h in heads]
            sq = [_dot(sq_bf[h], sq_bf[h]) for h in heads]
            p = [p[h] + _dot(p[h].astype(BF16), sq[h].astype(BF16)) for h in heads]
        eg = [jnp.exp(g_col[h]) for h in heads]
        rhs = [jnp.concatenate([v[h] * beta[h], kb[h] * eg[h]], axis=1).astype(BF16) for h in heads]
        wk = [_dot(p[h].astype(BF16), rhs[h]) for h in heads]
        w_val = [wk[h][:, :GDN_DV] for h in heads]
        k_cum = [wk[h][:, GDN_DV:].astype(BF16) for h in heads]
        attn = [(_dot_nt(q[h].astype(BF16), k_bf[h]) * decay_c[h]).astype(BF16) for h in heads]
        q_dec = [(q[h] * eg[h]).astype(BF16) for h in heads]
        kd_t = [(k[h] * jnp.exp(gl_col[h] - g_col[h])).T.astype(BF16) for h in heads]
        gl = [jnp.exp(gl_col[h]) for h in heads]
        st = [st_ref[h] for h in heads]
        outs = [[] for _ in heads]
        for c in range(n_chunk):
            lo = c * GDN_CHUNK
            hi = lo + GDN_CHUNK
            st_bf = [st[h].astype(BF16) for h in heads]
            v_new = [w_val[h][lo:hi] - _dot(k_cum[h][lo:hi], st_bf[h]) for h in heads]
            v_pad = []
            for h in heads:
                pieces = [jnp.zeros((GDN_CHUNK, GDN_DV), F32)] * n_chunk
                pieces[c] = v_new[h]
                v_pad.append(jnp.concatenate(pieces, axis=0).astype(BF16))
            for h in heads:
                outs[h].append(_dot(q_dec[h][lo:hi], st_bf[h]) + _dot(attn[h][lo:hi], v_pad[h]))
            st = [st[h] * gl[h][lo:lo + 1, :] + _dot(kd_t[h], v_pad[h]) for h in heads]
        for h in heads:
            st_ref[h] = st[h]
            o = jnp.concatenate(outs[h], axis=0)
            o = o * lax.rsqrt(jnp.mean(o * o, axis=-1, keepdims=True) + NORM_EPS) * nw_ref[...]
            z = proj_ref[slot, :, pcols(COL_ZA, h)]
            oa_ref[pl.ds(r0, SUPER), hcols[h]] = (o * _silu(z)).astype(oa_ref.dtype)

    def ret_heads(r0, slot):
        rc = rc_ref[pl.ds(r0, SUPER), :]
        rs = rs_ref[pl.ds(r0, SUPER), :]
        lg = [hp_ref[2, h] for h in heads]
        qm, km, vb = [], [], []
        for h in heads:
            x = proj_ref[slot, :, pcols(COL_QKB, h)]
            swapped = jnp.where(pair_lo, pltpu.roll(x, LANES - RET_DK // 2, 1), pltpu.roll(x, RET_DK // 2, 1))
            xr = x * rc + swapped * rs
            qm.append(jnp.where(half, xr, 0.0))
            km.append(jnp.where(half, pltpu.roll(xr, LANES - RET_DK, 1), 0.0))
            vb.append(proj_ref[slot, :, pcols(COL_VB, h)].astype(BF16))
        inner = [(_dot_nt(qm[h].astype(BF16), km[h].astype(BF16))
                  * jnp.where(lower, jnp.exp(lg[h] * dpos), 0.0)).astype(BF16) for h in heads]
        st2 = [st2_ref[h] for h in heads]
        ob = [_dot(inner[h], vb[h])
              + _dot((qm[h] * jnp.exp(lg[h] * (pos + 1.0))).astype(BF16), st2[h].astype(BF16)) for h in heads]
        for h in heads:
            g_chunk = jnp.exp(jnp.full((1, 1), SUPER, F32) * lg[h])
            st2_ref[h] = st2[h] * g_chunk + _dot((km[h] * jnp.exp(lg[h] * (SUPER - 1.0 - pos))).T.astype(BF16), vb[h])
        for h in heads:
            mu = jnp.mean(ob[h], axis=-1, keepdims=True)
            oc = ob[h] - mu
            oc = oc * lax.rsqrt(jnp.mean(oc * oc, axis=-1, keepdims=True) + NORM_EPS)
            gate = proj_ref[slot, :, pcols(COL_GB, h)]
            ob_ref[pl.ds(r0, SUPER), hcols[h]] = (oc * _silu(gate)).astype(ob_ref.dtype)

    for s in range(n_super):
        r0 = s * SUPER
        slot = s % 2
        tail = proj_ref[1 - slot, SUPER - SUBLANES:SUPER, 0:COL_ZA * LANES]
        nxt = x_ref[r0 + SUPER:r0 + 2 * SUPER, :] if s + 1 < n_super else xn_ref[...]
        project(nxt, 1 - slot)
        gdn_heads(r0, s, slot, tail)
        ret_heads(r0, slot)


def _mixer(x2, w_bf, hp, conv_w, norm_w, rot_c, rot_s, batch, seq):
    t, d = x2.shape
    ts = MIX_TS if seq % MIX_TS == 0 else seq
    nj = seq // ts
    per_blk = ts // SUPER
    assert per_blk % 2 == 0
    last_chunk = t // SUPER - 1
    const2 = lambda shape: pl.BlockSpec(shape, lambda b, j: (0, 0))
    tab = pl.BlockSpec((ts, LANES), lambda b, j: (j, 0))
    in_specs = [
        pl.BlockSpec(memory_space=pltpu.SMEM),
        pl.BlockSpec((ts, d), lambda b, j: (b * nj + j, 0)),
        pl.BlockSpec((SUPER, d), lambda b, j: (jnp.minimum((b * nj + j + 1) * per_blk, last_chunk), 0)),
        const2(w_bf.shape), const2(conv_w.shape), const2((1, GDN_DV)), tab, tab,
    ]
    out_spec = pl.BlockSpec((ts, HEAD_W), lambda b, j: (b * nj + j, 0))
    return pl.pallas_call(
        _mixer_kernel,
        grid=(batch, nj),
        in_specs=in_specs,
        out_specs=[out_spec, out_spec],
        out_shape=[jax.ShapeDtypeStruct((t, GDN_VW), BF16), jax.ShapeDtypeStruct((t, RET_VW), BF16)],
        scratch_shapes=[pltpu.VMEM((2, SUPER, PROJ_W), F32),
                        pltpu.VMEM((GDN_HEADS, GDN_DK, GDN_DV), F32),
                        pltpu.VMEM((RET_HEADS, LANES, RET_DV), F32)],
        compiler_params=_params(("arbitrary", "arbitrary")),
        name="mixer",
    )(hp, x2, x2, w_bf, conv_w, norm_w, rot_c, rot_s)


def _layer_norm(u, g, b):
    mu = jnp.mean(u, axis=-1, keepdims=True)
    uc = u - mu
    var = jnp.mean(uc * uc, axis=-1, keepdims=True)
    return uc * lax.rsqrt(var + LN_EPS) * g + b


def _post_mix_kernel(x_ref, oa_ref, ob_ref, wo_ref, g_ref, b_ref, wr_ref, rb_ref,
                     h_ref, hp_ref, idx_ref, wts_ref, cnt_ref):
    n_exp = wr_ref.shape[0]
    tm = x_ref.shape[0]
    per_grp = n_exp // N_GROUPS
    mix = _dot(oa_ref[...], wo_ref[:GDN_VW, :]) + _dot(ob_ref[...], wo_ref[GDN_VW:, :])
    hh = _layer_norm(DN_ALPHA * x_ref[...] + mix, g_ref[...], b_ref[...])
    _store_token_tiles(h_ref, hh)
    _store_packed(hp_ref, hh)
    scores = jax.nn.sigmoid(_dot_nt(wr_ref[...], hh.astype(BF16)))
    choice = scores + rb_ref[...]
    big = float(n_exp)
    io_g = _iota((per_grp, tm), 0).astype(F32)
    grp = []
    for g in range(N_GROUPS):
        blk = choice[g * per_grp:(g + 1) * per_grp, :]
        m1 = jnp.max(blk, axis=0, keepdims=True)
        i1 = jnp.min(jnp.where(blk == m1, io_g, big), axis=0, keepdims=True)
        m2 = jnp.max(jnp.where(io_g == i1, NEG_INF, blk), axis=0, keepdims=True)
        grp.append(m1 + m2)
    gsc = jnp.concatenate(grp, axis=0)
    io8 = _iota((N_GROUPS, tm), 0).astype(F32)
    gsel = jnp.zeros((N_GROUPS, tm), F32)
    for _ in range(TOPK_GROUPS):
        m = jnp.max(gsc, axis=0, keepdims=True)
        i = jnp.min(jnp.where(gsc == m, io8, big), axis=0, keepdims=True)
        hit = io8 == i
        gsel = jnp.where(hit, 1.0, gsel)
        gsc = jnp.where(hit, NEG_INF, gsc)
    masked = jnp.concatenate(
        [jnp.where(gsel[g:g + 1, :] > 0.0, choice[g * per_grp:(g + 1) * per_grp, :], NEG_INF)
         for g in range(N_GROUPS)], axis=0)
    io_e = _iota((n_exp, tm), 0).astype(F32)
    allowed = masked
    ids, ws = [], []
    for _ in range(TOP_K):
        m = jnp.max(masked, axis=0, keepdims=True)
        i = jnp.min(jnp.where(masked == m, io_e, big), axis=0, keepdims=True)
        hit = io_e == i
        ws.append(jnp.sum(jnp.where(hit, scores, 0.0), axis=0, keepdims=True))
        ids.append(i)
        masked = jnp.where(hit, NEG_INF, masked)
    sel = jnp.where(masked == NEG_INF, jnp.where(allowed == NEG_INF, 0.0, 1.0), 0.0)
    w = jnp.concatenate(ws, axis=0)
    wts_ref[...] = w / jnp.sum(w, axis=0, keepdims=True) * ROUTED_SCALE
    idx_ref[...] = jnp.concatenate(ids, axis=0).astype(jnp.int32)

    @pl.when(pl.program_id(0) == 0)
    def _():
        cnt_ref[...] = jnp.zeros_like(cnt_ref)

    cnt_ref[...] += jnp.sum(sel, axis=1, keepdims=True)


def _post_mix(x2, oa, ob, wo_bf, g1, b1, wr_t, rbias, tm):
    t, d = x2.shape
    n_exp = wr_t.shape[0]
    row = lambda w: pl.BlockSpec((tm, w), lambda i: (i, 0))
    const = lambda shape: pl.BlockSpec(shape, lambda i: (0, 0))
    tok = pl.BlockSpec((TOP_K, tm), lambda i: (0, i))
    return pl.pallas_call(
        _post_mix_kernel,
        grid=(t // tm,),
        in_specs=[row(d), row(GDN_VW), row(RET_VW), const(wo_bf.shape), const((1, d)), const((1, d)),
                  const(wr_t.shape), const((n_exp, 1))],
        out_specs=[pl.BlockSpec((tm * SUBLANES, LANES), lambda i: (i, 0)),
                   pl.BlockSpec((tm * PACK_ROWS, LANES), lambda i: (i, 0)), tok, tok, const((n_exp, 1))],
        out_shape=[jax.ShapeDtypeStruct((t * SUBLANES, LANES), F32),
                   jax.ShapeDtypeStruct((t * PACK_ROWS, LANES), jnp.uint32),
                   jax.ShapeDtypeStruct((TOP_K, t), jnp.int32),
                   jax.ShapeDtypeStruct((TOP_K, t), F32),
                   jax.ShapeDtypeStruct((n_exp, 1), F32)],
        compiler_params=_params(("arbitrary",)),
        name="post_mix",
    )(x2, oa, ob, wo_bf, g1, b1, wr_t, rbias)


def _dest_kernel(idx_ref, base_ref, dest_ref, carry_ref):
    n_exp = base_ref.shape[0]
    tm = idx_ref.shape[1]

    @pl.when(pl.program_id(0) == 0)
    def _():
        carry_ref[...] = jnp.zeros_like(carry_ref)

    io_e = _iota((n_exp, tm), 0)
    idx = idx_ref[...]
    hits = [io_e == idx[k:k + 1, :] for k in range(TOP_K)]
    sel = jnp.zeros((n_exp, tm), F32)
    for hit in hits:
        sel = jnp.where(hit, 1.0, sel)
    before = jnp.where(_iota((tm, tm), 0) < _iota((tm, tm), 1), 1.0, 0.0).astype(BF16)
    rank = _dot(sel.astype(BF16), before)
    tot = base_ref[...] + carry_ref[...] + rank
    dest = [jnp.sum(jnp.where(hit, tot, 0.0), axis=0, keepdims=True) for hit in hits]
    dest_ref[...] = jnp.concatenate(dest, axis=0).astype(jnp.int32) * PACK_ROWS
    carry_ref[...] += jnp.sum(sel, axis=1, keepdims=True)


def _dest(idx_t, base, tm):
    t = idx_t.shape[1]
    n_exp = base.shape[0]
    tok = pl.BlockSpec((TOP_K, tm), lambda i: (0, i))
    return pl.pallas_call(
        _dest_kernel,
        grid=(t // tm,),
        in_specs=[tok, pl.BlockSpec((n_exp, 1), lambda i: (0, 0))],
        out_specs=tok,
        out_shape=jax.ShapeDtypeStruct((TOP_K, t), jnp.int32),
        scratch_shapes=[pltpu.VMEM((n_exp, 1), F32)],
        compiler_params=_params(("arbitrary",)),
        name="dest",
    )(idx_t, base)


def _dispatch_kernel(dest_ref, h_ref, xs_ref, zero_ref, sem, zsem):
    tm = h_ref.shape[0] // PACK_ROWS
    last = pl.num_programs(0) - 1
    pad_row = xs_ref.shape[0] - zero_ref.shape[0]

    @pl.when(pl.program_id(0) == last)
    def _():
        zero_ref[...] = jnp.zeros_like(zero_ref)
        _tile_copy(zero_ref, 0, xs_ref, pad_row, zsem, EXPERT_CHUNK, PACK_ROWS).start()

    def start(t, c):
        for k in range(TOP_K):
            _tile_copy(h_ref, t * PACK_ROWS, xs_ref, dest_ref[t * TOP_K + k], sem, 1, PACK_ROWS).start(priority=k % 2)
        return c

    lax.fori_loop(0, tm, start, 0)

    for k in range(TOP_K):
        _tile_copy(h_ref, 0, xs_ref, 0, sem, tm, PACK_ROWS).wait()

    @pl.when(pl.program_id(0) == last)
    def _():
        _tile_copy(zero_ref, 0, xs_ref, pad_row, zsem, EXPERT_CHUNK, PACK_ROWS).wait()


def _dispatch(dest_t, h_pk, tm):
    t = dest_t.shape[0] // TOP_K
    n_rows = (t * TOP_K + EXPERT_CHUNK) * PACK_ROWS
    return pl.pallas_call(
        _dispatch_kernel,
        grid=(t // tm,),
        in_specs=[pl.BlockSpec((tm * TOP_K,), lambda i: (i,), memory_space=pltpu.SMEM),
                  pl.BlockSpec((tm * PACK_ROWS, LANES), lambda i: (i, 0))],
        out_specs=pl.BlockSpec(memory_space=pl.ANY),
        out_shape=jax.ShapeDtypeStruct((n_rows, LANES), jnp.uint32),
        scratch_shapes=[pltpu.VMEM((EXPERT_CHUNK * PACK_ROWS, LANES), jnp.uint32),
                        pltpu.SemaphoreType.DMA(()), pltpu.SemaphoreType.DMA(())],
        compiler_params=_params(("arbitrary",)),
        name="dispatch",
    )(dest_t, h_pk)


def _chunk_pieces(n):
    pieces = [(j < n // EXPERT_PIECE, j * EXPERT_PIECE, EXPERT_PIECE) for j in range(EXPERT_CHUNK // EXPERT_PIECE)]
    size = EXPERT_PIECE // 2
    while size >= 1:
        shift = int(math.log2(size)) + 1
        pieces.append(((n & size) != 0, (n >> shift) << shift, size))
        size //= 2
    return pieces


ST_CHUNKS = 0
ST_ISSUED = 1
ST_CUR_E = 2
ST_CUR_I = 3
ST_WRITE = 4
N_XBUF = 4
N_OBUF = 3


def _expert_kernel(start_ref, count_ref, wg_ref, wu_ref, wd_ref, xs_ref, eo_ref,
                   xbuf, obuf, wgb, wub, wdb, st_ref, rsem, wsem):
    e = pl.program_id(0)
    n_exp = pl.num_programs(0)
    ch = EXPERT_CHUNK
    s0 = start_ref[e]
    cnt = count_ref[e]
    n_ch = (cnt + ch - 1) // ch

    def read_parts(first_slot, buf):
        return [_tile_copy(xs_ref, (first_slot + j * EXPERT_PIECE) * PACK_ROWS, xbuf.at[buf],
                           j * EXPERT_PIECE * PACK_ROWS, rsem.at[buf], EXPERT_PIECE, PACK_ROWS)
                for j in range(ch // EXPERT_PIECE)]

    def read_next():
        def exhausted(c):
            cnt_c = count_ref[jnp.minimum(c[0], n_exp - 1)]
            return jnp.logical_and(c[0] < n_exp, c[1] * ch >= cnt_c)

        ce, ci = lax.while_loop(exhausted, lambda c: (c[0] + 1, jnp.int32(0)),
                                (st_ref[ST_CUR_E], st_ref[ST_CUR_I]))

        @pl.when(ce < n_exp)
        def _():
            first_slot = start_ref[jnp.minimum(ce, n_exp - 1)] + ci * ch
            buf = lax.rem(st_ref[ST_ISSUED], N_XBUF)
            for j, cp in enumerate(read_parts(first_slot, buf)):
                cp.start(priority=j % 2)
            st_ref[ST_ISSUED] = st_ref[ST_ISSUED] + 1

        st_ref[ST_CUR_E] = ce
        st_ref[ST_CUR_I] = ci + 1

    def wait_read(buf):
        for cp in read_parts(0, buf):
            cp.wait()

    def out_pieces(buf, first_slot, n, wait):
        for j, (pred, off, size) in enumerate(_chunk_pieces(n)):
            @pl.when(pred)
            def _(j=j, off=off, size=size):
                cp = _tile_copy(obuf.at[buf], off * PACK_ROWS, eo_ref, (first_slot + off) * PACK_ROWS,
                                wsem.at[buf], size, PACK_ROWS)
                cp.wait() if wait else cp.start(priority=j % 2)

    def drain(buf):
        out_pieces(buf, st_ref[ST_WRITE + 2 * buf], st_ref[ST_WRITE + 2 * buf + 1], True)
        st_ref[ST_WRITE + 2 * buf + 1] = 0

    @pl.when(e == 0)
    def _():
        for i in range(st_ref.shape[0]):
            st_ref[i] = 0
        for _ in range(N_XBUF - 1):
            read_next()

    @pl.when(n_ch > 0)
    def _():
        wgb[...] = wg_ref[0].astype(BF16)
        wub[...] = wu_ref[0].astype(BF16)
        wdb[...] = wd_ref[0].astype(BF16)

    def body(i, c):
        g = st_ref[ST_CHUNKS]
        buf = lax.rem(g, N_XBUF)
        wait_read(buf)
        read_next()
        obf = lax.rem(g, N_OBUF)
        drain(obf)
        first_slot = s0 + i * ch
        n = jnp.minimum(ch, cnt - i * ch)

        def swiglu(rows):
            xb = _load_packed(xbuf, rows, lead=(buf,), dtype=BF16)
            hid = _silu(_dot(xb, wgb[...])) * _dot(xb, wub[...])
            _store_packed(obuf, _dot(hid.astype(BF16), wdb[...]), lead=(obf,))

        pl.when(n > ch // 2)(lambda: swiglu(ch))
        pl.when(n <= ch // 2)(lambda: swiglu(ch // 2))
        out_pieces(obf, first_slot, n, False)
        st_ref[ST_WRITE + 2 * obf] = first_slot
        st_ref[ST_WRITE + 2 * obf + 1] = n
        st_ref[ST_CHUNKS] = g + 1
        return c

    lax.fori_loop(0, n_ch, body, 0)

    @pl.when(e == n_exp - 1)
    def _():
        for b in range(N_OBUF):
            drain(b)


def _experts(starts, counts, xs, wg, wu, wd):
    n_exp, d, d_e = wg.shape
    n_slot = xs.shape[0] // PACK_ROWS - EXPERT_CHUNK
    rows = EXPERT_CHUNK * PACK_ROWS
    grid_spec = pltpu.PrefetchScalarGridSpec(
        num_scalar_prefetch=2,
        grid=(n_exp,),
        in_specs=[pl.BlockSpec((1, d, d_e), lambda e, st, ct: (e, 0, 0)),
                  pl.BlockSpec((1, d, d_e), lambda e, st, ct: (e, 0, 0)),
                  pl.BlockSpec((1, d_e, d), lambda e, st, ct: (e, 0, 0)),
                  pl.BlockSpec(memory_space=pl.ANY)],
        out_specs=pl.BlockSpec(memory_space=pl.ANY),
        scratch_shapes=[pltpu.VMEM((N_XBUF, rows, LANES), jnp.uint32), pltpu.VMEM((N_OBUF, rows, LANES), jnp.uint32),
                        pltpu.VMEM((d, d_e), BF16), pltpu.VMEM((d, d_e), BF16), pltpu.VMEM((d_e, d), BF16),
                        pltpu.SMEM((ST_WRITE + 2 * N_OBUF,), jnp.int32),
                        pltpu.SemaphoreType.DMA((N_XBUF,)), pltpu.SemaphoreType.DMA((N_OBUF,))],
    )
    return pl.pallas_call(
        _expert_kernel,
        grid_spec=grid_spec,
        out_shape=jax.ShapeDtypeStruct((n_slot * PACK_ROWS, LANES), jnp.uint32),
        compiler_params=_params(("arbitrary",)),
        name="experts",
    )(starts, counts, wg, wu, wd, xs)


def _combine_kernel(dest_ref, dnext_ref, wts_ref, h_ref, eo_ref, wgs_ref, wus_ref, wds_ref, g_ref, b_ref,
                    out_ref, gbuf, sem):
    i = pl.program_id(0)
    tm = out_ref.shape[0]
    buf = lax.rem(i, 2)

    def gather(d_ref, bf):
        def start(t, c):
            for k in range(TOP_K):
                _tile_copy(eo_ref, d_ref[t * TOP_K + k], gbuf.at[bf, k], t * PACK_ROWS, sem.at[bf],
                           1, PACK_ROWS).start(priority=k % 2)
            return c

        lax.fori_loop(0, tm, start, 0)

    @pl.when(i == 0)
    def _():
        gather(dest_ref, 0)

    def step(bf):
        for t in range(tm):
            for k in range(TOP_K):
                _tile_copy(eo_ref, dnext_ref[t * TOP_K + k], gbuf.at[1 - bf, k], t * PACK_ROWS, sem.at[1 - bf],
                           1, PACK_ROWS).start(priority=k % 2)
        hh = _load_token_tiles(h_ref, tm)
        hb = hh.astype(BF16)
        hid = _silu(_dot(hb, wgs_ref[...])) * _dot(hb, wus_ref[...])
        acc = DN_ALPHA * hh + _dot(hid.astype(BF16), wds_ref[...])
        w_t = jnp.concatenate([wts_ref[...], jnp.zeros((LANES - TOP_K, tm), F32)], axis=0).T
        for k in range(TOP_K):
            _tile_copy(eo_ref, 0, gbuf.at[bf, k], 0, sem.at[bf], tm, PACK_ROWS).wait()
        for k in range(TOP_K):
            acc = acc + _load_packed(gbuf, tm, lead=(bf, k)) * w_t[:, k:k + 1]
        out_ref[...] = _layer_norm(acc, g_ref[...], b_ref[...])

        @pl.when(i == pl.num_programs(0) - 1)
        def _():
            for k in range(TOP_K):
                _tile_copy(eo_ref, 0, gbuf.at[1 - bf, k], 0, sem.at[1 - bf], tm, PACK_ROWS).wait()

    for bf in range(2):
        pl.when(buf == bf)(lambda bf=bf: step(bf))


def _combine(dest_t, wts_t, h_tt, eo, wgs, wus, wds, g2, b2, tm):
    t = dest_t.shape[0] // TOP_K
    d = wgs.shape[0]
    const = lambda shape: pl.BlockSpec(shape, lambda i: (0, 0))
    n_tiles = t // tm
    return pl.pallas_call(
        _combine_kernel,
        grid=(n_tiles,),
        in_specs=[pl.BlockSpec((tm * TOP_K,), lambda i: (i,), memory_space=pltpu.SMEM),
                  pl.BlockSpec((tm * TOP_K,), lambda i: (jnp.minimum(i + 1, n_tiles - 1),), memory_space=pltpu.SMEM),
                  pl.BlockSpec((TOP_K, tm), lambda i: (0, i)),
                  pl.BlockSpec((tm * SUBLANES, LANES), lambda i: (i, 0)),
                  pl.BlockSpec(memory_space=pl.ANY),
                  const(wgs.shape), const(wus.shape), const(wds.shape), const((1, d)), const((1, d))],
        out_specs=pl.BlockSpec((tm, d), lambda i: (i, 0)),
        out_shape=jax.ShapeDtypeStruct((t, d), F32),
        scratch_shapes=[pltpu.VMEM((2, TOP_K, tm * PACK_ROWS, LANES), jnp.uint32), pltpu.SemaphoreType.DMA((2,))],
        compiler_params=_params(("arbitrary",)),
        name="combine",
    )(dest_t, dest_t, wts_t, h_tt, eo, wgs, wus, wds, g2, b2)


def _regroup_w_in(w):
    d = w.shape[0]
    splits = np.cumsum([GDN_QK, GDN_QK, GDN_VW, GDN_VW, GDN_HEADS, GDN_HEADS, RET_QK, RET_QK, RET_VW])
    qa, ka, va, za, ba, aa, qb, kb, vb, gb = jnp.split(w, splits.tolist(), axis=1)
    ba_blk = jnp.concatenate([ba, aa, jnp.zeros((d, LANES - 2 * GDN_HEADS), w.dtype)], axis=1)
    qb = qb.reshape(d, RET_HEADS, RET_DK // 2, 2)
    kb = kb.reshape(d, RET_HEADS, RET_DK // 2, 2)
    qkb = jnp.concatenate([qb[..., 0], qb[..., 1], kb[..., 0], kb[..., 1]], axis=-1).reshape(d, RET_HEADS * LANES)
    return jnp.concatenate([qa, ka, va, za, qkb, vb, gb, ba_blk], axis=1).astype(BF16)


def _rotary_tables(seq):
    inv = 1.0 / (ROPE_BASE ** jnp.linspace(0.0, 1.0, RET_DK // 2, dtype=F32))
    ang = jnp.arange(seq, dtype=F32)[:, None] * inv[None, :]
    c, s = jnp.cos(ang), jnp.sin(ang)
    ks = RET_DK ** -0.5
    rot_c = jnp.concatenate([c, c, c * ks, c * ks], axis=1)
    rot_s = jnp.concatenate([-s, s, -s * ks, s * ks], axis=1)
    return rot_c, rot_s


def _tile(n, pref):
    return pref if n % pref == 0 else n


def kernel(x, w_in, gdn_conv_w, gdn_a_log, gdn_dt_bias, gdn_norm_w, w_out, ln1_g, ln1_b, w_router, router_bias,
           w_gate_e, w_up_e, w_down_e, w_gate_s, w_up_s, w_down_s, ln2_g, ln2_b):
    batch, seq, d = x.shape
    t = batch * seq
    n_exp = w_router.shape[-1]
    hcur = x.reshape(t, d)
    for l in range(DEPTH):
        w_bf = _regroup_w_in(w_in[l])
        log_gamma = jnp.log(1.0 - 2.0 ** (-5.0 - jnp.arange(RET_HEADS, dtype=F32)))
        hp = jnp.stack([-jnp.exp(gdn_a_log[l].astype(F32)), gdn_dt_bias[l].astype(F32), log_gamma])
        rot_c, rot_s = _rotary_tables(seq)
        oa, ob = _mixer(hcur, w_bf, hp,gdn_conv_w[l].astype(F32), gdn_norm_w[l].reshape(1, -1).astype(F32),
                        rot_c, rot_s, batch, seq)
        hh, h_pk, idx_t, wts_t, counts = _post_mix(
            hcur, oa, ob, w_out[l].astype(BF16), ln1_g[l].reshape(1, d), ln1_b[l].reshape(1, d),
            w_router[l].T.astype(BF16), router_bias[l].reshape(n_exp, 1).astype(F32), _tile(t, POST_MIX_TM))
        base = jnp.cumsum(counts, axis=0) - counts
        dest_t = _dest(idx_t, base, _tile(t, DEST_TM)).T.reshape(t * TOP_K)
        xs = _dispatch(dest_t, h_pk, _tile(t, DISPATCH_TM))
        eo = _experts(base.reshape(n_exp).astype(jnp.int32), counts.reshape(n_exp).astype(jnp.int32),
                      xs, w_gate_e[l], w_up_e[l], w_down_e[l])
        hcur = _combine(dest_t, wts_t, hh, eo, w_gate_s[l].astype(BF16), w_up_s[l].astype(BF16),
                        w_down_s[l].astype(BF16), ln2_g[l].reshape(1, d), ln2_b[l].reshape(1, d), _tile(t, COMBINE_TM))
    return hcur.reshape(batch, seq, d)
```

```python
import math

import numpy as np
import jax
import jax.numpy as jnp
from jax import lax
from jax.experimental import pallas as pl
from jax.experimental.pallas import tpu as pltpu

GDN_HEADS = 4
GDN_DK = 128
GDN_DV = 128
GDN_CONV = 4
GDN_CHUNK = 64
RET_HEADS = 4
RET_DK = 64
RET_DV = 128
ROPE_BASE = 10000.0
N_GROUPS = 8
TOPK_GROUPS = 4
TOP_K = 8
ROUTED_SCALE = 2.5
DEPTH = 1
DN_ALPHA = (2.0 * DEPTH) ** 0.25
LN_EPS = 1e-5
NORM_EPS = 1e-6

GDN_QK = GDN_HEADS * GDN_DK
GDN_VW = GDN_HEADS * GDN_DV
RET_QK = RET_HEADS * RET_DK
RET_VW = RET_HEADS * RET_DV

LANES = 128
SUBLANES = 8
VMEM_LIMIT_BYTES = 56 * 1024 * 1024

SUPER = 256
NEG_INF = float("-inf")

COL_QA = 0
COL_KA = COL_QA + GDN_HEADS
COL_VA = COL_KA + GDN_HEADS
COL_ZA = COL_VA + GDN_HEADS
COL_QKB = COL_ZA + GDN_HEADS
COL_VB = COL_QKB + RET_HEADS
COL_GB = COL_VB + RET_HEADS
COL_BA = COL_GB + RET_HEADS
N_COLBLK = COL_BA + 1
PROJ_W = N_COLBLK * LANES
HEAD_W = GDN_HEADS * LANES
MIX_TS = 512
POST_MIX_TM = 512
DEST_TM = 512
DISPATCH_TM = 1024
COMBINE_TM = 128
EXPERT_CHUNK = 512
EXPERT_PIECE = 64
PACK_ROWS = 4

BF16 = jnp.bfloat16
F32 = jnp.float32


def _dot(a, b):
    return jnp.dot(a, b, preferred_element_type=F32)


def _dot_nt(a, b):
    return lax.dot_general(a, b, (((1,), (1,)), ((), ())), preferred_element_type=F32)


def _silu(x):
    return x * jax.nn.sigmoid(x)


def _softplus(x):
    return jnp.maximum(x, 0.0) + jnp.log1p(jnp.exp(-jnp.abs(x)))


def _iota(shape, dim, dtype=jnp.int32):
    return lax.broadcasted_iota(dtype, shape, dim)


def _params(sem):
    return pltpu.CompilerParams(dimension_semantics=sem, vmem_limit_bytes=VMEM_LIMIT_BYTES)


def _load_token_tiles(ref, n, lead=()):
    parts = [ref[lead + (pl.ds(s, n, stride=SUBLANES), slice(None))] for s in range(SUBLANES)]
    return jnp.concatenate(parts, axis=1)


def _store_token_tiles(ref, val, lead=()):
    n = val.shape[0]
    for s in range(SUBLANES):
        ref[lead + (pl.ds(s, n, stride=SUBLANES), slice(None))] = val[:, s * LANES:(s + 1) * LANES]


def _tile_copy(src_ref, src_row, dst_ref, dst_row, sem, n_slots=1, slot_rows=SUBLANES):
    rows = n_slots * slot_rows
    aligned = lambda r: r if isinstance(r, int) else pl.multiple_of(r, slot_rows)
    return pltpu.make_async_copy(src_ref.at[pl.ds(aligned(src_row), rows)],
                                 dst_ref.at[pl.ds(aligned(dst_row), rows)], sem)


def _store_packed(ref, val, lead=()):
    n, width = val.shape
    bits = pltpu.bitcast(val.astype(BF16).astype(F32), jnp.uint32)
    words = (bits[:, width // 2:] & jnp.uint32(0xFFFF0000)) | (bits[:, :width // 2] >> 16)
    for s in range(PACK_ROWS):
        ref[lead + (pl.ds(s, n, stride=PACK_ROWS), slice(None))] = words[:, s * LANES:(s + 1) * LANES]


def _load_packed(ref, n, lead=(), dtype=F32):
    words = [ref[lead + (pl.ds(s, n, stride=PACK_ROWS), slice(None))] for s in range(PACK_ROWS)]
    lo = [pltpu.bitcast(w << 16, F32) for w in words]
    hi = [pltpu.bitcast(w & jnp.uint32(0xFFFF0000), F32) for w in words]
    return jnp.concatenate(lo + hi, axis=1).astype(dtype)


def _mixer_kernel(hp_ref,
                  x_ref, xn_ref,
                  w_ref, cw_ref, nw_ref, rc_ref, rs_ref,
                  oa_ref, ob_ref, proj_ref, st_ref, st2_ref):
    j = pl.program_id(1)
    ts = x_ref.shape[0]
    n_super = ts // SUPER
    n_chunk = SUPER // GDN_CHUNK

    def project(rows, slot):
        xb = rows.astype(BF16)
        step = 4 * LANES
        for c in range(0, PROJ_W, step):
            n = min(step, PROJ_W - c)
            proj_ref[slot, :, c:c + n] = _dot(xb, w_ref[:, c:c + n])

    @pl.when(j == 0)
    def _():
        st_ref[...] = jnp.zeros_like(st_ref)
        st2_ref[...] = jnp.zeros_like(st2_ref)

    @pl.when((j == 0) & (pl.program_id(0) == 0))
    def _():
        proj_ref[1] = jnp.zeros(proj_ref.shape[1:], F32)
        project(x_ref[0:SUPER, :], 0)

    ri = _iota((SUPER, SUPER), 0)
    ci = _iota((SUPER, SUPER), 1)
    shift = int(math.log2(GDN_CHUNK))
    same = (ri >> shift) == (ci >> shift)
    causal = same & (ci <= ri)
    strict = same & (ci < ri)
    causal_t = same & (ri <= ci)
    eye = jnp.where(ri == ci, 1.0, 0.0)
    lane = _iota((SUPER, LANES), 1)
    dpos = jnp.maximum((ri - ci).astype(F32), 0.0)
    lower = ri >= ci
    pos = _iota((SUPER, 1), 0).astype(F32)
    half = lane < RET_DK
    pair_lo = (lane % RET_DK) < (RET_DK // 2)
    lane1 = _iota((1, LANES), 1)
    neg_a_lane = jnp.zeros((1, LANES), F32)
    dt_b_lane = jnp.zeros((1, LANES), F32)
    for h in range(GDN_HEADS):
        neg_a_lane = jnp.where(lane1 == h + GDN_HEADS, hp_ref[0, h], neg_a_lane)
        dt_b_lane = jnp.where(lane1 == h + GDN_HEADS, hp_ref[1, h], dt_b_lane)

    def pcols(col_blk, h):
        return slice((col_blk + h) * LANES, (col_blk + h + 1) * LANES)

    def conv_silu(col_blk, h, slot, s, tail):
        cur = proj_ref[slot, :, pcols(col_blk, h)]
        prev = jnp.where((s == 0) & (j == 0), 0.0, tail[:, pcols(col_blk, h)])
        ext = jnp.concatenate([prev, cur], axis=0)
        acc = jnp.zeros((SUPER, LANES), F32)
        for jj in range(GDN_CONV):
            off = SUBLANES - (GDN_CONV - 1) + jj
            acc = acc + cw_ref[jj:jj + 1, pcols(col_blk, h)] * ext[off:off + SUPER, :]
        return _silu(acc)

    def l2n(u):
        return u * lax.rsqrt(jnp.sum(u * u, axis=-1, keepdims=True) + NORM_EPS)

    heads = range(GDN_HEADS)
    hcols = [slice(h * LANES, (h + 1) * LANES) for h in heads]

    def gdn_heads(r0, s, slot, tail):
        q = [l2n(conv_silu(COL_QA, h, slot, s, tail)) * (GDN_DK ** -0.5) for h in heads]
        k = [l2n(conv_silu(COL_KA, h, slot, s, tail)) for h in heads]
        v = [conv_silu(COL_VA, h, slot, s, tail) for h in heads]
        ba = proj_ref[slot, :, pcols(COL_BA, 0)]
        gates = jnp.where(lane < GDN_HEADS, jax.nn.sigmoid(ba), neg_a_lane * _softplus(ba + dt_b_lane))
        gates_t = gates.T
        beta, g_col, gl_col, decay_c, decay_s = [], [], [], [], []
        for h in heads:
            beta.append(jnp.sum(jnp.where(lane == h, gates, 0.0), axis=1, keepdims=True))
            la_col = jnp.sum(jnp.where(lane == h + GDN_HEADS, gates, 0.0), axis=1, keepdims=True)
            la_row = gates_t[h + GDN_HEADS:h + GDN_HEADS + 1, :]
            gc = jnp.sum(jnp.where(causal, la_row, 0.0), axis=1, keepdims=True)
            gr = jnp.sum(jnp.where(causal_t, la_col, 0.0), axis=0, keepdims=True)
            g_col.append(gc)
            gl_col.append(jnp.sum(jnp.where(same, la_row, 0.0), axis=1, keepdims=True))
            dc = jnp.exp(jnp.where(causal, gc - gr, NEG_INF))
            decay_c.append(dc)
            decay_s.append(jnp.where(strict, dc, 0.0))
        kb = [k[h] * beta[h] for h in heads]
        k_bf = [k[h].astype(BF16) for h in heads]
        low = [_dot_nt(kb[h].astype(BF16), k_bf[h]) * decay_s[h] for h in heads]
        p = [eye - low[h] for h in heads]
        sq = low
        for _ in range(shift - 1):
            sq_bf = [sq[h].astype(BF16) for h in heads]
            sq = [_dot(sq_bf[h], sq_bf[h]) for h in heads]
            p = [p[h] + _dot(p[h].astype(BF16), sq[h].astype(BF16)) for h in heads]
        eg = [jnp.exp(g_col[h]) for h in heads]
        rhs = [jnp.concatenate([v[h] * beta[h], kb[h] * eg[h]], axis=1).astype(BF16) for h in heads]
        wk = [_dot(p[h].astype(BF16), rhs[h]) for h in heads]
        w_val = [wk[h][:, :GDN_DV] for h in heads]
        k_cum = [wk[h][:, GDN_DV:].astype(BF16) for h in heads]
        attn = [(_dot_nt(q[h].astype(BF16), k_bf[h]) * decay_c[h]).astype(BF16) for h in heads]
        q_dec = [(q[h] * eg[h]).astype(BF16) for h in heads]
        kd_t = [(k[h] * jnp.exp(gl_col[h] - g_col[h])).T.astype(BF16) for h in heads]
        gl = [jnp.exp(gl_col[h]) for h in heads]
        st = [st_ref[h] for h in heads]
        outs = [[] for _ in heads]
        for c in range(n_chunk):
            lo = c * GDN_CHUNK
            hi = lo + GDN_CHUNK
            st_bf = [st[h].astype(BF16) for h in heads]
            v_new = [w_val[h][lo:hi] - _dot(k_cum[h][lo:hi], st_bf[h]) for h in heads]
            v_pad = []
            for h in heads:
                pieces = [jnp.zeros((GDN_CHUNK, GDN_DV), F32)] * n_chunk
                pieces[c] = v_new[h]
                v_pad.append(jnp.concatenate(pieces, axis=0).astype(BF16))
            for h in heads:
                outs[h].append(_dot(q_dec[h][lo:hi], st_bf[h]) + _dot(attn[h][lo:hi], v_pad[h]))
            st = [st[h] * gl[h][lo:lo + 1, :] + _dot(kd_t[h], v_pad[h]) for h in heads]
        for h in heads:
            st_ref[h] = st[h]
            o = jnp.concatenate(outs[h], axis=0)
            o = o * lax.rsqrt(jnp.mean(o * o, axis=-1, keepdims=True) + NORM_EPS) * nw_ref[...]
            z = proj_ref[slot, :, pcols(COL_ZA, h)]
            oa_ref[pl.ds(r0, SUPER), hcols[h]] = (o * _silu(z)).astype(oa_ref.dtype)

    def ret_heads(r0, slot):
        rc = rc_ref[pl.ds(r0, SUPER), :]
        rs = rs_ref[pl.ds(r0, SUPER), :]
        lg = [hp_ref[2, h] for h in heads]
        qm, km, vb = [], [], []
        for h in heads:
            x = proj_ref[slot, :, pcols(COL_QKB, h)]
            swapped = jnp.where(pair_lo, pltpu.roll(x, LANES - RET_DK // 2, 1), pltpu.roll(x, RET_DK // 2, 1))
            xr = x * rc + swapped * rs
            qm.append(jnp.where(half, xr, 0.0))
            km.append(jnp.where(half, pltpu.roll(xr, LANES - RET_DK, 1), 0.0))
            vb.append(proj_ref[slot, :, pcols(COL_VB, h)].astype(BF16))
        inner = [(_dot_nt(qm[h].astype(BF16), km[h].astype(BF16))
                  * jnp.where(lower, jnp.exp(lg[h] * dpos), 0.0)).astype(BF16) for h in heads]
        st2 = [st2_ref[h] for h in heads]
        ob = [_dot(inner[h], vb[h])
              + _dot((qm[h] * jnp.exp(lg[h] * (pos + 1.0))).astype(BF16), st2[h].astype(BF16)) for h in heads]
        for h in heads:
            g_chunk = jnp.exp(jnp.full((1, 1), SUPER, F32) * lg[h])
            st2_ref[h] = st2[h] * g_chunk + _dot((km[h] * jnp.exp(lg[h] * (SUPER - 1.0 - pos))).T.astype(BF16), vb[h])
        for h in heads:
            mu = jnp.mean(ob[h], axis=-1, keepdims=True)
            oc = ob[h] - mu
            oc = oc * lax.rsqrt(jnp.mean(oc * oc, axis=-1, keepdims=True) + NORM_EPS)
            gate = proj_ref[slot, :, pcols(COL_GB, h)]
            ob_ref[pl.ds(r0, SUPER), hcols[h]] = (oc * _silu(gate)).astype(ob_ref.dtype)

    for s in range(n_super):
        r0 = s * SUPER
        slot = s % 2
        tail = proj_ref[1 - slot, SUPER - SUBLANES:SUPER, 0:COL_ZA * LANES]
        nxt = x_ref[r0 + SUPER:r0 + 2 * SUPER, :] if s + 1 < n_super else xn_ref[...]
        project(nxt, 1 - slot)
        gdn_heads(r0, s, slot, tail)
        ret_heads(r0, slot)


def _mixer(x2, w_bf, hp, conv_w, norm_w, rot_c, rot_s, batch, seq):
    t, d = x2.shape
    ts = MIX_TS if seq % MIX_TS == 0 else seq
    nj = seq // ts
    per_blk = ts // SUPER
    assert per_blk % 2 == 0
    last_chunk = t // SUPER - 1
    const2 = lambda shape: pl.BlockSpec(shape, lambda b, j: (0, 0))
    tab = pl.BlockSpec((ts, LANES), lambda b, j: (j, 0))
    in_specs = [
        pl.BlockSpec(memory_space=pltpu.SMEM),
        pl.BlockSpec((ts, d), lambda b, j: (b * nj + j, 0)),
        pl.BlockSpec((SUPER, d), lambda b, j: (jnp.minimum((b * nj + j + 1) * per_blk, last_chunk), 0)),
        const2(w_bf.shape), const2(conv_w.shape), const2((1, GDN_DV)), tab, tab,
    ]
    out_spec = pl.BlockSpec((ts, HEAD_W), lambda b, j: (b * nj + j, 0))
    return pl.pallas_call(
        _mixer_kernel,
        grid=(batch, nj),
        in_specs=in_specs,
        out_specs=[out_spec, out_spec],
        out_shape=[jax.ShapeDtypeStruct((t, GDN_VW), BF16), jax.ShapeDtypeStruct((t, RET_VW), BF16)],
        scratch_shapes=[pltpu.VMEM((2, SUPER, PROJ_W), F32),
                        pltpu.VMEM((GDN_HEADS, GDN_DK, GDN_DV), F32),
                        pltpu.VMEM((RET_HEADS, LANES, RET_DV), F32)],
        compiler_params=_params(("arbitrary", "arbitrary")),
        name="mixer",
    )(hp, x2, x2, w_bf, conv_w, norm_w, rot_c, rot_s)


def _layer_norm(u, g, b):
    mu = jnp.mean(u, axis=-1, keepdims=True)
    uc = u - mu
    var = jnp.mean(uc * uc, axis=-1, keepdims=True)
    return uc * lax.rsqrt(var + LN_EPS) * g + b


def _post_mix_kernel(x_ref, oa_ref, ob_ref, wo_ref, g_ref, b_ref, wr_ref, rb_ref,
                     h_ref, hp_ref, idx_ref, wts_ref, cnt_ref):
    n_exp = wr_ref.shape[0]
    tm = x_ref.shape[0]
    per_grp = n_exp // N_GROUPS
    mix = _dot(oa_ref[...], wo_ref[:GDN_VW, :]) + _dot(ob_ref[...], wo_ref[GDN_VW:, :])
    hh = _layer_norm(DN_ALPHA * x_ref[...] + mix, g_ref[...], b_ref[...])
    _store_token_tiles(h_ref, hh)
    _store_packed(hp_ref, hh)
    scores = jax.nn.sigmoid(_dot_nt(wr_ref[...], hh.astype(BF16)))
    choice = scores + rb_ref[...]
    big = float(n_exp)
    io_g = _iota((per_grp, tm), 0).astype(F32)
    grp = []
    for g in range(N_GROUPS):
        blk = choice[g * per_grp:(g + 1) * per_grp, :]
        m1 = jnp.max(blk, axis=0, keepdims=True)
        i1 = jnp.min(jnp.where(blk == m1, io_g, big), axis=0, keepdims=True)
        m2 = jnp.max(jnp.where(io_g == i1, NEG_INF, blk), axis=0, keepdims=True)
        grp.append(m1 + m2)
    gsc = jnp.concatenate(grp, axis=0)
    io8 = _iota((N_GROUPS, tm), 0).astype(F32)
    gsel = jnp.zeros((N_GROUPS, tm), F32)
    for _ in range(TOPK_GROUPS):
        m = jnp.max(gsc, axis=0, keepdims=True)
        i = jnp.min(jnp.where(gsc == m, io8, big), axis=0, keepdims=True)
        hit = io8 == i
        gsel = jnp.where(hit, 1.0, gsel)
        gsc = jnp.where(hit, NEG_INF, gsc)
    masked = jnp.concatenate(
        [jnp.where(gsel[g:g + 1, :] > 0.0, choice[g * per_grp:(g + 1) * per_grp, :], NEG_INF)
         for g in range(N_GROUPS)], axis=0)
    io_e = _iota((n_exp, tm), 0).astype(F32)
    allowed = masked
    ids, ws = [], []
    for _ in range(TOP_K):
        m = jnp.max(masked, axis=0, keepdims=True)
        i = jnp.min(jnp.where(masked == m, io_e, big), axis=0, keepdims=True)
        hit = io_e == i
        ws.append(jnp.sum(jnp.where(hit, scores, 0.0), axis=0, keepdims=True))
        ids.append(i)
        masked = jnp.where(hit, NEG_INF, masked)
    sel = jnp.where(masked == NEG_INF, jnp.where(allowed == NEG_INF, 0.0, 1.0), 0.0)
    w = jnp.concatenate(ws, axis=0)
    wts_ref[...] = w / jnp.sum(w, axis=0, keepdims=True) * ROUTED_SCALE
    idx_ref[...] = jnp.concatenate(ids, axis=0).astype(jnp.int32)

    @pl.when(pl.program_id(0) == 0)
    def _():
        cnt_ref[...] = jnp.zeros_like(cnt_ref)

    cnt_ref[...] += jnp.sum(sel, axis=1, keepdims=True)


def _post_mix(x2, oa, ob, wo_bf, g1, b1, wr_t, rbias, tm):
    t, d = x2.shape
    n_exp = wr_t.shape[0]
    row = lambda w: pl.BlockSpec((tm, w), lambda i: (i, 0))
    const = lambda shape: pl.BlockSpec(shape, lambda i: (0, 0))
    tok = pl.BlockSpec((TOP_K, tm), lambda i: (0, i))
    return pl.pallas_call(
        _post_mix_kernel,
        grid=(t // tm,),
        in_specs=[row(d), row(GDN_VW), row(RET_VW), const(wo_bf.shape), const((1, d)), const((1, d)),
                  const(wr_t.shape), const((n_exp, 1))],
        out_specs=[pl.BlockSpec((tm * SUBLANES, LANES), lambda i: (i, 0)),
                   pl.BlockSpec((tm * PACK_ROWS, LANES), lambda i: (i, 0)), tok, tok, const((n_exp, 1))],
        out_shape=[jax.ShapeDtypeStruct((t * SUBLANES, LANES), F32),
                   jax.ShapeDtypeStruct((t * PACK_ROWS, LANES), jnp.uint32),
                   jax.ShapeDtypeStruct((TOP_K, t), jnp.int32),
                   jax.ShapeDtypeStruct((TOP_K, t), F32),
                   jax.ShapeDtypeStruct((n_exp, 1), F32)],
        compiler_params=_params(("arbitrary",)),
        name="post_mix",
    )(x2, oa, ob, wo_bf, g1, b1, wr_t, rbias)


def _dest_kernel(idx_ref, base_ref, dest_ref, carry_ref):
    n_exp = base_ref.shape[0]
    tm = idx_ref.shape[1]

    @pl.when(pl.program_id(0) == 0)
    def _():
        carry_ref[...] = jnp.zeros_like(carry_ref)

    io_e = _iota((n_exp, tm), 0)
    idx = idx_ref[...]
    hits = [io_e == idx[k:k + 1, :] for k in range(TOP_K)]
    sel = jnp.zeros((n_exp, tm), F32)
    for hit in hits:
        sel = jnp.where(hit, 1.0, sel)
    before = jnp.where(_iota((tm, tm), 0) < _iota((tm, tm), 1), 1.0, 0.0).astype(BF16)
    rank = _dot(sel.astype(BF16), before)
    tot = base_ref[...] + carry_ref[...] + rank
    dest = [jnp.sum(jnp.where(hit, tot, 0.0), axis=0, keepdims=True) for hit in hits]
    dest_ref[...] = jnp.concatenate(dest, axis=0).astype(jnp.int32) * PACK_ROWS
    carry_ref[...] += jnp.sum(sel, axis=1, keepdims=True)


def _dest(idx_t, base, tm):
    t = idx_t.shape[1]
    n_exp = base.shape[0]
    tok = pl.BlockSpec((TOP_K, tm), lambda i: (0, i))
    return pl.pallas_call(
        _dest_kernel,
        grid=(t // tm,),
        in_specs=[tok, pl.BlockSpec((n_exp, 1), lambda i: (0, 0))],
        out_specs=tok,
        out_shape=jax.ShapeDtypeStruct((TOP_K, t), jnp.int32),
        scratch_shapes=[pltpu.VMEM((n_exp, 1), F32)],
        compiler_params=_params(("arbitrary",)),
        name="dest",
    )(idx_t, base)


def _dispatch_kernel(dest_ref, h_ref, xs_ref, zero_ref, sem, zsem):
    tm = h_ref.shape[0] // PACK_ROWS
    last = pl.num_programs(0) - 1
    pad_row = xs_ref.shape[0] - zero_ref.shape[0]

    @pl.when(pl.program_id(0) == last)
    def _():
        zero_ref[...] = jnp.zeros_like(zero_ref)
        _tile_copy(zero_ref, 0, xs_ref, pad_row, zsem, EXPERT_CHUNK, PACK_ROWS).start()

    def start(t, c):
        for k in range(TOP_K):
            _tile_copy(h_ref, t * PACK_ROWS, xs_ref, dest_ref[t * TOP_K + k], sem, 1, PACK_ROWS).start(priority=k % 2)
        return c

    lax.fori_loop(0, tm, start, 0)

    for k in range(TOP_K):
        _tile_copy(h_ref, 0, xs_ref, 0, sem, tm, PACK_ROWS).wait()

    @pl.when(pl.program_id(0) == last)
    def _():
        _tile_copy(zero_ref, 0, xs_ref, pad_row, zsem, EXPERT_CHUNK, PACK_ROWS).wait()


def _dispatch(dest_t, h_pk, tm):
    t = dest_t.shape[0] // TOP_K
    n_rows = (t * TOP_K + EXPERT_CHUNK) * PACK_ROWS
    return pl.pallas_call(
        _dispatch_kernel,
        grid=(t // tm,),
        in_specs=[pl.BlockSpec((tm * TOP_K,), lambda i: (i,), memory_space=pltpu.SMEM),
                  pl.BlockSpec((tm * PACK_ROWS, LANES), lambda i: (i, 0))],
        out_specs=pl.BlockSpec(memory_space=pl.ANY),
        out_shape=jax.ShapeDtypeStruct((n_rows, LANES), jnp.uint32),
        scratch_shapes=[pltpu.VMEM((EXPERT_CHUNK * PACK_ROWS, LANES), jnp.uint32),
                        pltpu.SemaphoreType.DMA(()), pltpu.SemaphoreType.DMA(())],
        compiler_params=_params(("arbitrary",)),
        name="dispatch",
    )(dest_t, h_pk)


def _chunk_pieces(n):
    pieces = [(j < n // EXPERT_PIECE, j * EXPERT_PIECE, EXPERT_PIECE) for j in range(EXPERT_CHUNK // EXPERT_PIECE)]
    size = EXPERT_PIECE // 2
    while size >= 1:
        shift = int(math.log2(size)) + 1
        pieces.append(((n & size) != 0, (n >> shift) << shift, size))
        size //= 2
    return pieces


ST_CHUNKS = 0
ST_ISSUED = 1
ST_CUR_E = 2
ST_CUR_I = 3
ST_WRITE = 4
N_XBUF = 4
N_OBUF = 3


def _expert_kernel(start_ref, count_ref, wg_ref, wu_ref, wd_ref, xs_ref, eo_ref,
                   xbuf, obuf, st_ref, rsem, wsem):
    e = pl.program_id(0)
    n_exp = pl.num_programs(0)
    ch = EXPERT_CHUNK
    s0 = start_ref[e]
    cnt = count_ref[e]
    n_ch = (cnt + ch - 1) // ch

    def read_parts(first_slot, buf):
        return [_tile_copy(xs_ref, (first_slot + j * EXPERT_PIECE) * PACK_ROWS, xbuf.at[buf],
                           j * EXPERT_PIECE * PACK_ROWS, rsem.at[buf], EXPERT_PIECE, PACK_ROWS)
                for j in range(ch // EXPERT_PIECE)]

    def read_next():
        def exhausted(c):
            cnt_c = count_ref[jnp.minimum(c[0], n_exp - 1)]
            return jnp.logical_and(c[0] < n_exp, c[1] * ch >= cnt_c)

        ce, ci = lax.while_loop(exhausted, lambda c: (c[0] + 1, jnp.int32(0)),
                                (st_ref[ST_CUR_E], st_ref[ST_CUR_I]))

        @pl.when(ce < n_exp)
        def _():
            first_slot = start_ref[jnp.minimum(ce, n_exp - 1)] + ci * ch
            buf = lax.rem(st_ref[ST_ISSUED], N_XBUF)
            for j, cp in enumerate(read_parts(first_slot, buf)):
                cp.start(priority=j % 2)
            st_ref[ST_ISSUED] = st_ref[ST_ISSUED] + 1

        st_ref[ST_CUR_E] = ce
        st_ref[ST_CUR_I] = ci + 1

    def wait_read(buf):
        for cp in read_parts(0, buf):
            cp.wait()

    def out_pieces(buf, first_slot, n, wait):
        for j, (pred, off, size) in enumerate(_chunk_pieces(n)):
            @pl.when(pred)
            def _(j=j, off=off, size=size):
                cp = _tile_copy(obuf.at[buf], off * PACK_ROWS, eo_ref, (first_slot + off) * PACK_ROWS,
                                wsem.at[buf], size, PACK_ROWS)
                cp.wait() if wait else cp.start(priority=j % 2)

    def drain(buf):
        out_pieces(buf, st_ref[ST_WRITE + 2 * buf], st_ref[ST_WRITE + 2 * buf + 1], True)
        st_ref[ST_WRITE + 2 * buf + 1] = 0

    @pl.when(e == 0)
    def _():
        for i in range(st_ref.shape[0]):
            st_ref[i] = 0
        for _ in range(N_XBUF - 1):
            read_next()

    def body(i, c):
        g = st_ref[ST_CHUNKS]
        buf = lax.rem(g, N_XBUF)
        wait_read(buf)
        read_next()
        obf = lax.rem(g, N_OBUF)
        drain(obf)
        first_slot = s0 + i * ch
        n = jnp.minimum(ch, cnt - i * ch)

        def swiglu(rows):
            xb = _load_packed(xbuf, rows, lead=(buf,), dtype=BF16)
            hid = _silu(_dot(xb, wg_ref[0].astype(BF16))) * _dot(xb, wu_ref[0].astype(BF16))
            _store_packed(obuf, _dot(hid.astype(BF16), wd_ref[0].astype(BF16)), lead=(obf,))

        pl.when(n > ch // 2)(lambda: swiglu(ch))
        pl.when(n <= ch // 2)(lambda: swiglu(ch // 2))
        out_pieces(obf, first_slot, n, False)
        st_ref[ST_WRITE + 2 * obf] = first_slot
        st_ref[ST_WRITE + 2 * obf + 1] = n
        st_ref[ST_CHUNKS] = g + 1
        return c

    lax.fori_loop(0, n_ch, body, 0)

    @pl.when(e == n_exp - 1)
    def _():
        for b in range(N_OBUF):
            drain(b)


def _experts(starts, counts, xs, wg, wu, wd):
    n_exp, d, d_e = wg.shape
    n_slot = xs.shape[0] // PACK_ROWS - EXPERT_CHUNK
    rows = EXPERT_CHUNK * PACK_ROWS
    grid_spec = pltpu.PrefetchScalarGridSpec(
        num_scalar_prefetch=2,
        grid=(n_exp,),
        in_specs=[pl.BlockSpec((1, d, d_e), lambda e, st, ct: (e, 0, 0)),
                  pl.BlockSpec((1, d, d_e), lambda e, st, ct: (e, 0, 0)),
                  pl.BlockSpec((1, d_e, d), lambda e, st, ct: (e, 0, 0)),
                  pl.BlockSpec(memory_space=pl.ANY)],
        out_specs=pl.BlockSpec(memory_space=pl.ANY),
        scratch_shapes=[pltpu.VMEM((N_XBUF, rows, LANES), jnp.uint32), pltpu.VMEM((N_OBUF, rows, LANES), jnp.uint32),
                        pltpu.SMEM((ST_WRITE + 2 * N_OBUF,), jnp.int32),
                        pltpu.SemaphoreType.DMA((N_XBUF,)), pltpu.SemaphoreType.DMA((N_OBUF,))],
    )
    return pl.pallas_call(
        _expert_kernel,
        grid_spec=grid_spec,
        out_shape=jax.ShapeDtypeStruct((n_slot * PACK_ROWS, LANES), jnp.uint32),
        compiler_params=_params(("arbitrary",)),
        name="experts",
    )(starts, counts, wg, wu, wd, xs)


def _combine_kernel(dest_ref, dnext_ref, wts_ref, h_ref, eo_ref, wgs_ref, wus_ref, wds_ref, g_ref, b_ref,
                    out_ref, gbuf, sem):
    i = pl.program_id(0)
    tm = out_ref.shape[0]
    buf = lax.rem(i, 2)

    def gather(d_ref, bf):
        def start(t, c):
            for k in range(TOP_K):
                _tile_copy(eo_ref, d_ref[t * TOP_K + k], gbuf.at[bf, k], t * PACK_ROWS, sem.at[bf],
                           1, PACK_ROWS).start(priority=k % 2)
            return c

        lax.fori_loop(0, tm, start, 0)

    @pl.when(i == 0)
    def _():
        gather(dest_ref, 0)

    def step(bf):
        for t in range(tm):
            for k in range(TOP_K):
                _tile_copy(eo_ref, dnext_ref[t * TOP_K + k], gbuf.at[1 - bf, k], t * PACK_ROWS, sem.at[1 - bf],
                           1, PACK_ROWS).start(priority=k % 2)
        hh = _load_token_tiles(h_ref, tm)
        hb = hh.astype(BF16)
        hid = _silu(_dot(hb, wgs_ref[...])) * _dot(hb, wus_ref[...])
        acc = DN_ALPHA * hh + _dot(hid.astype(BF16), wds_ref[...])
        w_t = jnp.concatenate([wts_ref[...], jnp.zeros((LANES - TOP_K, tm), F32)], axis=0).T
        for k in range(TOP_K):
            _tile_copy(eo_ref, 0, gbuf.at[bf, k], 0, sem.at[bf], tm, PACK_ROWS).wait()
        for k in range(TOP_K):
            acc = acc + _load_packed(gbuf, tm, lead=(bf, k)) * w_t[:, k:k + 1]
        out_ref[...] = _layer_norm(acc, g_ref[...], b_ref[...])

        @pl.when(i == pl.num_programs(0) - 1)
        def _():
            for k in range(TOP_K):
                _tile_copy(eo_ref, 0, gbuf.at[1 - bf, k], 0, sem.at[1 - bf], tm, PACK_ROWS).wait()

    for bf in range(2):
        pl.when(buf == bf)(lambda bf=bf: step(bf))


def _combine(dest_t, wts_t, h_tt, eo, wgs, wus, wds, g2, b2, tm):
    t = dest_t.shape[0] // TOP_K
    d = wgs.shape[0]
    const = lambda shape: pl.BlockSpec(shape, lambda i: (0, 0))
    n_tiles = t // tm
    return pl.pallas_call(
        _combine_kernel,
        grid=(n_tiles,),
        in_specs=[pl.BlockSpec((tm * TOP_K,), lambda i: (i,), memory_space=pltpu.SMEM),
                  pl.BlockSpec((tm * TOP_K,), lambda i: (jnp.minimum(i + 1, n_tiles - 1),), memory_space=pltpu.SMEM),
                  pl.BlockSpec((TOP_K, tm), lambda i: (0, i)),
                  pl.BlockSpec((tm * SUBLANES, LANES), lambda i: (i, 0)),
                  pl.BlockSpec(memory_space=pl.ANY),
                  const(wgs.shape), const(wus.shape), const(wds.shape), const((1, d)), const((1, d))],
        out_specs=pl.BlockSpec((tm, d), lambda i: (i, 0)),
        out_shape=jax.ShapeDtypeStruct((t, d), F32),
        scratch_shapes=[pltpu.VMEM((2, TOP_K, tm * PACK_ROWS, LANES), jnp.uint32), pltpu.SemaphoreType.DMA((2,))],
        compiler_params=_params(("arbitrary",)),
        name="combine",
    )(dest_t, dest_t, wts_t, h_tt, eo, wgs, wus, wds, g2, b2)


def _regroup_w_in(w):
    d = w.shape[0]
    splits = np.cumsum([GDN_QK, GDN_QK, GDN_VW, GDN_VW, GDN_HEADS, GDN_HEADS, RET_QK, RET_QK, RET_VW])
    qa, ka, va, za, ba, aa, qb, kb, vb, gb = jnp.split(w, splits.tolist(), axis=1)
    ba_blk = jnp.concatenate([ba, aa, jnp.zeros((d, LANES - 2 * GDN_HEADS), w.dtype)], axis=1)
    qb = qb.reshape(d, RET_HEADS, RET_DK // 2, 2)
    kb = kb.reshape(d, RET_HEADS, RET_DK // 2, 2)
    qkb = jnp.concatenate([qb[..., 0], qb[..., 1], kb[..., 0], kb[..., 1]], axis=-1).reshape(d, RET_HEADS * LANES)
    return jnp.concatenate([qa, ka, va, za, qkb, vb, gb, ba_blk], axis=1).astype(BF16)


def _rotary_tables(seq):
    inv = 1.0 / (ROPE_BASE ** jnp.linspace(0.0, 1.0, RET_DK // 2, dtype=F32))
    ang = jnp.arange(seq, dtype=F32)[:, None] * inv[None, :]
    c, s = jnp.cos(ang), jnp.sin(ang)
    ks = RET_DK ** -0.5
    rot_c = jnp.concatenate([c, c, c * ks, c * ks], axis=1)
    rot_s = jnp.concatenate([-s, s, -s * ks, s * ks], axis=1)
    return rot_c, rot_s


def _tile(n, pref):
    return pref if n % pref == 0 else n


def kernel(x, w_in, gdn_conv_w, gdn_a_log, gdn_dt_bias, gdn_norm_w, w_out, ln1_g, ln1_b, w_router, router_bias,
           w_gate_e, w_up_e, w_down_e, w_gate_s, w_up_s, w_down_s, ln2_g, ln2_b):
    batch, seq, d = x.shape
    t = batch * seq
    n_exp = w_router.shape[-1]
    hcur = x.reshape(t, d)
    for l in range(DEPTH):
        w_bf = _regroup_w_in(w_in[l])
        log_gamma = jnp.log(1.0 - 2.0 ** (-5.0 - jnp.arange(RET_HEADS, dtype=F32)))
        hp = jnp.stack([-jnp.exp(gdn_a_log[l].astype(F32)), gdn_dt_bias[l].astype(F32), log_gamma])
        rot_c, rot_s = _rotary_tables(seq)
        oa, ob = _mixer(hcur, w_bf, hp,gdn_conv_w[l].astype(F32), gdn_norm_w[l].reshape(1, -1).astype(F32),
                        rot_c, rot_s, batch, seq)
        hh, h_pk, idx_t, wts_t, counts = _post_mix(
            hcur, oa, ob, w_out[l].astype(BF16), ln1_g[l].reshape(1, d), ln1_b[l].reshape(1, d),
            w_router[l].T.astype(BF16), router_bias[l].reshape(n_exp, 1).astype(F32), _tile(t, POST_MIX_TM))
        base = jnp.cumsum(counts, axis=0) - counts
        dest_t = _dest(idx_t, base, _tile(t, DEST_TM)).T.reshape(t * TOP_K)
        xs = _dispatch(dest_t, h_pk, _tile(t, DISPATCH_TM))
        eo = _experts(base.reshape(n_exp).astype(jnp.int32), counts.reshape(n_exp).astype(jnp.int32),
                      xs, w_gate_e[l], w_up_e[l], w_down_e[l])
        hcur = _combine(dest_t, wts_t, hh, eo, w_gate_s[l].astype(BF16), w_up_s[l].astype(BF16),
                        w_down_s[l].astype(BF16), ln2_g[l].reshape(1, d), ln2_b[l].reshape(1, d), _tile(t, COMBINE_TM))
    return hcur.reshape(batch, seq, d)
```

```python
import math

import numpy as np
import jax
import jax.numpy as jnp
from jax import lax
from jax.experimental import pallas as pl
from jax.experimental.pallas import tpu as pltpu

GDN_HEADS = 4
GDN_DK = 128
GDN_DV = 128
GDN_CONV = 4
GDN_CHUNK = 64
RET_HEADS = 4
RET_DK = 64
RET_DV = 128
ROPE_BASE = 10000.0
N_GROUPS = 8
TOPK_GROUPS = 4
TOP_K = 8
ROUTED_SCALE = 2.5
DEPTH = 1
DN_ALPHA = (2.0 * DEPTH) ** 0.25
LN_EPS = 1e-5
NORM_EPS = 1e-6

GDN_QK = GDN_HEADS * GDN_DK
GDN_VW = GDN_HEADS * GDN_DV
RET_QK = RET_HEADS * RET_DK
RET_VW = RET_HEADS * RET_DV

LANES = 128
SUBLANES = 8
VMEM_LIMIT_BYTES = 56 * 1024 * 1024

SUPER = 256
NEG_INF = float("-inf")

COL_QA = 0
COL_KA = COL_QA + GDN_HEADS
COL_VA = COL_KA + GDN_HEADS
COL_ZA = COL_VA + GDN_HEADS
COL_QKB = COL_ZA + GDN_HEADS
COL_VB = COL_QKB + RET_HEADS
COL_GB = COL_VB + RET_HEADS
COL_BA = COL_GB + RET_HEADS
N_COLBLK = COL_BA + 1
PROJ_W = N_COLBLK * LANES
HEAD_W = GDN_HEADS * LANES
MIX_TS = 512
POST_MIX_TM = 512
DEST_TM = 512
DISPATCH_TM = 1024
COMBINE_TM = 128
EXPERT_CHUNK = 512
EXPERT_PIECE = 64
PACK_ROWS = 4

BF16 = jnp.bfloat16
F32 = jnp.float32


def _dot(a, b):
    return jnp.dot(a, b, preferred_element_type=F32)


def _dot_nt(a, b):
    return lax.dot_general(a, b, (((1,), (1,)), ((), ())), preferred_element_type=F32)


def _silu(x):
    return x * jax.nn.sigmoid(x)


def _softplus(x):
    return jnp.maximum(x, 0.0) + jnp.log1p(jnp.exp(-jnp.abs(x)))


def _iota(shape, dim, dtype=jnp.int32):
    return lax.broadcasted_iota(dtype, shape, dim)


def _params(sem):
    return pltpu.CompilerParams(dimension_semantics=sem, vmem_limit_bytes=VMEM_LIMIT_BYTES)


def _load_token_tiles(ref, n, lead=()):
    parts = [ref[lead + (pl.ds(s, n, stride=SUBLANES), slice(None))] for s in range(SUBLANES)]
    return jnp.concatenate(parts, axis=1)


def _store_token_tiles(ref, val, lead=()):
    n = val.shape[0]
    for s in range(SUBLANES):
        ref[lead + (pl.ds(s, n, stride=SUBLANES), slice(None))] = val[:, s * LANES:(s + 1) * LANES]


def _tile_copy(src_ref, src_row, dst_ref, dst_row, sem, n_slots=1, slot_rows=SUBLANES):
    rows = n_slots * slot_rows
    aligned = lambda r: r if isinstance(r, int) else pl.multiple_of(r, slot_rows)
    return pltpu.make_async_copy(src_ref.at[pl.ds(aligned(src_row), rows)],
                                 dst_ref.at[pl.ds(aligned(dst_row), rows)], sem)


def _store_packed(ref, val, lead=()):
    n, width = val.shape
    bits = pltpu.bitcast(val.astype(BF16).astype(F32), jnp.uint32)
    words = (bits[:, width // 2:] & jnp.uint32(0xFFFF0000)) | (bits[:, :width // 2] >> 16)
    for s in range(PACK_ROWS):
        ref[lead + (pl.ds(s, n, stride=PACK_ROWS), slice(None))] = words[:, s * LANES:(s + 1) * LANES]


def _load_packed(ref, n, lead=(), dtype=F32):
    words = [ref[lead + (pl.ds(s, n, stride=PACK_ROWS), slice(None))] for s in range(PACK_ROWS)]
    lo = [pltpu.bitcast(w << 16, F32) for w in words]
    hi = [pltpu.bitcast(w & jnp.uint32(0xFFFF0000), F32) for w in words]
    return jnp.concatenate(lo + hi, axis=1).astype(dtype)


def _mixer_kernel(hp_ref,
                  x_ref, xn_ref,
                  w_ref, cw_ref, nw_ref, rc_ref, rs_ref,
                  oa_ref, ob_ref, proj_ref, st_ref, st2_ref):
    j = pl.program_id(1)
    ts = x_ref.shape[0]
    n_super = ts // SUPER
    n_chunk = SUPER // GDN_CHUNK

    def project(rows, slot):
        xb = rows.astype(BF16)
        step = 4 * LANES
        for c in range(0, PROJ_W, step):
            n = min(step, PROJ_W - c)
            proj_ref[slot, :, c:c + n] = _dot(xb, w_ref[:, c:c + n])

    @pl.when(j == 0)
    def _():
        st_ref[...] = jnp.zeros_like(st_ref)
        st2_ref[...] = jnp.zeros_like(st2_ref)

    @pl.when((j == 0) & (pl.program_id(0) == 0))
    def _():
        proj_ref[1] = jnp.zeros(proj_ref.shape[1:], F32)
        project(x_ref[0:SUPER, :], 0)

    ri = _iota((SUPER, SUPER), 0)
    ci = _iota((SUPER, SUPER), 1)
    shift = int(math.log2(GDN_CHUNK))
    same = (ri >> shift) == (ci >> shift)
    causal = same & (ci <= ri)
    strict = same & (ci < ri)
    causal_t = same & (ri <= ci)
    eye = jnp.where(ri == ci, 1.0, 0.0)
    lane = _iota((SUPER, LANES), 1)
    dpos = jnp.maximum((ri - ci).astype(F32), 0.0)
    lower = ri >= ci
    pos = _iota((SUPER, 1), 0).astype(F32)
    half = lane < RET_DK
    pair_lo = (lane % RET_DK) < (RET_DK // 2)
    lane1 = _iota((1, LANES), 1)
    neg_a_lane = jnp.zeros((1, LANES), F32)
    dt_b_lane = jnp.zeros((1, LANES), F32)
    for h in range(GDN_HEADS):
        neg_a_lane = jnp.where(lane1 == h + GDN_HEADS, hp_ref[0, h], neg_a_lane)
        dt_b_lane = jnp.where(lane1 == h + GDN_HEADS, hp_ref[1, h], dt_b_lane)

    def pcols(col_blk, h):
        return slice((col_blk + h) * LANES, (col_blk + h + 1) * LANES)

    def conv_silu(col_blk, h, slot, s, tail):
        cur = proj_ref[slot, :, pcols(col_blk, h)]
        prev = jnp.where((s == 0) & (j == 0), 0.0, tail[:, pcols(col_blk, h)])
        ext = jnp.concatenate([prev, cur], axis=0)
        acc = jnp.zeros((SUPER, LANES), F32)
        for jj in range(GDN_CONV):
            off = SUBLANES - (GDN_CONV - 1) + jj
            acc = acc + cw_ref[jj:jj + 1, pcols(col_blk, h)] * ext[off:off + SUPER, :]
        return _silu(acc)

    def l2n(u):
        return u * lax.rsqrt(jnp.sum(u * u, axis=-1, keepdims=True) + NORM_EPS)

    heads = range(GDN_HEADS)
    hcols = [slice(h * LANES, (h + 1) * LANES) for h in heads]

    def gdn_heads(r0, s, slot, tail):
        q = [l2n(conv_silu(COL_QA, h, slot, s, tail)) * (GDN_DK ** -0.5) for h in heads]
        k = [l2n(conv_silu(COL_KA, h, slot, s, tail)) for h in heads]
        v = [conv_silu(COL_VA, h, slot, s, tail) for h in heads]
        ba = proj_ref[slot, :, pcols(COL_BA, 0)]
        gates = jnp.where(lane < GDN_HEADS, jax.nn.sigmoid(ba), neg_a_lane * _softplus(ba + dt_b_lane))
        gates_t = gates.T
        beta, g_col, gl_col, decay_c, decay_s = [], [], [], [], []
        for h in heads:
            beta.append(jnp.sum(jnp.where(lane == h, gates, 0.0), axis=1, keepdims=True))
            la_col = jnp.sum(jnp.where(lane == h + GDN_HEADS, gates, 0.0), axis=1, keepdims=True)
            la_row = gates_t[h + GDN_HEADS:h + GDN_HEADS + 1, :]
            gc = jnp.sum(jnp.where(causal, la_row, 0.0), axis=1, keepdims=True)
            gr = jnp.sum(jnp.where(causal_t, la_col, 0.0), axis=0, keepdims=True)
            g_col.append(gc)
            gl_col.append(jnp.sum(jnp.where(same, la_row, 0.0), axis=1, keepdims=True))
            dc = jnp.exp(jnp.where(causal, gc - gr, NEG_INF))
            decay_c.append(dc)
            decay_s.append(jnp.where(strict, dc, 0.0))
        kb = [k[h] * beta[h] for h in heads]
        k_bf = [k[h].astype(BF16) for h in heads]
        low = [_dot_nt(kb[h].astype(BF16), k_bf[h]) * decay_s[h] for h in heads]
        p = [eye - low[h] for h in heads]
        sq = low
        for _ in range(shift - 1):
            sq_bf = [sq[h].astype(BF16) for h in heads]
            sq = [_dot(sq_bf[h], sq_bf[h]) for h in heads]
            p = [p[h] + _dot(p[h].astype(BF16), sq[h].astype(BF16)) for h in heads]
        eg = [jnp.exp(g_col[h]) for h in heads]
        rhs = [jnp.concatenate([v[h] * beta[h], kb[h] * eg[h]], axis=1).astype(BF16) for h in heads]
        wk = [_dot(p[h].astype(BF16), rhs[h]) for h in heads]
        w_val = [wk[h][:, :GDN_DV] for h in heads]
        k_cum = [wk[h][:, GDN_DV:].astype(BF16) for h in heads]
        attn = [(_dot_nt(q[h].astype(BF16), k_bf[h]) * decay_c[h]).astype(BF16) for h in heads]
        q_dec = [(q[h] * eg[h]).astype(BF16) for h in heads]
        kd_t = [(k[h] * jnp.exp(gl_col[h] - g_col[h])).T.astype(BF16) for h in heads]
        gl = [jnp.exp(gl_col[h]) for h in heads]
        st = [st_ref[h] for h in heads]
        outs = [[] for _ in heads]
        for c in range(n_chunk):
            lo = c * GDN_CHUNK
            hi = lo + GDN_CHUNK
            st_bf = [st[h].astype(BF16) for h in heads]
            v_new = [w_val[h][lo:hi] - _dot(k_cum[h][lo:hi], st_bf[h]) for h in heads]
            v_pad = []
            for h in heads:
                pieces = [jnp.zeros((GDN_CHUNK, GDN_DV), F32)] * n_chunk
                pieces[c] = v_new[h]
                v_pad.append(jnp.concatenate(pieces, axis=0).astype(BF16))
            for h in heads:
                outs[h].append(_dot(q_dec[h][lo:hi], st_bf[h]) + _dot(attn[h][lo:hi], v_pad[h]))
            st = [st[h] * gl[h][lo:lo + 1, :] + _dot(kd_t[h], v_pad[h]) for h in heads]
        for h in heads:
            st_ref[h] = st[h]
            o = jnp.concatenate(outs[h], axis=0)
            o = o * lax.rsqrt(jnp.mean(o * o, axis=-1, keepdims=True) + NORM_EPS) * nw_ref[...]
            z = proj_ref[slot, :, pcols(COL_ZA, h)]
            oa_ref[pl.ds(r0, SUPER), hcols[h]] = (o * _silu(z)).astype(oa_ref.dtype)

    def ret_heads(r0, slot):
        rc = rc_ref[pl.ds(r0, SUPER), :]
        rs = rs_ref[pl.ds(r0, SUPER), :]
        lg = [hp_ref[2, h] for h in heads]
        qm, km, vb = [], [], []
        for h in heads:
            x = proj_ref[slot, :, pcols(COL_QKB, h)]
            swapped = jnp.where(pair_lo, pltpu.roll(x, LANES - RET_DK // 2, 1), pltpu.roll(x, RET_DK // 2, 1))
            xr = x * rc + swapped * rs
            qm.append(jnp.where(half, xr, 0.0))
            km.append(jnp.where(half, pltpu.roll(xr, LANES - RET_DK, 1), 0.0))
            vb.append(proj_ref[slot, :, pcols(COL_VB, h)].astype(BF16))
        inner = [(_dot_nt(qm[h].astype(BF16), km[h].astype(BF16))
                  * jnp.where(lower, jnp.exp(lg[h] * dpos), 0.0)).astype(BF16) for h in heads]
        st2 = [st2_ref[h] for h in heads]
        ob = [_dot(inner[h], vb[h])
              + _dot((qm[h] * jnp.exp(lg[h] * (pos + 1.0))).astype(BF16), st2[h].astype(BF16)) for h in heads]
        for h in heads:
            g_chunk = jnp.exp(jnp.full((1, 1), SUPER, F32) * lg[h])
            st2_ref[h] = st2[h] * g_chunk + _dot((km[h] * jnp.exp(lg[h] * (SUPER - 1.0 - pos))).T.astype(BF16), vb[h])
        for h in heads:
            mu = jnp.mean(ob[h], axis=-1, keepdims=True)
            oc = ob[h] - mu
            oc = oc * lax.rsqrt(jnp.mean(oc * oc, axis=-1, keepdims=True) + NORM_EPS)
            gate = proj_ref[slot, :, pcols(COL_GB, h)]
            ob_ref[pl.ds(r0, SUPER), hcols[h]] = (oc * _silu(gate)).astype(ob_ref.dtype)

    for s in range(n_super):
        r0 = s * SUPER
        slot = s % 2
        tail = proj_ref[1 - slot, SUPER - SUBLANES:SUPER, 0:COL_ZA * LANES]
        nxt = x_ref[r0 + SUPER:r0 + 2 * SUPER, :] if s + 1 < n_super else xn_ref[...]
        project(nxt, 1 - slot)
        gdn_heads(r0, s, slot, tail)
        ret_heads(r0, slot)


def _mixer(x2, w_bf, hp, conv_w, norm_w, rot_c, rot_s, batch, seq):
    t, d = x2.shape
    ts = MIX_TS if seq % MIX_TS == 0 else seq
    nj = seq // ts
    per_blk = ts // SUPER
    assert per_blk % 2 == 0
    last_chunk = t // SUPER - 1
    const2 = lambda shape: pl.BlockSpec(shape, lambda b, j: (0, 0))
    tab = pl.BlockSpec((ts, LANES), lambda b, j: (j, 0))
    in_specs = [
        pl.BlockSpec(memory_space=pltpu.SMEM),
        pl.BlockSpec((ts, d), lambda b, j: (b * nj + j, 0)),
        pl.BlockSpec((SUPER, d), lambda b, j: (jnp.minimum((b * nj + j + 1) * per_blk, last_chunk), 0)),
        const2(w_bf.shape), const2(conv_w.shape), const2((1, GDN_DV)), tab, tab,
    ]
    out_spec = pl.BlockSpec((ts, HEAD_W), lambda b, j: (b * nj + j, 0))
    return pl.pallas_call(
        _mixer_kernel,
        grid=(batch, nj),
        in_specs=in_specs,
        out_specs=[out_spec, out_spec],
        out_shape=[jax.ShapeDtypeStruct((t, GDN_VW), BF16), jax.ShapeDtypeStruct((t, RET_VW), BF16)],
        scratch_shapes=[pltpu.VMEM((2, SUPER, PROJ_W), F32),
                        pltpu.VMEM((GDN_HEADS, GDN_DK, GDN_DV), F32),
                        pltpu.VMEM((RET_HEADS, LANES, RET_DV), F32)],
        compiler_params=_params(("arbitrary", "arbitrary")),
        name="mixer",
    )(hp, x2, x2, w_bf, conv_w, norm_w, rot_c, rot_s)


def _layer_norm(u, g, b):
    mu = jnp.mean(u, axis=-1, keepdims=True)
    uc = u - mu
    var = jnp.mean(uc * uc, axis=-1, keepdims=True)
    return uc * lax.rsqrt(var + LN_EPS) * g + b


def _post_mix_kernel(x_ref, oa_ref, ob_ref, wo_ref, g_ref, b_ref, wr_ref, rb_ref,
                     h_ref, hp_ref, idx_ref, wts_ref, cnt_ref):
    n_exp = wr_ref.shape[0]
    tm = x_ref.shape[0]
    per_grp = n_exp // N_GROUPS
    mix = _dot(oa_ref[...], wo_ref[:GDN_VW, :]) + _dot(ob_ref[...], wo_ref[GDN_VW:, :])
    hh = _layer_norm(DN_ALPHA * x_ref[...] + mix, g_ref[...], b_ref[...])
    _store_token_tiles(h_ref, hh)
    _store_packed(hp_ref, hh)
    scores = jax.nn.sigmoid(_dot_nt(wr_ref[...], hh.astype(BF16)))
    choice = scores + rb_ref[...]
    big = float(n_exp)
    io_g = _iota((per_grp, tm), 0).astype(F32)
    grp = []
    for g in range(N_GROUPS):
        blk = choice[g * per_grp:(g + 1) * per_grp, :]
        m1 = jnp.max(blk, axis=0, keepdims=True)
        i1 = jnp.min(jnp.where(blk == m1, io_g, big), axis=0, keepdims=True)
        m2 = jnp.max(jnp.where(io_g == i1, NEG_INF, blk), axis=0, keepdims=True)
        grp.append(m1 + m2)
    gsc = jnp.concatenate(grp, axis=0)
    io8 = _iota((N_GROUPS, tm), 0).astype(F32)
    gsel = jnp.zeros((N_GROUPS, tm), F32)
    for _ in range(TOPK_GROUPS):
        m = jnp.max(gsc, axis=0, keepdims=True)
        i = jnp.min(jnp.where(gsc == m, io8, big), axis=0, keepdims=True)
        hit = io8 == i
        gsel = jnp.where(hit, 1.0, gsel)
        gsc = jnp.where(hit, NEG_INF, gsc)
    masked = jnp.concatenate(
        [jnp.where(gsel[g:g + 1, :] > 0.0, choice[g * per_grp:(g + 1) * per_grp, :], NEG_INF)
         for g in range(N_GROUPS)], axis=0)
    io_e = _iota((n_exp, tm), 0).astype(F32)
    allowed = masked
    ids, ws = [], []
    for _ in range(TOP_K):
        m = jnp.max(masked, axis=0, keepdims=True)
        i = jnp.min(jnp.where(masked == m, io_e, big), axis=0, keepdims=True)
        hit = io_e == i
        ws.append(jnp.sum(jnp.where(hit, scores, 0.0), axis=0, keepdims=True))
        ids.append(i)
        masked = jnp.where(hit, NEG_INF, masked)
    sel = jnp.where(masked == NEG_INF, jnp.where(allowed == NEG_INF, 0.0, 1.0), 0.0)
    w = jnp.concatenate(ws, axis=0)
    wts_ref[...] = w / jnp.sum(w, axis=0, keepdims=True) * ROUTED_SCALE
    idx_ref[...] = jnp.concatenate(ids, axis=0).astype(jnp.int32)

    @pl.when(pl.program_id(0) == 0)
    def _():
        cnt_ref[...] = jnp.zeros_like(cnt_ref)

    cnt_ref[...] += jnp.sum(sel, axis=1, keepdims=True)


def _post_mix(x2, oa, ob, wo_bf, g1, b1, wr_t, rbias, tm):
    t, d = x2.shape
    n_exp = wr_t.shape[0]
    row = lambda w: pl.BlockSpec((tm, w), lambda i: (i, 0))
    const = lambda shape: pl.BlockSpec(shape, lambda i: (0, 0))
    tok = pl.BlockSpec((TOP_K, tm), lambda i: (0, i))
    return pl.pallas_call(
        _post_mix_kernel,
        grid=(t // tm,),
        in_specs=[row(d), row(GDN_VW), row(RET_VW), const(wo_bf.shape), const((1, d)), const((1, d)),
                  const(wr_t.shape), const((n_exp, 1))],
        out_specs=[pl.BlockSpec((tm * SUBLANES, LANES), lambda i: (i, 0)),
                   pl.BlockSpec((tm * PACK_ROWS, LANES), lambda i: (i, 0)), tok, tok, const((n_exp, 1))],
        out_shape=[jax.ShapeDtypeStruct((t * SUBLANES, LANES), F32),
                   jax.ShapeDtypeStruct((t * PACK_ROWS, LANES), jnp.uint32),
                   jax.ShapeDtypeStruct((TOP_K, t), jnp.int32),
                   jax.ShapeDtypeStruct((TOP_K, t), F32),
                   jax.ShapeDtypeStruct((n_exp, 1), F32)],
        compiler_params=_params(("arbitrary",)),
        name="post_mix",
    )(x2, oa, ob, wo_bf, g1, b1, wr_t, rbias)


def _dest_kernel(idx_ref, base_ref, dest_ref, carry_ref):
    n_exp = base_ref.shape[0]
    tm = idx_ref.shape[1]

    @pl.when(pl.program_id(0) == 0)
    def _():
        carry_ref[...] = jnp.zeros_like(carry_ref)

    io_e = _iota((n_exp, tm), 0)
    idx = idx_ref[...]
    hits = [io_e == idx[k:k + 1, :] for k in range(TOP_K)]
    sel = jnp.zeros((n_exp, tm), F32)
    for hit in hits:
        sel = jnp.where(hit, 1.0, sel)
    before = jnp.where(_iota((tm, tm), 0) < _iota((tm, tm), 1), 1.0, 0.0).astype(BF16)
    rank = _dot(sel.astype(BF16), before)
    tot = base_ref[...] + carry_ref[...] + rank
    dest = [jnp.sum(jnp.where(hit, tot, 0.0), axis=0, keepdims=True) for hit in hits]
    dest_ref[...] = jnp.concatenate(dest, axis=0).astype(jnp.int32) * PACK_ROWS
    carry_ref[...] += jnp.sum(sel, axis=1, keepdims=True)


def _dest(idx_t, base, tm):
    t = idx_t.shape[1]
    n_exp = base.shape[0]
    tok = pl.BlockSpec((TOP_K, tm), lambda i: (0, i))
    return pl.pallas_call(
        _dest_kernel,
        grid=(t // tm,),
        in_specs=[tok, pl.BlockSpec((n_exp, 1), lambda i: (0, 0))],
        out_specs=tok,
        out_shape=jax.ShapeDtypeStruct((TOP_K, t), jnp.int32),
        scratch_shapes=[pltpu.VMEM((n_exp, 1), F32)],
        compiler_params=_params(("arbitrary",)),
        name="dest",
    )(idx_t, base)


def _dispatch_kernel(dest_ref, h_ref, xs_ref, zero_ref, sem, zsem):
    tm = h_ref.shape[0] // PACK_ROWS
    last = pl.num_programs(0) - 1
    pad_row = xs_ref.shape[0] - zero_ref.shape[0]

    @pl.when(pl.program_id(0) == last)
    def _():
        zero_ref[...] = jnp.zeros_like(zero_ref)
        _tile_copy(zero_ref, 0, xs_ref, pad_row, zsem, EXPERT_CHUNK, PACK_ROWS).start()

    def start(t, c):
        for k in range(TOP_K):
            _tile_copy(h_ref, t * PACK_ROWS, xs_ref, dest_ref[t * TOP_K + k], sem, 1, PACK_ROWS).start(priority=k % 2)
        return c

    lax.fori_loop(0, tm, start, 0)

    for k in range(TOP_K):
        _tile_copy(h_ref, 0, xs_ref, 0, sem, tm, PACK_ROWS).wait()

    @pl.when(pl.program_id(0) == last)
    def _():
        _tile_copy(zero_ref, 0, xs_ref, pad_row, zsem, EXPERT_CHUNK, PACK_ROWS).wait()


def _dispatch(dest_t, h_pk, tm):
    t = dest_t.shape[0] // TOP_K
    n_rows = (t * TOP_K + EXPERT_CHUNK) * PACK_ROWS
    return pl.pallas_call(
        _dispatch_kernel,
        grid=(t // tm,),
        in_specs=[pl.BlockSpec((tm * TOP_K,), lambda i: (i,), memory_space=pltpu.SMEM),
                  pl.BlockSpec((tm * PACK_ROWS, LANES), lambda i: (i, 0))],
        out_specs=pl.BlockSpec(memory_space=pl.ANY),
        out_shape=jax.ShapeDtypeStruct((n_rows, LANES), jnp.uint32),
        scratch_shapes=[pltpu.VMEM((EXPERT_CHUNK * PACK_ROWS, LANES), jnp.uint32),
                        pltpu.SemaphoreType.DMA(()), pltpu.SemaphoreType.DMA(())],
        compiler_params=_params(("arbitrary",)),
        name="dispatch",
    )(dest_t, h_pk)


def _chunk_pieces(n):
    pieces = [(j < n // EXPERT_PIECE, j * EXPERT_PIECE, EXPERT_PIECE) for j in range(EXPERT_CHUNK // EXPERT_PIECE)]
    size = EXPERT_PIECE // 2
    while size >= 1:
        shift = int(math.log2(size)) + 1
        pieces.append(((n & size) != 0, (n >> shift) << shift, size))
        size //= 2
    return pieces


ST_CHUNKS = 0
ST_ISSUED = 1
ST_CUR_E = 2
ST_CUR_I = 3
ST_WRITE = 4
N_XBUF = 4
N_OBUF = 3


def _expert_kernel(start_ref, count_ref, wg_ref, wu_ref, wd_ref, xs_ref, eo_ref,
                   xbuf, obuf, st_ref, rsem, wsem):
    e = pl.program_id(0)
    n_exp = pl.num_programs(0)
    ch = EXPERT_CHUNK
    s0 = start_ref[e]
    cnt = count_ref[e]
    n_ch = (cnt + ch - 1) // ch

    def read_parts(first_slot, buf):
        return [_tile_copy(xs_ref, (first_slot + j * EXPERT_PIECE) * PACK_ROWS, xbuf.at[buf],
                           j * EXPERT_PIECE * PACK_ROWS, rsem.at[buf], EXPERT_PIECE, PACK_ROWS)
                for j in range(ch // EXPERT_PIECE)]

    def read_next():
        def exhausted(c):
            cnt_c = count_ref[jnp.minimum(c[0], n_exp - 1)]
            return jnp.logical_and(c[0] < n_exp, c[1] * ch >= cnt_c)

        ce, ci = lax.while_loop(exhausted, lambda c: (c[0] + 1, jnp.int32(0)),
                                (st_ref[ST_CUR_E], st_ref[ST_CUR_I]))

        @pl.when(ce < n_exp)
        def _():
            first_slot = start_ref[jnp.minimum(ce, n_exp - 1)] + ci * ch
            buf = lax.rem(st_ref[ST_ISSUED], N_XBUF)
            for j, cp in enumerate(read_parts(first_slot, buf)):
                cp.start(priority=j % 2)
            st_ref[ST_ISSUED] = st_ref[ST_ISSUED] + 1

        st_ref[ST_CUR_E] = ce
        st_ref[ST_CUR_I] = ci + 1

    def wait_read(buf):
        for cp in read_parts(0, buf):
            cp.wait()

    def out_pieces(buf, first_slot, n, wait):
        for j, (pred, off, size) in enumerate(_chunk_pieces(n)):
            @pl.when(pred)
            def _(j=j, off=off, size=size):
                cp = _tile_copy(obuf.at[buf], off * PACK_ROWS, eo_ref, (first_slot + off) * PACK_ROWS,
                                wsem.at[buf], size, PACK_ROWS)
                cp.wait() if wait else cp.start(priority=j % 2)

    def drain(buf):
        out_pieces(buf, st_ref[ST_WRITE + 2 * buf], st_ref[ST_WRITE + 2 * buf + 1], True)
        st_ref[ST_WRITE + 2 * buf + 1] = 0

    @pl.when(e == 0)
    def _():
        for i in range(st_ref.shape[0]):
            st_ref[i] = 0
        for _ in range(N_XBUF - 1):
            read_next()

    def body(i, c):
        g = st_ref[ST_CHUNKS]
        buf = lax.rem(g, N_XBUF)
        wait_read(buf)
        read_next()
        obf = lax.rem(g, N_OBUF)
        drain(obf)
        first_slot = s0 + i * ch
        n = jnp.minimum(ch, cnt - i * ch)

        def swiglu(rows):
            xb = _load_packed(xbuf, rows, lead=(buf,), dtype=BF16)
            hid = _silu(_dot(xb, wg_ref[0].astype(BF16))) * _dot(xb, wu_ref[0].astype(BF16))
            _store_packed(obuf, _dot(hid.astype(BF16), wd_ref[0].astype(BF16)), lead=(obf,))

        pl.when(n > ch // 2)(lambda: swiglu(ch))
        pl.when(n <= ch // 2)(lambda: swiglu(ch // 2))
        out_pieces(obf, first_slot, n, False)
        st_ref[ST_WRITE + 2 * obf] = first_slot
        st_ref[ST_WRITE + 2 * obf + 1] = n
        st_ref[ST_CHUNKS] = g + 1
        return c

    lax.fori_loop(0, n_ch, body, 0)

    @pl.when(e == n_exp - 1)
    def _():
        for b in range(N_OBUF):
            drain(b)


def _experts(starts, counts, xs, wg, wu, wd):
    n_exp, d, d_e = wg.shape
    n_slot = xs.shape[0] // PACK_ROWS - EXPERT_CHUNK
    rows = EXPERT_CHUNK * PACK_ROWS
    grid_spec = pltpu.PrefetchScalarGridSpec(
        num_scalar_prefetch=2,
        grid=(n_exp,),
        in_specs=[pl.BlockSpec((1, d, d_e), lambda e, st, ct: (e, 0, 0)),
                  pl.BlockSpec((1, d, d_e), lambda e, st, ct: (e, 0, 0)),
                  pl.BlockSpec((1, d_e, d), lambda e, st, ct: (e, 0, 0)),
                  pl.BlockSpec(memory_space=pl.ANY)],
        out_specs=pl.BlockSpec(memory_space=pl.ANY),
        scratch_shapes=[pltpu.VMEM((N_XBUF, rows, LANES), jnp.uint32), pltpu.VMEM((N_OBUF, rows, LANES), jnp.uint32),
                        pltpu.SMEM((ST_WRITE + 2 * N_OBUF,), jnp.int32),
                        pltpu.SemaphoreType.DMA((N_XBUF,)), pltpu.SemaphoreType.DMA((N_OBUF,))],
    )
    return pl.pallas_call(
        _expert_kernel,
        grid_spec=grid_spec,
        out_shape=jax.ShapeDtypeStruct((n_slot * PACK_ROWS, LANES), jnp.uint32),
        compiler_params=_params(("arbitrary",)),
        name="experts",
    )(starts, counts, wg, wu, wd, xs)


def _combine_kernel(dest_ref, dnext_ref, wts_ref, h_ref, eo_ref, wgs_ref, wus_ref, wds_ref, g_ref, b_ref,
                    out_ref, gbuf, sem):
    i = pl.program_id(0)
    tm = out_ref.shape[0]
    buf = lax.rem(i, 2)

    def gather(d_ref, bf):
        def start(t, c):
            for k in range(TOP_K):
                _tile_copy(eo_ref, d_ref[t * TOP_K + k], gbuf.at[bf, k], t * PACK_ROWS, sem.at[bf],
                           1, PACK_ROWS).start(priority=k % 2)
            return c

        lax.fori_loop(0, tm, start, 0)

    @pl.when(i == 0)
    def _():
        gather(dest_ref, 0)

    def step(bf):
        for t in range(tm):
            for k in range(TOP_K):
                _tile_copy(eo_ref, dnext_ref[t * TOP_K + k], gbuf.at[1 - bf, k], t * PACK_ROWS, sem.at[1 - bf],
                           1, PACK_ROWS).start(priority=k % 2)
        hh = _load_token_tiles(h_ref, tm)
        hb = hh.astype(BF16)
        hid = _silu(_dot(hb, wgs_ref[...])) * _dot(hb, wus_ref[...])
        acc = DN_ALPHA * hh + _dot(hid.astype(BF16), wds_ref[...])
        w_t = jnp.concatenate([wts_ref[...], jnp.zeros((LANES - TOP_K, tm), F32)], axis=0).T
        for k in range(TOP_K):
            _tile_copy(eo_ref, 0, gbuf.at[bf, k], 0, sem.at[bf], tm, PACK_ROWS).wait()
        for k in range(TOP_K):
            acc = acc + _load_packed(gbuf, tm, lead=(bf, k)) * w_t[:, k:k + 1]
        out_ref[...] = _layer_norm(acc, g_ref[...], b_ref[...])

        @pl.when(i == pl.num_programs(0) - 1)
        def _():
            for k in range(TOP_K):
                _tile_copy(eo_ref, 0, gbuf.at[1 - bf, k], 0, sem.at[1 - bf], tm, PACK_ROWS).wait()

    for bf in range(2):
        pl.when(buf == bf)(lambda bf=bf: step(bf))


def _combine(dest_t, wts_t, h_tt, eo, wgs, wus, wds, g2, b2, tm):
    t = dest_t.shape[0] // TOP_K
    d = wgs.shape[0]
    const = lambda shape: pl.BlockSpec(shape, lambda i: (0, 0))
    n_tiles = t // tm
    return pl.pallas_call(
        _combine_kernel,
        grid=(n_tiles,),
        in_specs=[pl.BlockSpec((tm * TOP_K,), lambda i: (i,), memory_space=pltpu.SMEM),
                  pl.BlockSpec((tm * TOP_K,), lambda i: (jnp.minimum(i + 1, n_tiles - 1),), memory_space=pltpu.SMEM),
                  pl.BlockSpec((TOP_K, tm), lambda i: (0, i)),
                  pl.BlockSpec((tm * SUBLANES, LANES), lambda i: (i, 0)),
                  pl.BlockSpec(memory_space=pl.ANY),
                  const(wgs.shape), const(wus.shape), const(wds.shape), const((1, d)), const((1, d))],
        out_specs=pl.BlockSpec((tm, d), lambda i: (i, 0)),
        out_shape=jax.ShapeDtypeStruct((t, d), F32),
        scratch_shapes=[pltpu.VMEM((2, TOP_K, tm * PACK_ROWS, LANES), jnp.uint32), pltpu.SemaphoreType.DMA((2,))],
        compiler_params=_params(("arbitrary",)),
        name="combine",
    )(dest_t, dest_t, wts_t, h_tt, eo, wgs, wus, wds, g2, b2)


def _regroup_w_in(w):
    d = w.shape[0]
    splits = np.cumsum([GDN_QK, GDN_QK, GDN_VW, GDN_VW, GDN_HEADS, GDN_HEADS, RET_QK, RET_QK, RET_VW])
    qa, ka, va, za, ba, aa, qb, kb, vb, gb = jnp.split(w, splits.tolist(), axis=1)
    ba_blk = jnp.concatenate([ba, aa, jnp.zeros((d, LANES - 2 * GDN_HEADS), w.dtype)], axis=1)
    qb = qb.reshape(d, RET_HEADS, RET_DK // 2, 2)
    kb = kb.reshape(d, RET_HEADS, RET_DK // 2, 2)
    qkb = jnp.concatenate([qb[..., 0], qb[..., 1], kb[..., 0], kb[..., 1]], axis=-1).reshape(d, RET_HEADS * LANES)
    return jnp.concatenate([qa, ka, va, za, qkb, vb, gb, ba_blk], axis=1).astype(BF16)


def _rotary_tables(seq):
    inv = 1.0 / (ROPE_BASE ** jnp.linspace(0.0, 1.0, RET_DK // 2, dtype=F32))
    ang = jnp.arange(seq, dtype=F32)[:, None] * inv[None, :]
    c, s = jnp.cos(ang), jnp.sin(ang)
    ks = RET_DK ** -0.5
    rot_c = jnp.concatenate([c, c, c * ks, c * ks], axis=1)
    rot_s = jnp.concatenate([-s, s, -s * ks, s * ks], axis=1)
    return rot_c, rot_s


def _tile(n, pref):
    return pref if n % pref == 0 else n


def kernel(x, w_in, gdn_conv_w, gdn_a_log, gdn_dt_bias, gdn_norm_w, w_out, ln1_g, ln1_b, w_router, router_bias,
           w_gate_e, w_up_e, w_down_e, w_gate_s, w_up_s, w_down_s, ln2_g, ln2_b):
    batch, seq, d = x.shape
    t = batch * seq
    n_exp = w_router.shape[-1]
    hcur = x.reshape(t, d)
    for l in range(DEPTH):
        w_bf = _regroup_w_in(w_in[l])
        log_gamma = jnp.log(1.0 - 2.0 ** (-5.0 - jnp.arange(RET_HEADS, dtype=F32)))
        hp = jnp.stack([-jnp.exp(gdn_a_log[l].astype(F32)), gdn_dt_bias[l].astype(F32), log_gamma])
        rot_c, rot_s = _rotary_tables(seq)
        oa, ob = _mixer(hcur, w_bf, hp, gdn_conv_w[l].astype(F32), gdn_norm_w[l].reshape(1, -1).astype(F32),
                        rot_c, rot_s, batch, seq)
        hh, h_pk, idx_t, wts_t, counts = _post_mix(
            hcur, oa, ob, w_out[l].astype(BF16), ln1_g[l].reshape(1, d), ln1_b[l].reshape(1, d),
            w_router[l].T.astype(BF16), router_bias[l].reshape(n_exp, 1).astype(F32), _tile(t, POST_MIX_TM))
        base = jnp.cumsum(counts, axis=0) - counts
        dest_t = _dest(idx_t, base, _tile(t, DEST_TM)).T.reshape(t * TOP_K)
        xs = _dispatch(dest_t, h_pk, _tile(t, DISPATCH_TM))
        eo = _experts(base.reshape(n_exp).astype(jnp.int32), counts.reshape(n_exp).astype(jnp.int32),
                      xs, w_gate_e[l], w_up_e[l], w_down_e[l])
        hcur = _combine(dest_t, wts_t, hh, eo, w_gate_s[l].astype(BF16), w_up_s[l].astype(BF16),
                        w_down_s[l].astype(BF16), ln2_g[l].reshape(1, d), ln2_b[l].reshape(1, d), _tile(t, COMBINE_TM))
    return hcur.reshape(batch, seq, d)
```

```python
import math

import numpy as np
import jax
import jax.numpy as jnp
from jax import lax
from jax.experimental import pallas as pl
from jax.experimental.pallas import tpu as pltpu

GDN_HEADS = 4
GDN_DK = 128
GDN_DV = 128
GDN_CONV = 4
GDN_CHUNK = 64
RET_HEADS = 4
RET_DK = 64
RET_DV = 128
ROPE_BASE = 10000.0
N_GROUPS = 8
TOPK_GROUPS = 4
TOP_K = 8
ROUTED_SCALE = 2.5
DEPTH = 1
DN_ALPHA = (2.0 * DEPTH) ** 0.25
LN_EPS = 1e-5
NORM_EPS = 1e-6

GDN_QK = GDN_HEADS * GDN_DK
GDN_VW = GDN_HEADS * GDN_DV
RET_QK = RET_HEADS * RET_DK
RET_VW = RET_HEADS * RET_DV

LANES = 128
SUBLANES = 8
VMEM_LIMIT_BYTES = 56 * 1024 * 1024

SUPER = 256
NEG_INF = float("-inf")

COL_QA = 0
COL_KA = COL_QA + GDN_HEADS
COL_VA = COL_KA + GDN_HEADS
COL_ZA = COL_VA + GDN_HEADS
COL_QKB = COL_ZA + GDN_HEADS
COL_VB = COL_QKB + RET_HEADS
COL_GB = COL_VB + RET_HEADS
COL_BA = COL_GB + RET_HEADS
N_COLBLK = COL_BA + 1
PROJ_W = N_COLBLK * LANES
HEAD_W = GDN_HEADS * LANES
MIX_TS = 512
POST_MIX_TM = 512
DEST_TM = 512
DISPATCH_TM = 1024
COMBINE_TM = 256
EXPERT_CHUNK = 512
EXPERT_PIECE = 64
PACK_ROWS = 4

BF16 = jnp.bfloat16
F32 = jnp.float32


def _dot(a, b):
    return jnp.dot(a, b, preferred_element_type=F32)


def _dot_nt(a, b):
    return lax.dot_general(a, b, (((1,), (1,)), ((), ())), preferred_element_type=F32)


def _silu(x):
    return x * jax.nn.sigmoid(x)


def _softplus(x):
    return jnp.maximum(x, 0.0) + jnp.log1p(jnp.exp(-jnp.abs(x)))


def _iota(shape, dim, dtype=jnp.int32):
    return lax.broadcasted_iota(dtype, shape, dim)


def _params(sem):
    return pltpu.CompilerParams(dimension_semantics=sem, vmem_limit_bytes=VMEM_LIMIT_BYTES)


def _load_token_tiles(ref, n, lead=()):
    parts = [ref[lead + (pl.ds(s, n, stride=SUBLANES), slice(None))] for s in range(SUBLANES)]
    return jnp.concatenate(parts, axis=1)


def _store_token_tiles(ref, val, lead=()):
    n = val.shape[0]
    for s in range(SUBLANES):
        ref[lead + (pl.ds(s, n, stride=SUBLANES), slice(None))] = val[:, s * LANES:(s + 1) * LANES]


def _tile_copy(src_ref, src_row, dst_ref, dst_row, sem, n_slots=1, slot_rows=SUBLANES):
    rows = n_slots * slot_rows
    aligned = lambda r: r if isinstance(r, int) else pl.multiple_of(r, slot_rows)
    return pltpu.make_async_copy(src_ref.at[pl.ds(aligned(src_row), rows)],
                                 dst_ref.at[pl.ds(aligned(dst_row), rows)], sem)


def _store_packed(ref, val, lead=()):
    n, width = val.shape
    bits = pltpu.bitcast(val.astype(BF16).astype(F32), jnp.uint32)
    words = (bits[:, width // 2:] & jnp.uint32(0xFFFF0000)) | (bits[:, :width // 2] >> 16)
    for s in range(PACK_ROWS):
        ref[lead + (pl.ds(s, n, stride=PACK_ROWS), slice(None))] = words[:, s * LANES:(s + 1) * LANES]


def _load_packed(ref, n, lead=(), dtype=F32):
    words = [ref[lead + (pl.ds(s, n, stride=PACK_ROWS), slice(None))] for s in range(PACK_ROWS)]
    lo = [pltpu.bitcast(w << 16, F32) for w in words]
    hi = [pltpu.bitcast(w & jnp.uint32(0xFFFF0000), F32) for w in words]
    return jnp.concatenate(lo + hi, axis=1).astype(dtype)


def _mixer_kernel(hp_ref,
                  x_ref, xn_ref,
                  w_ref, cw_ref, nw_ref, rc_ref, rs_ref,
                  oa_ref, ob_ref, proj_ref, st_ref, st2_ref):
    j = pl.program_id(1)
    ts = x_ref.shape[0]
    n_super = ts // SUPER
    n_chunk = SUPER // GDN_CHUNK

    def project(rows, slot):
        xb = rows.astype(BF16)
        step = 4 * LANES
        for c in range(0, PROJ_W, step):
            n = min(step, PROJ_W - c)
            proj_ref[slot, :, c:c + n] = _dot(xb, w_ref[:, c:c + n])

    @pl.when(j == 0)
    def _():
        st_ref[...] = jnp.zeros_like(st_ref)
        st2_ref[...] = jnp.zeros_like(st2_ref)

    @pl.when((j == 0) & (pl.program_id(0) == 0))
    def _():
        proj_ref[1] = jnp.zeros(proj_ref.shape[1:], F32)
        project(x_ref[0:SUPER, :], 0)

    ri = _iota((SUPER, SUPER), 0)
    ci = _iota((SUPER, SUPER), 1)
    shift = int(math.log2(GDN_CHUNK))
    same = (ri >> shift) == (ci >> shift)
    causal = same & (ci <= ri)
    strict = same & (ci < ri)
    causal_t = same & (ri <= ci)
    eye = jnp.where(ri == ci, 1.0, 0.0)
    lane = _iota((SUPER, LANES), 1)
    dpos = jnp.maximum((ri - ci).astype(F32), 0.0)
    lower = ri >= ci
    pos = _iota((SUPER, 1), 0).astype(F32)
    half = lane < RET_DK
    pair_lo = (lane % RET_DK) < (RET_DK // 2)
    lane1 = _iota((1, LANES), 1)
    neg_a_lane = jnp.zeros((1, LANES), F32)
    dt_b_lane = jnp.zeros((1, LANES), F32)
    for h in range(GDN_HEADS):
        neg_a_lane = jnp.where(lane1 == h + GDN_HEADS, hp_ref[0, h], neg_a_lane)
        dt_b_lane = jnp.where(lane1 == h + GDN_HEADS, hp_ref[1, h], dt_b_lane)

    def pcols(col_blk, h):
        return slice((col_blk + h) * LANES, (col_blk + h + 1) * LANES)

    def conv_silu(col_blk, h, slot, s, tail):
        cur = proj_ref[slot, :, pcols(col_blk, h)]
        prev = jnp.where((s == 0) & (j == 0), 0.0, tail[:, pcols(col_blk, h)])
        ext = jnp.concatenate([prev, cur], axis=0)
        acc = jnp.zeros((SUPER, LANES), F32)
        for jj in range(GDN_CONV):
            off = SUBLANES - (GDN_CONV - 1) + jj
            acc = acc + cw_ref[jj:jj + 1, pcols(col_blk, h)] * ext[off:off + SUPER, :]
        return _silu(acc)

    def l2n(u):
        return u * lax.rsqrt(jnp.sum(u * u, axis=-1, keepdims=True) + NORM_EPS)

    heads = range(GDN_HEADS)
    hcols = [slice(h * LANES, (h + 1) * LANES) for h in heads]

    def gdn_heads(r0, s, slot, tail):
        q = [l2n(conv_silu(COL_QA, h, slot, s, tail)) * (GDN_DK ** -0.5) for h in heads]
        k = [l2n(conv_silu(COL_KA, h, slot, s, tail)) for h in heads]
        v = [conv_silu(COL_VA, h, slot, s, tail) for h in heads]
        ba = proj_ref[slot, :, pcols(COL_BA, 0)]
        gates = jnp.where(lane < GDN_HEADS, jax.nn.sigmoid(ba), neg_a_lane * _softplus(ba + dt_b_lane))
        gates_t = gates.T
        beta, g_col, gl_col, decay_c, decay_s = [], [], [], [], []
        for h in heads:
            beta.append(jnp.sum(jnp.where(lane == h, gates, 0.0), axis=1, keepdims=True))
            la_col = jnp.sum(jnp.where(lane == h + GDN_HEADS, gates, 0.0), axis=1, keepdims=True)
            la_row = gates_t[h + GDN_HEADS:h + GDN_HEADS + 1, :]
            gc = jnp.sum(jnp.where(causal, la_row, 0.0), axis=1, keepdims=True)
            gr = jnp.sum(jnp.where(causal_t, la_col, 0.0), axis=0, keepdims=True)
            g_col.append(gc)
            gl_col.append(jnp.sum(jnp.where(same, la_row, 0.0), axis=1, keepdims=True))
            dc = jnp.exp(jnp.where(causal, gc - gr, NEG_INF))
            decay_c.append(dc)
            decay_s.append(jnp.where(strict, dc, 0.0))
        kb = [k[h] * beta[h] for h in heads]
        k_bf = [k[h].astype(BF16) for h in heads]
        low = [_dot_nt(kb[h].astype(BF16), k_bf[h]) * decay_s[h] for h in heads]
        p = [eye - low[h] for h in heads]
        sq = low
        for _ in range(shift - 1):
            sq_bf = [sq[h].astype(BF16) for h in heads]
            sq = [_dot(sq_bf[h], sq_bf[h]) for h in heads]
            p = [p[h] + _dot(p[h].astype(BF16), sq[h].astype(BF16)) for h in heads]
        eg = [jnp.exp(g_col[h]) for h in heads]
        rhs = [jnp.concatenate([v[h] * beta[h], kb[h] * eg[h]], axis=1).astype(BF16) for h in heads]
        wk = [_dot(p[h].astype(BF16), rhs[h]) for h in heads]
        w_val = [wk[h][:, :GDN_DV] for h in heads]
        k_cum = [wk[h][:, GDN_DV:].astype(BF16) for h in heads]
        attn = [(_dot_nt(q[h].astype(BF16), k_bf[h]) * decay_c[h]).astype(BF16) for h in heads]
        q_dec = [(q[h] * eg[h]).astype(BF16) for h in heads]
        kd_t = [(k[h] * jnp.exp(gl_col[h] - g_col[h])).T.astype(BF16) for h in heads]
        gl = [jnp.exp(gl_col[h]) for h in heads]
        st = [st_ref[h] for h in heads]
        outs = [[] for _ in heads]
        for c in range(n_chunk):
            lo = c * GDN_CHUNK
            hi = lo + GDN_CHUNK
            st_bf = [st[h].astype(BF16) for h in heads]
            v_new = [w_val[h][lo:hi] - _dot(k_cum[h][lo:hi], st_bf[h]) for h in heads]
            v_pad = []
            for h in heads:
                pieces = [jnp.zeros((GDN_CHUNK, GDN_DV), F32)] * n_chunk
                pieces[c] = v_new[h]
                v_pad.append(jnp.concatenate(pieces, axis=0).astype(BF16))
            for h in heads:
                outs[h].append(_dot(q_dec[h][lo:hi], st_bf[h]) + _dot(attn[h][lo:hi], v_pad[h]))
            st = [st[h] * gl[h][lo:lo + 1, :] + _dot(kd_t[h], v_pad[h]) for h in heads]
        for h in heads:
            st_ref[h] = st[h]
            o = jnp.concatenate(outs[h], axis=0)
            o = o * lax.rsqrt(jnp.mean(o * o, axis=-1, keepdims=True) + NORM_EPS) * nw_ref[...]
            z = proj_ref[slot, :, pcols(COL_ZA, h)]
            oa_ref[pl.ds(r0, SUPER), hcols[h]] = (o * _silu(z)).astype(oa_ref.dtype)

    def ret_heads(r0, slot):
        rc = rc_ref[pl.ds(r0, SUPER), :]
        rs = rs_ref[pl.ds(r0, SUPER), :]
        lg = [hp_ref[2, h] for h in heads]
        qm, km, vb = [], [], []
        for h in heads:
            x = proj_ref[slot, :, pcols(COL_QKB, h)]
            swapped = jnp.where(pair_lo, pltpu.roll(x, LANES - RET_DK // 2, 1), pltpu.roll(x, RET_DK // 2, 1))
            xr = x * rc + swapped * rs
            qm.append(jnp.where(half, xr, 0.0))
            km.append(jnp.where(half, pltpu.roll(xr, LANES - RET_DK, 1), 0.0))
            vb.append(proj_ref[slot, :, pcols(COL_VB, h)].astype(BF16))
        inner = [(_dot_nt(qm[h].astype(BF16), km[h].astype(BF16))
                  * jnp.where(lower, jnp.exp(lg[h] * dpos), 0.0)).astype(BF16) for h in heads]
        st2 = [st2_ref[h] for h in heads]
        ob = [_dot(inner[h], vb[h])
              + _dot((qm[h] * jnp.exp(lg[h] * (pos + 1.0))).astype(BF16), st2[h].astype(BF16)) for h in heads]
        for h in heads:
            g_chunk = jnp.exp(jnp.full((1, 1), SUPER, F32) * lg[h])
            st2_ref[h] = st2[h] * g_chunk + _dot((km[h] * jnp.exp(lg[h] * (SUPER - 1.0 - pos))).T.astype(BF16), vb[h])
        for h in heads:
            mu = jnp.mean(ob[h], axis=-1, keepdims=True)
            oc = ob[h] - mu
            oc = oc * lax.rsqrt(jnp.mean(oc * oc, axis=-1, keepdims=True) + NORM_EPS)
            gate = proj_ref[slot, :, pcols(COL_GB, h)]
            ob_ref[pl.ds(r0, SUPER), hcols[h]] = (oc * _silu(gate)).astype(ob_ref.dtype)

    for s in range(n_super):
        r0 = s * SUPER
        slot = s % 2
        tail = proj_ref[1 - slot, SUPER - SUBLANES:SUPER, 0:COL_ZA * LANES]
        nxt = x_ref[r0 + SUPER:r0 + 2 * SUPER, :] if s + 1 < n_super else xn_ref[...]
        project(nxt, 1 - slot)
        gdn_heads(r0, s, slot, tail)
        ret_heads(r0, slot)


def _mixer(x2, w_bf, hp, conv_w, norm_w, rot_c, rot_s, batch, seq):
    t, d = x2.shape
    ts = MIX_TS if seq % MIX_TS == 0 else seq
    nj = seq // ts
    per_blk = ts // SUPER
    assert per_blk % 2 == 0
    last_chunk = t // SUPER - 1
    const2 = lambda shape: pl.BlockSpec(shape, lambda b, j: (0, 0))
    tab = pl.BlockSpec((ts, LANES), lambda b, j: (j, 0))
    in_specs = [
        pl.BlockSpec(memory_space=pltpu.SMEM),
        pl.BlockSpec((ts, d), lambda b, j: (b * nj + j, 0)),
        pl.BlockSpec((SUPER, d), lambda b, j: (jnp.minimum((b * nj + j + 1) * per_blk, last_chunk), 0)),
        const2(w_bf.shape), const2(conv_w.shape), const2((1, GDN_DV)), tab, tab,
    ]
    out_spec = pl.BlockSpec((ts, HEAD_W), lambda b, j: (b * nj + j, 0))
    return pl.pallas_call(
        _mixer_kernel,
        grid=(batch, nj),
        in_specs=in_specs,
        out_specs=[out_spec, out_spec],
        out_shape=[jax.ShapeDtypeStruct((t, GDN_VW), BF16), jax.ShapeDtypeStruct((t, RET_VW), BF16)],
        scratch_shapes=[pltpu.VMEM((2, SUPER, PROJ_W), F32),
                        pltpu.VMEM((GDN_HEADS, GDN_DK, GDN_DV), F32),
                        pltpu.VMEM((RET_HEADS, LANES, RET_DV), F32)],
        compiler_params=_params(("arbitrary", "arbitrary")),
        name="mixer",
    )(hp, x2, x2, w_bf, conv_w, norm_w, rot_c, rot_s)


def _layer_norm(u, g, b):
    mu = jnp.mean(u, axis=-1, keepdims=True)
    uc = u - mu
    var = jnp.mean(uc * uc, axis=-1, keepdims=True)
    return uc * lax.rsqrt(var + LN_EPS) * g + b


def _post_mix_kernel(x_ref, oa_ref, ob_ref, wo_ref, g_ref, b_ref, wr_ref, rb_ref,
                     h_ref, hp_ref, idx_ref, wts_ref, cnt_ref):
    n_exp = wr_ref.shape[0]
    tm = x_ref.shape[0]
    per_grp = n_exp // N_GROUPS
    mix = _dot(oa_ref[...], wo_ref[:GDN_VW, :]) + _dot(ob_ref[...], wo_ref[GDN_VW:, :])
    hh = _layer_norm(DN_ALPHA * x_ref[...] + mix, g_ref[...], b_ref[...])
    _store_token_tiles(h_ref, hh)
    _store_packed(hp_ref, hh)
    scores = jax.nn.sigmoid(_dot_nt(wr_ref[...], hh.astype(BF16)))
    choice = scores + rb_ref[...]
    big = float(n_exp)
    io_g = _iota((per_grp, tm), 0).astype(F32)
    grp = []
    for g in range(N_GROUPS):
        blk = choice[g * per_grp:(g + 1) * per_grp, :]
        m1 = jnp.max(blk, axis=0, keepdims=True)
        i1 = jnp.min(jnp.where(blk == m1, io_g, big), axis=0, keepdims=True)
        m2 = jnp.max(jnp.where(io_g == i1, NEG_INF, blk), axis=0, keepdims=True)
        grp.append(m1 + m2)
    gsc = jnp.concatenate(grp, axis=0)
    io8 = _iota((N_GROUPS, tm), 0).astype(F32)
    gsel = jnp.zeros((N_GROUPS, tm), F32)
    for _ in range(TOPK_GROUPS):
        m = jnp.max(gsc, axis=0, keepdims=True)
        i = jnp.min(jnp.where(gsc == m, io8, big), axis=0, keepdims=True)
        hit = io8 == i
        gsel = jnp.where(hit, 1.0, gsel)
        gsc = jnp.where(hit, NEG_INF, gsc)
    masked = jnp.concatenate(
        [jnp.where(gsel[g:g + 1, :] > 0.0, choice[g * per_grp:(g + 1) * per_grp, :], NEG_INF)
         for g in range(N_GROUPS)], axis=0)
    io_e = _iota((n_exp, tm), 0).astype(F32)
    allowed = masked
    ids, ws = [], []
    for _ in range(TOP_K):
        m = jnp.max(masked, axis=0, keepdims=True)
        i = jnp.min(jnp.where(masked == m, io_e, big), axis=0, keepdims=True)
        hit = io_e == i
        ws.append(jnp.sum(jnp.where(hit, scores, 0.0), axis=0, keepdims=True))
        ids.append(i)
        masked = jnp.where(hit, NEG_INF, masked)
    sel = jnp.where(masked == NEG_INF, jnp.where(allowed == NEG_INF, 0.0, 1.0), 0.0)
    w = jnp.concatenate(ws, axis=0)
    wts_ref[...] = w / jnp.sum(w, axis=0, keepdims=True) * ROUTED_SCALE
    idx_ref[...] = jnp.concatenate(ids, axis=0).astype(jnp.int32)

    @pl.when(pl.program_id(0) == 0)
    def _():
        cnt_ref[...] = jnp.zeros_like(cnt_ref)

    cnt_ref[...] += jnp.sum(sel, axis=1, keepdims=True)


def _post_mix(x2, oa, ob, wo_bf, g1, b1, wr_t, rbias, tm):
    t, d = x2.shape
    n_exp = wr_t.shape[0]
    row = lambda w: pl.BlockSpec((tm, w), lambda i: (i, 0))
    const = lambda shape: pl.BlockSpec(shape, lambda i: (0, 0))
    tok = pl.BlockSpec((TOP_K, tm), lambda i: (0, i))
    return pl.pallas_call(
        _post_mix_kernel,
        grid=(t // tm,),
        in_specs=[row(d), row(GDN_VW), row(RET_VW), const(wo_bf.shape), const((1, d)), const((1, d)),
                  const(wr_t.shape), const((n_exp, 1))],
        out_specs=[pl.BlockSpec((tm * SUBLANES, LANES), lambda i: (i, 0)),
                   pl.BlockSpec((tm * PACK_ROWS, LANES), lambda i: (i, 0)), tok, tok, const((n_exp, 1))],
        out_shape=[jax.ShapeDtypeStruct((t * SUBLANES, LANES), F32),
                   jax.ShapeDtypeStruct((t * PACK_ROWS, LANES), jnp.uint32),
                   jax.ShapeDtypeStruct((TOP_K, t), jnp.int32),
                   jax.ShapeDtypeStruct((TOP_K, t), F32),
                   jax.ShapeDtypeStruct((n_exp, 1), F32)],
        compiler_params=_params(("arbitrary",)),
        name="post_mix",
    )(x2, oa, ob, wo_bf, g1, b1, wr_t, rbias)


def _dest_kernel(idx_ref, base_ref, dest_ref, carry_ref):
    n_exp = base_ref.shape[0]
    tm = idx_ref.shape[1]

    @pl.when(pl.program_id(0) == 0)
    def _():
        carry_ref[...] = jnp.zeros_like(carry_ref)

    io_e = _iota((n_exp, tm), 0)
    idx = idx_ref[...]
    hits = [io_e == idx[k:k + 1, :] for k in range(TOP_K)]
    sel = jnp.zeros((n_exp, tm), F32)
    for hit in hits:
        sel = jnp.where(hit, 1.0, sel)
    before = jnp.where(_iota((tm, tm), 0) < _iota((tm, tm), 1), 1.0, 0.0).astype(BF16)
    rank = _dot(sel.astype(BF16), before)
    tot = base_ref[...] + carry_ref[...] + rank
    dest = [jnp.sum(jnp.where(hit, tot, 0.0), axis=0, keepdims=True) for hit in hits]
    dest_ref[...] = jnp.concatenate(dest, axis=0).astype(jnp.int32) * PACK_ROWS
    carry_ref[...] += jnp.sum(sel, axis=1, keepdims=True)


def _dest(idx_t, base, tm):
    t = idx_t.shape[1]
    n_exp = base.shape[0]
    tok = pl.BlockSpec((TOP_K, tm), lambda i: (0, i))
    return pl.pallas_call(
        _dest_kernel,
        grid=(t // tm,),
        in_specs=[tok, pl.BlockSpec((n_exp, 1), lambda i: (0, 0))],
        out_specs=tok,
        out_shape=jax.ShapeDtypeStruct((TOP_K, t), jnp.int32),
        scratch_shapes=[pltpu.VMEM((n_exp, 1), F32)],
        compiler_params=_params(("arbitrary",)),
        name="dest",
    )(idx_t, base)


def _dispatch_kernel(dest_ref, h_ref, xs_ref, zero_ref, sem, zsem):
    tm = h_ref.shape[0] // PACK_ROWS
    last = pl.num_programs(0) - 1
    pad_row = xs_ref.shape[0] - zero_ref.shape[0]

    @pl.when(pl.program_id(0) == last)
    def _():
        zero_ref[...] = jnp.zeros_like(zero_ref)
        _tile_copy(zero_ref, 0, xs_ref, pad_row, zsem, EXPERT_CHUNK, PACK_ROWS).start()

    def start(t, c):
        for k in range(TOP_K):
            _tile_copy(h_ref, t * PACK_ROWS, xs_ref, dest_ref[t * TOP_K + k], sem, 1, PACK_ROWS).start(priority=k % 2)
        return c

    lax.fori_loop(0, tm, start, 0)

    for k in range(TOP_K):
        _tile_copy(h_ref, 0, xs_ref, 0, sem, tm, PACK_ROWS).wait()

    @pl.when(pl.program_id(0) == last)
    def _():
        _tile_copy(zero_ref, 0, xs_ref, pad_row, zsem, EXPERT_CHUNK, PACK_ROWS).wait()


def _dispatch(dest_t, h_pk, tm):
    t = dest_t.shape[0] // TOP_K
    n_rows = (t * TOP_K + EXPERT_CHUNK) * PACK_ROWS
    return pl.pallas_call(
        _dispatch_kernel,
        grid=(t // tm,),
        in_specs=[pl.BlockSpec((tm * TOP_K,), lambda i: (i,), memory_space=pltpu.SMEM),
                  pl.BlockSpec((tm * PACK_ROWS, LANES), lambda i: (i, 0))],
        out_specs=pl.BlockSpec(memory_space=pl.ANY),
        out_shape=jax.ShapeDtypeStruct((n_rows, LANES), jnp.uint32),
        scratch_shapes=[pltpu.VMEM((EXPERT_CHUNK * PACK_ROWS, LANES), jnp.uint32),
                        pltpu.SemaphoreType.DMA(()), pltpu.SemaphoreType.DMA(())],
        compiler_params=_params(("arbitrary",)),
        name="dispatch",
    )(dest_t, h_pk)


def _chunk_pieces(n):
    pieces = [(j < n // EXPERT_PIECE, j * EXPERT_PIECE, EXPERT_PIECE) for j in range(EXPERT_CHUNK // EXPERT_PIECE)]
    size = EXPERT_PIECE // 2
    while size >= 1:
        shift = int(math.log2(size)) + 1
        pieces.append(((n & size) != 0, (n >> shift) << shift, size))
        size //= 2
    return pieces


ST_CHUNKS = 0
ST_ISSUED = 1
ST_CUR_E = 2
ST_CUR_I = 3
ST_WRITE = 4
N_XBUF = 4
N_OBUF = 3


def _expert_kernel(start_ref, count_ref, wg_ref, wu_ref, wd_ref, xs_ref, eo_ref,
                   xbuf, obuf, st_ref, rsem, wsem):
    e = pl.program_id(0)
    n_exp = pl.num_programs(0)
    ch = EXPERT_CHUNK
    s0 = start_ref[e]
    cnt = count_ref[e]
    n_ch = (cnt + ch - 1) // ch

    def read_parts(first_slot, buf):
        return [_tile_copy(xs_ref, (first_slot + j * EXPERT_PIECE) * PACK_ROWS, xbuf.at[buf],
                           j * EXPERT_PIECE * PACK_ROWS, rsem.at[buf], EXPERT_PIECE, PACK_ROWS)
                for j in range(ch // EXPERT_PIECE)]

    def read_next():
        def exhausted(c):
            cnt_c = count_ref[jnp.minimum(c[0], n_exp - 1)]
            return jnp.logical_and(c[0] < n_exp, c[1] * ch >= cnt_c)

        ce, ci = lax.while_loop(exhausted, lambda c: (c[0] + 1, jnp.int32(0)),
                                (st_ref[ST_CUR_E], st_ref[ST_CUR_I]))

        @pl.when(ce < n_exp)
        def _():
            first_slot = start_ref[jnp.minimum(ce, n_exp - 1)] + ci * ch
            buf = lax.rem(st_ref[ST_ISSUED], N_XBUF)
            for j, cp in enumerate(read_parts(first_slot, buf)):
                cp.start(priority=j % 2)
            st_ref[ST_ISSUED] = st_ref[ST_ISSUED] + 1

        st_ref[ST_CUR_E] = ce
        st_ref[ST_CUR_I] = ci + 1

    def wait_read(buf):
        for cp in read_parts(0, buf):
            cp.wait()

    def out_pieces(buf, first_slot, n, wait):
        for j, (pred, off, size) in enumerate(_chunk_pieces(n)):
            @pl.when(pred)
            def _(j=j, off=off, size=size):
                cp = _tile_copy(obuf.at[buf], off * PACK_ROWS, eo_ref, (first_slot + off) * PACK_ROWS,
                                wsem.at[buf], size, PACK_ROWS)
                cp.wait() if wait else cp.start(priority=j % 2)

    def drain(buf):
        out_pieces(buf, st_ref[ST_WRITE + 2 * buf], st_ref[ST_WRITE + 2 * buf + 1], True)
        st_ref[ST_WRITE + 2 * buf + 1] = 0

    @pl.when(e == 0)
    def _():
        for i in range(st_ref.shape[0]):
            st_ref[i] = 0
        for _ in range(N_XBUF - 1):
            read_next()

    def body(i, c):
        g = st_ref[ST_CHUNKS]
        buf = lax.rem(g, N_XBUF)
        wait_read(buf)
        read_next()
        obf = lax.rem(g, N_OBUF)
        drain(obf)
        first_slot = s0 + i * ch
        n = jnp.minimum(ch, cnt - i * ch)

        def swiglu(rows):
            xb = _load_packed(xbuf, rows, lead=(buf,), dtype=BF16)
            hid = _silu(_dot(xb, wg_ref[0].astype(BF16))) * _dot(xb, wu_ref[0].astype(BF16))
            _store_packed(obuf, _dot(hid.astype(BF16), wd_ref[0].astype(BF16)), lead=(obf,))

        pl.when(n > ch // 2)(lambda: swiglu(ch))
        pl.when(n <= ch // 2)(lambda: swiglu(ch // 2))
        out_pieces(obf, first_slot, n, False)
        st_ref[ST_WRITE + 2 * obf] = first_slot
        st_ref[ST_WRITE + 2 * obf + 1] = n
        st_ref[ST_CHUNKS] = g + 1
        return c

    lax.fori_loop(0, n_ch, body, 0)

    @pl.when(e == n_exp - 1)
    def _():
        for b in range(N_OBUF):
            drain(b)


def _experts(starts, counts, xs, wg, wu, wd):
    n_exp, d, d_e = wg.shape
    n_slot = xs.shape[0] // PACK_ROWS - EXPERT_CHUNK
    rows = EXPERT_CHUNK * PACK_ROWS
    grid_spec = pltpu.PrefetchScalarGridSpec(
        num_scalar_prefetch=2,
        grid=(n_exp,),
        in_specs=[pl.BlockSpec((1, d, d_e), lambda e, st, ct: (e, 0, 0)),
                  pl.BlockSpec((1, d, d_e), lambda e, st, ct: (e, 0, 0)),
                  pl.BlockSpec((1, d_e, d), lambda e, st, ct: (e, 0, 0)),
                  pl.BlockSpec(memory_space=pl.ANY)],
        out_specs=pl.BlockSpec(memory_space=pl.ANY),
        scratch_shapes=[pltpu.VMEM((N_XBUF, rows, LANES), jnp.uint32), pltpu.VMEM((N_OBUF, rows, LANES), jnp.uint32),
                        pltpu.SMEM((ST_WRITE + 2 * N_OBUF,), jnp.int32),
                        pltpu.SemaphoreType.DMA((N_XBUF,)), pltpu.SemaphoreType.DMA((N_OBUF,))],
    )
    return pl.pallas_call(
        _expert_kernel,
        grid_spec=grid_spec,
        out_shape=jax.ShapeDtypeStruct((n_slot * PACK_ROWS, LANES), jnp.uint32),
        compiler_params=_params(("arbitrary",)),
        name="experts",
    )(starts, counts, wg, wu, wd, xs)


def _combine_kernel(dest_ref, dnext_ref, wts_ref, h_ref, eo_ref, wgs_ref, wus_ref, wds_ref, g_ref, b_ref,
                    out_ref, gbuf, sem):
    i = pl.program_id(0)
    tm = out_ref.shape[0]
    buf = lax.rem(i, 2)

    def gather(d_ref, bf):
        def start(t, c):
            for k in range(TOP_K):
                _tile_copy(eo_ref, d_ref[t * TOP_K + k], gbuf.at[bf, k], t * PACK_ROWS, sem.at[bf],
                           1, PACK_ROWS).start(priority=k % 2)
            return c

        lax.fori_loop(0, tm, start, 0)

    @pl.when(i == 0)
    def _():
        gather(dest_ref, 0)

    def step(bf):
        for t in range(tm):
            for k in range(TOP_K):
                _tile_copy(eo_ref, dnext_ref[t * TOP_K + k], gbuf.at[1 - bf, k], t * PACK_ROWS, sem.at[1 - bf],
                           1, PACK_ROWS).start(priority=k % 2)
        hh = _load_token_tiles(h_ref, tm)
        hb = hh.astype(BF16)
        hid = _silu(_dot(hb, wgs_ref[...])) * _dot(hb, wus_ref[...])
        acc = DN_ALPHA * hh + _dot(hid.astype(BF16), wds_ref[...])
        w_t = jnp.concatenate([wts_ref[...], jnp.zeros((LANES - TOP_K, tm), F32)], axis=0).T
        for k in range(TOP_K):
            _tile_copy(eo_ref, 0, gbuf.at[bf, k], 0, sem.at[bf], tm, PACK_ROWS).wait()
        for k in range(TOP_K):
            acc = acc + _load_packed(gbuf, tm, lead=(bf, k)) * w_t[:, k:k + 1]
        out_ref[...] = _layer_norm(acc, g_ref[...], b_ref[...])

        @pl.when(i == pl.num_programs(0) - 1)
        def _():
            for k in range(TOP_K):
                _tile_copy(eo_ref, 0, gbuf.at[1 - bf, k], 0, sem.at[1 - bf], tm, PACK_ROWS).wait()

    for bf in range(2):
        pl.when(buf == bf)(lambda bf=bf: step(bf))


def _combine(dest_t, wts_t, h_tt, eo, wgs, wus, wds, g2, b2, tm):
    t = dest_t.shape[0] // TOP_K
    d = wgs.shape[0]
    const = lambda shape: pl.BlockSpec(shape, lambda i: (0, 0))
    n_tiles = t // tm
    return pl.pallas_call(
        _combine_kernel,
        grid=(n_tiles,),
        in_specs=[pl.BlockSpec((tm * TOP_K,), lambda i: (i,), memory_space=pltpu.SMEM),
                  pl.BlockSpec((tm * TOP_K,), lambda i: (jnp.minimum(i + 1, n_tiles - 1),), memory_space=pltpu.SMEM),
                  pl.BlockSpec((TOP_K, tm), lambda i: (0, i)),
                  pl.BlockSpec((tm * SUBLANES, LANES), lambda i: (i, 0)),
                  pl.BlockSpec(memory_space=pl.ANY),
                  const(wgs.shape), const(wus.shape), const(wds.shape), const((1, d)), const((1, d))],
        out_specs=pl.BlockSpec((tm, d), lambda i: (i, 0)),
        out_shape=jax.ShapeDtypeStruct((t, d), F32),
        scratch_shapes=[pltpu.VMEM((2, TOP_K, tm * PACK_ROWS, LANES), jnp.uint32), pltpu.SemaphoreType.DMA((2,))],
        compiler_params=_params(("arbitrary",)),
        name="combine",
    )(dest_t, dest_t, wts_t, h_tt, eo, wgs, wus, wds, g2, b2)


def _regroup_w_in(w):
    d = w.shape[0]
    splits = np.cumsum([GDN_QK, GDN_QK, GDN_VW, GDN_VW, GDN_HEADS, GDN_HEADS, RET_QK, RET_QK, RET_VW])
    qa, ka, va, za, ba, aa, qb, kb, vb, gb = jnp.split(w, splits.tolist(), axis=1)
    ba_blk = jnp.concatenate([ba, aa, jnp.zeros((d, LANES - 2 * GDN_HEADS), w.dtype)], axis=1)
    qb = qb.reshape(d, RET_HEADS, RET_DK // 2, 2)
    kb = kb.reshape(d, RET_HEADS, RET_DK // 2, 2)
    qkb = jnp.concatenate([qb[..., 0], qb[..., 1], kb[..., 0], kb[..., 1]], axis=-1).reshape(d, RET_HEADS * LANES)
    return jnp.concatenate([qa, ka, va, za, qkb, vb, gb, ba_blk], axis=1).astype(BF16)


def _rotary_tables(seq):
    inv = 1.0 / (ROPE_BASE ** jnp.linspace(0.0, 1.0, RET_DK // 2, dtype=F32))
    ang = jnp.arange(seq, dtype=F32)[:, None] * inv[None, :]
    c, s = jnp.cos(ang), jnp.sin(ang)
    ks = RET_DK ** -0.5
    rot_c = jnp.concatenate([c, c, c * ks, c * ks], axis=1)
    rot_s = jnp.concatenate([-s, s, -s * ks, s * ks], axis=1)
    return rot_c, rot_s


def _tile(n, pref):
    return pref if n % pref == 0 else n


def kernel(x, w_in, gdn_conv_w, gdn_a_log, gdn_dt_bias, gdn_norm_w, w_out, ln1_g, ln1_b, w_router, router_bias,
           w_gate_e, w_up_e, w_down_e, w_gate_s, w_up_s, w_down_s, ln2_g, ln2_b):
    batch, seq, d = x.shape
    t = batch * seq
    n_exp = w_router.shape[-1]
    hcur = x.reshape(t, d)
    for l in range(DEPTH):
        w_bf = _regroup_w_in(w_in[l])
        log_gamma = jnp.log(1.0 - 2.0 ** (-5.0 - jnp.arange(RET_HEADS, dtype=F32)))
        hp = jnp.stack([-jnp.exp(gdn_a_log[l].astype(F32)), gdn_dt_bias[l].astype(F32), log_gamma])
        rot_c, rot_s = _rotary_tables(seq)
        oa, ob = _mixer(hcur, w_bf, hp, gdn_conv_w[l].astype(F32), gdn_norm_w[l].reshape(1, -1).astype(F32),
                        rot_c, rot_s, batch, seq)
        hh, h_pk, idx_t, wts_t, counts = _post_mix(
            hcur, oa, ob, w_out[l].astype(BF16), ln1_g[l].reshape(1, d), ln1_b[l].reshape(1, d),
            w_router[l].T.astype(BF16), router_bias[l].reshape(n_exp, 1).astype(F32), _tile(t, POST_MIX_TM))
        base = jnp.cumsum(counts, axis=0) - counts
        dest_t = _dest(idx_t, base, _tile(t, DEST_TM)).T.reshape(t * TOP_K)
        xs = _dispatch(dest_t, h_pk, _tile(t, DISPATCH_TM))
        eo = _experts(base.reshape(n_exp).astype(jnp.int32), counts.reshape(n_exp).astype(jnp.int32),
                      xs, w_gate_e[l], w_up_e[l], w_down_e[l])
        hcur = _combine(dest_t, wts_t, hh, eo, w_gate_s[l].astype(BF16), w_up_s[l].astype(BF16),
                        w_down_s[l].astype(BF16), ln2_g[l].reshape(1, d), ln2_b[l].reshape(1, d), _tile(t, COMBINE_TM))
    return hcur.reshape(batch, seq, d)
```

```python
import math

import numpy as np
import jax
import jax.numpy as jnp
from jax import lax
from jax.experimental import pallas as pl
from jax.experimental.pallas import tpu as pltpu

GDN_HEADS = 4
GDN_DK = 128
GDN_DV = 128
GDN_CONV = 4
GDN_CHUNK = 64
RET_HEADS = 4
RET_DK = 64
RET_DV = 128
ROPE_BASE = 10000.0
N_GROUPS = 8
TOPK_GROUPS = 4
TOP_K = 8
ROUTED_SCALE = 2.5
DEPTH = 1
DN_ALPHA = (2.0 * DEPTH) ** 0.25
LN_EPS = 1e-5
NORM_EPS = 1e-6

GDN_QK = GDN_HEADS * GDN_DK
GDN_VW = GDN_HEADS * GDN_DV
RET_QK = RET_HEADS * RET_DK
RET_VW = RET_HEADS * RET_DV

LANES = 128
SUBLANES = 8
VMEM_LIMIT_BYTES = 56 * 1024 * 1024

SUPER = 256
NEG_INF = float("-inf")

COL_QA = 0
COL_KA = COL_QA + GDN_HEADS
COL_VA = COL_KA + GDN_HEADS
COL_ZA = COL_VA + GDN_HEADS
COL_QKB = COL_ZA + GDN_HEADS
COL_VB = COL_QKB + RET_HEADS
COL_GB = COL_VB + RET_HEADS
COL_BA = COL_GB + RET_HEADS
N_COLBLK = COL_BA + 1
PROJ_W = N_COLBLK * LANES
HEAD_W = GDN_HEADS * LANES
MIX_TS = 512
POST_MIX_TM = 512
DEST_TM = 512
DISPATCH_TM = 1024
COMBINE_TM = 256
EXPERT_CHUNK = 1024
EXPERT_PIECE = 128
PACK_ROWS = 4

BF16 = jnp.bfloat16
F32 = jnp.float32


def _dot(a, b):
    return jnp.dot(a, b, preferred_element_type=F32)


def _dot_nt(a, b):
    return lax.dot_general(a, b, (((1,), (1,)), ((), ())), preferred_element_type=F32)


def _silu(x):
    return x * jax.nn.sigmoid(x)


def _softplus(x):
    return jnp.maximum(x, 0.0) + jnp.log1p(jnp.exp(-jnp.abs(x)))


def _iota(shape, dim, dtype=jnp.int32):
    return lax.broadcasted_iota(dtype, shape, dim)


def _params(sem):
    return pltpu.CompilerParams(dimension_semantics=sem, vmem_limit_bytes=VMEM_LIMIT_BYTES)


def _load_token_tiles(ref, n, lead=()):
    parts = [ref[lead + (pl.ds(s, n, stride=SUBLANES), slice(None))] for s in range(SUBLANES)]
    return jnp.concatenate(parts, axis=1)


def _store_token_tiles(ref, val, lead=()):
    n = val.shape[0]
    for s in range(SUBLANES):
        ref[lead + (pl.ds(s, n, stride=SUBLANES), slice(None))] = val[:, s * LANES:(s + 1) * LANES]


def _tile_copy(src_ref, src_row, dst_ref, dst_row, sem, n_slots=1, slot_rows=SUBLANES):
    rows = n_slots * slot_rows
    aligned = lambda r: r if isinstance(r, int) else pl.multiple_of(r, slot_rows)
    return pltpu.make_async_copy(src_ref.at[pl.ds(aligned(src_row), rows)],
                                 dst_ref.at[pl.ds(aligned(dst_row), rows)], sem)


def _store_packed(ref, val, lead=()):
    n, width = val.shape
    bits = pltpu.bitcast(val.astype(BF16).astype(F32), jnp.uint32)
    words = (bits[:, width // 2:] & jnp.uint32(0xFFFF0000)) | (bits[:, :width // 2] >> 16)
    for s in range(PACK_ROWS):
        ref[lead + (pl.ds(s, n, stride=PACK_ROWS), slice(None))] = words[:, s * LANES:(s + 1) * LANES]


def _load_packed(ref, n, lead=(), dtype=F32):
    words = [ref[lead + (pl.ds(s, n, stride=PACK_ROWS), slice(None))] for s in range(PACK_ROWS)]
    lo = [pltpu.bitcast(w << 16, F32) for w in words]
    hi = [pltpu.bitcast(w & jnp.uint32(0xFFFF0000), F32) for w in words]
    return jnp.concatenate(lo + hi, axis=1).astype(dtype)


def _mixer_kernel(hp_ref,
                  x_ref, xn_ref,
                  w_ref, cw_ref, nw_ref, rc_ref, rs_ref,
                  oa_ref, ob_ref, proj_ref, st_ref, st2_ref):
    j = pl.program_id(1)
    ts = x_ref.shape[0]
    n_super = ts // SUPER
    n_chunk = SUPER // GDN_CHUNK

    def project(rows, slot):
        xb = rows.astype(BF16)
        step = 4 * LANES
        for c in range(0, PROJ_W, step):
            n = min(step, PROJ_W - c)
            proj_ref[slot, :, c:c + n] = _dot(xb, w_ref[:, c:c + n])

    @pl.when(j == 0)
    def _():
        st_ref[...] = jnp.zeros_like(st_ref)
        st2_ref[...] = jnp.zeros_like(st2_ref)

    @pl.when((j == 0) & (pl.program_id(0) == 0))
    def _():
        proj_ref[1] = jnp.zeros(proj_ref.shape[1:], F32)
        project(x_ref[0:SUPER, :], 0)

    ri = _iota((SUPER, SUPER), 0)
    ci = _iota((SUPER, SUPER), 1)
    shift = int(math.log2(GDN_CHUNK))
    same = (ri >> shift) == (ci >> shift)
    causal = same & (ci <= ri)
    strict = same & (ci < ri)
    causal_t = same & (ri <= ci)
    eye = jnp.where(ri == ci, 1.0, 0.0)
    lane = _iota((SUPER, LANES), 1)
    dpos = jnp.maximum((ri - ci).astype(F32), 0.0)
    lower = ri >= ci
    pos = _iota((SUPER, 1), 0).astype(F32)
    half = lane < RET_DK
    pair_lo = (lane % RET_DK) < (RET_DK // 2)
    lane1 = _iota((1, LANES), 1)
    neg_a_lane = jnp.zeros((1, LANES), F32)
    dt_b_lane = jnp.zeros((1, LANES), F32)
    for h in range(GDN_HEADS):
        neg_a_lane = jnp.where(lane1 == h + GDN_HEADS, hp_ref[0, h], neg_a_lane)
        dt_b_lane = jnp.where(lane1 == h + GDN_HEADS, hp_ref[1, h], dt_b_lane)

    def pcols(col_blk, h):
        return slice((col_blk + h) * LANES, (col_blk + h + 1) * LANES)

    def conv_silu(col_blk, h, slot, s, tail):
        cur = proj_ref[slot, :, pcols(col_blk, h)]
        prev = jnp.where((s == 0) & (j == 0), 0.0, tail[:, pcols(col_blk, h)])
        ext = jnp.concatenate([prev, cur], axis=0)
        acc = jnp.zeros((SUPER, LANES), F32)
        for jj in range(GDN_CONV):
            off = SUBLANES - (GDN_CONV - 1) + jj
            acc = acc + cw_ref[jj:jj + 1, pcols(col_blk, h)] * ext[off:off + SUPER, :]
        return _silu(acc)

    def l2n(u):
        return u * lax.rsqrt(jnp.sum(u * u, axis=-1, keepdims=True) + NORM_EPS)

    heads = range(GDN_HEADS)
    hcols = [slice(h * LANES, (h + 1) * LANES) for h in heads]

    def gdn_heads(r0, s, slot, tail):
        q = [l2n(conv_silu(COL_QA, h, slot, s, tail)) * (GDN_DK ** -0.5) for h in heads]
        k = [l2n(conv_silu(COL_KA, h, slot, s, tail)) for h in heads]
        v = [conv_silu(COL_VA, h, slot, s, tail) for h in heads]
        ba = proj_ref[slot, :, pcols(COL_BA, 0)]
        gates = jnp.where(lane < GDN_HEADS, jax.nn.sigmoid(ba), neg_a_lane * _softplus(ba + dt_b_lane))
        gates_t = gates.T
        beta, g_col, gl_col, decay_c, decay_s = [], [], [], [], []
        for h in heads:
            beta.append(jnp.sum(jnp.where(lane == h, gates, 0.0), axis=1, keepdims=True))
            la_col = jnp.sum(jnp.where(lane == h + GDN_HEADS, gates, 0.0), axis=1, keepdims=True)
            la_row = gates_t[h + GDN_HEADS:h + GDN_HEADS + 1, :]
            gc = jnp.sum(jnp.where(causal, la_row, 0.0), axis=1, keepdims=True)
            gr = jnp.sum(jnp.where(causal_t, la_col, 0.0), axis=0, keepdims=True)
            g_col.append(gc)
            gl_col.append(jnp.sum(jnp.where(same, la_row, 0.0), axis=1, keepdims=True))
            dc = jnp.exp(jnp.where(causal, gc - gr, NEG_INF))
            decay_c.append(dc)
            decay_s.append(jnp.where(strict, dc, 0.0))
        kb = [k[h] * beta[h] for h in heads]
        k_bf = [k[h].astype(BF16) for h in heads]
        low = [_dot_nt(kb[h].astype(BF16), k_bf[h]) * decay_s[h] for h in heads]
        p = [eye - low[h] for h in heads]
        sq = low
        for _ in range(shift - 1):
            sq_bf = [sq[h].astype(BF16) for h in heads]
            sq = [_dot(sq_bf[h], sq_bf[h]) for h in heads]
            p = [p[h] + _dot(p[h].astype(BF16), sq[h].astype(BF16)) for h in heads]
        eg = [jnp.exp(g_col[h]) for h in heads]
        rhs = [jnp.concatenate([v[h] * beta[h], kb[h] * eg[h]], axis=1).astype(BF16) for h in heads]
        wk = [_dot(p[h].astype(BF16), rhs[h]) for h in heads]
        w_val = [wk[h][:, :GDN_DV] for h in heads]
        k_cum = [wk[h][:, GDN_DV:].astype(BF16) for h in heads]
        attn = [(_dot_nt(q[h].astype(BF16), k_bf[h]) * decay_c[h]).astype(BF16) for h in heads]
        q_dec = [(q[h] * eg[h]).astype(BF16) for h in heads]
        kd_t = [(k[h] * jnp.exp(gl_col[h] - g_col[h])).T.astype(BF16) for h in heads]
        gl = [jnp.exp(gl_col[h]) for h in heads]
        st = [st_ref[h] for h in heads]
        outs = [[] for _ in heads]
        for c in range(n_chunk):
            lo = c * GDN_CHUNK
            hi = lo + GDN_CHUNK
            st_bf = [st[h].astype(BF16) for h in heads]
            v_new = [w_val[h][lo:hi] - _dot(k_cum[h][lo:hi], st_bf[h]) for h in heads]
            v_pad = []
            for h in heads:
                pieces = [jnp.zeros((GDN_CHUNK, GDN_DV), F32)] * n_chunk
                pieces[c] = v_new[h]
                v_pad.append(jnp.concatenate(pieces, axis=0).astype(BF16))
            for h in heads:
                outs[h].append(_dot(q_dec[h][lo:hi], st_bf[h]) + _dot(attn[h][lo:hi], v_pad[h]))
            st = [st[h] * gl[h][lo:lo + 1, :] + _dot(kd_t[h], v_pad[h]) for h in heads]
        for h in heads:
            st_ref[h] = st[h]
            o = jnp.concatenate(outs[h], axis=0)
            o = o * lax.rsqrt(jnp.mean(o * o, axis=-1, keepdims=True) + NORM_EPS) * nw_ref[...]
            z = proj_ref[slot, :, pcols(COL_ZA, h)]
            oa_ref[pl.ds(r0, SUPER), hcols[h]] = (o * _silu(z)).astype(oa_ref.dtype)

    def ret_heads(r0, slot):
        rc = rc_ref[pl.ds(r0, SUPER), :]
        rs = rs_ref[pl.ds(r0, SUPER), :]
        lg = [hp_ref[2, h] for h in heads]
        qm, km, vb = [], [], []
        for h in heads:
            x = proj_ref[slot, :, pcols(COL_QKB, h)]
            swapped = jnp.where(pair_lo, pltpu.roll(x, LANES - RET_DK // 2, 1), pltpu.roll(x, RET_DK // 2, 1))
            xr = x * rc + swapped * rs
            qm.append(jnp.where(half, xr, 0.0))
            km.append(jnp.where(half, pltpu.roll(xr, LANES - RET_DK, 1), 0.0))
            vb.append(proj_ref[slot, :, pcols(COL_VB, h)].astype(BF16))
        inner = [(_dot_nt(qm[h].astype(BF16), km[h].astype(BF16))
                  * jnp.where(lower, jnp.exp(lg[h] * dpos), 0.0)).astype(BF16) for h in heads]
        st2 = [st2_ref[h] for h in heads]
        ob = [_dot(inner[h], vb[h])
              + _dot((qm[h] * jnp.exp(lg[h] * (pos + 1.0))).astype(BF16), st2[h].astype(BF16)) for h in heads]
        for h in heads:
            g_chunk = jnp.exp(jnp.full((1, 1), SUPER, F32) * lg[h])
            st2_ref[h] = st2[h] * g_chunk + _dot((km[h] * jnp.exp(lg[h] * (SUPER - 1.0 - pos))).T.astype(BF16), vb[h])
        for h in heads:
            mu = jnp.mean(ob[h], axis=-1, keepdims=True)
            oc = ob[h] - mu
            oc = oc * lax.rsqrt(jnp.mean(oc * oc, axis=-1, keepdims=True) + NORM_EPS)
            gate = proj_ref[slot, :, pcols(COL_GB, h)]
            ob_ref[pl.ds(r0, SUPER), hcols[h]] = (oc * _silu(gate)).astype(ob_ref.dtype)

    for s in range(n_super):
        r0 = s * SUPER
        slot = s % 2
        tail = proj_ref[1 - slot, SUPER - SUBLANES:SUPER, 0:COL_ZA * LANES]
        nxt = x_ref[r0 + SUPER:r0 + 2 * SUPER, :] if s + 1 < n_super else xn_ref[...]
        project(nxt, 1 - slot)
        gdn_heads(r0, s, slot, tail)
        ret_heads(r0, slot)


def _mixer(x2, w_bf, hp, conv_w, norm_w, rot_c, rot_s, batch, seq):
    t, d = x2.shape
    ts = MIX_TS if seq % MIX_TS == 0 else seq
    nj = seq // ts
    per_blk = ts // SUPER
    assert per_blk % 2 == 0
    last_chunk = t // SUPER - 1
    const2 = lambda shape: pl.BlockSpec(shape, lambda b, j: (0, 0))
    tab = pl.BlockSpec((ts, LANES), lambda b, j: (j, 0))
    in_specs = [
        pl.BlockSpec(memory_space=pltpu.SMEM),
        pl.BlockSpec((ts, d), lambda b, j: (b * nj + j, 0)),
        pl.BlockSpec((SUPER, d), lambda b, j: (jnp.minimum((b * nj + j + 1) * per_blk, last_chunk), 0)),
        const2(w_bf.shape), const2(conv_w.shape), const2((1, GDN_DV)), tab, tab,
    ]
    out_spec = pl.BlockSpec((ts, HEAD_W), lambda b, j: (b * nj + j, 0))
    return pl.pallas_call(
        _mixer_kernel,
        grid=(batch, nj),
        in_specs=in_specs,
        out_specs=[out_spec, out_spec],
        out_shape=[jax.ShapeDtypeStruct((t, GDN_VW), BF16), jax.ShapeDtypeStruct((t, RET_VW), BF16)],
        scratch_shapes=[pltpu.VMEM((2, SUPER, PROJ_W), F32),
                        pltpu.VMEM((GDN_HEADS, GDN_DK, GDN_DV), F32),
                        pltpu.VMEM((RET_HEADS, LANES, RET_DV), F32)],
        compiler_params=_params(("arbitrary", "arbitrary")),
        name="mixer",
    )(hp, x2, x2, w_bf, conv_w, norm_w, rot_c, rot_s)


def _layer_norm(u, g, b):
    mu = jnp.mean(u, axis=-1, keepdims=True)
    uc = u - mu
    var = jnp.mean(uc * uc, axis=-1, keepdims=True)
    return uc * lax.rsqrt(var + LN_EPS) * g + b


def _post_mix_kernel(x_ref, oa_ref, ob_ref, wo_ref, g_ref, b_ref, wr_ref, rb_ref,
                     h_ref, hp_ref, idx_ref, wts_ref, cnt_ref):
    n_exp = wr_ref.shape[0]
    tm = x_ref.shape[0]
    per_grp = n_exp // N_GROUPS
    mix = _dot(oa_ref[...], wo_ref[:GDN_VW, :]) + _dot(ob_ref[...], wo_ref[GDN_VW:, :])
    hh = _layer_norm(DN_ALPHA * x_ref[...] + mix, g_ref[...], b_ref[...])
    _store_token_tiles(h_ref, hh)
    _store_packed(hp_ref, hh)
    scores = jax.nn.sigmoid(_dot_nt(wr_ref[...], hh.astype(BF16)))
    choice = scores + rb_ref[...]
    big = float(n_exp)
    io_g = _iota((per_grp, tm), 0).astype(F32)
    grp = []
    for g in range(N_GROUPS):
        blk = choice[g * per_grp:(g + 1) * per_grp, :]
        m1 = jnp.max(blk, axis=0, keepdims=True)
        i1 = jnp.min(jnp.where(blk == m1, io_g, big), axis=0, keepdims=True)
        m2 = jnp.max(jnp.where(io_g == i1, NEG_INF, blk), axis=0, keepdims=True)
        grp.append(m1 + m2)
    gsc = jnp.concatenate(grp, axis=0)
    io8 = _iota((N_GROUPS, tm), 0).astype(F32)
    gsel = jnp.zeros((N_GROUPS, tm), F32)
    for _ in range(TOPK_GROUPS):
        m = jnp.max(gsc, axis=0, keepdims=True)
        i = jnp.min(jnp.where(gsc == m, io8, big), axis=0, keepdims=True)
        hit = io8 == i
        gsel = jnp.where(hit, 1.0, gsel)
        gsc = jnp.where(hit, NEG_INF, gsc)
    masked = jnp.concatenate(
        [jnp.where(gsel[g:g + 1, :] > 0.0, choice[g * per_grp:(g + 1) * per_grp, :], NEG_INF)
         for g in range(N_GROUPS)], axis=0)
    io_e = _iota((n_exp, tm), 0).astype(F32)
    allowed = masked
    ids, ws = [], []
    for _ in range(TOP_K):
        m = jnp.max(masked, axis=0, keepdims=True)
        i = jnp.min(jnp.where(masked == m, io_e, big), axis=0, keepdims=True)
        hit = io_e == i
        ws.append(jnp.sum(jnp.where(hit, scores, 0.0), axis=0, keepdims=True))
        ids.append(i)
        masked = jnp.where(hit, NEG_INF, masked)
    sel = jnp.where(masked == NEG_INF, jnp.where(allowed == NEG_INF, 0.0, 1.0), 0.0)
    w = jnp.concatenate(ws, axis=0)
    wts_ref[...] = w / jnp.sum(w, axis=0, keepdims=True) * ROUTED_SCALE
    idx_ref[...] = jnp.concatenate(ids, axis=0).astype(jnp.int32)

    @pl.when(pl.program_id(0) == 0)
    def _():
        cnt_ref[...] = jnp.zeros_like(cnt_ref)

    cnt_ref[...] += jnp.sum(sel, axis=1, keepdims=True)


def _post_mix(x2, oa, ob, wo_bf, g1, b1, wr_t, rbias, tm):
    t, d = x2.shape
    n_exp = wr_t.shape[0]
    row = lambda w: pl.BlockSpec((tm, w), lambda i: (i, 0))
    const = lambda shape: pl.BlockSpec(shape, lambda i: (0, 0))
    tok = pl.BlockSpec((TOP_K, tm), lambda i: (0, i))
    return pl.pallas_call(
        _post_mix_kernel,
        grid=(t // tm,),
        in_specs=[row(d), row(GDN_VW), row(RET_VW), const(wo_bf.shape), const((1, d)), const((1, d)),
                  const(wr_t.shape), const((n_exp, 1))],
        out_specs=[pl.BlockSpec((tm * SUBLANES, LANES), lambda i: (i, 0)),
                   pl.BlockSpec((tm * PACK_ROWS, LANES), lambda i: (i, 0)), tok, tok, const((n_exp, 1))],
        out_shape=[jax.ShapeDtypeStruct((t * SUBLANES, LANES), F32),
                   jax.ShapeDtypeStruct((t * PACK_ROWS, LANES), jnp.uint32),
                   jax.ShapeDtypeStruct((TOP_K, t), jnp.int32),
                   jax.ShapeDtypeStruct((TOP_K, t), F32),
                   jax.ShapeDtypeStruct((n_exp, 1), F32)],
        compiler_params=_params(("arbitrary",)),
        name="post_mix",
    )(x2, oa, ob, wo_bf, g1, b1, wr_t, rbias)


def _dest_kernel(idx_ref, base_ref, dest_ref, carry_ref):
    n_exp = base_ref.shape[0]
    tm = idx_ref.shape[1]

    @pl.when(pl.program_id(0) == 0)
    def _():
        carry_ref[...] = jnp.zeros_like(carry_ref)

    io_e = _iota((n_exp, tm), 0)
    idx = idx_ref[...]
    hits = [io_e == idx[k:k + 1, :] for k in range(TOP_K)]
    sel = jnp.zeros((n_exp, tm), F32)
    for hit in hits:
        sel = jnp.where(hit, 1.0, sel)
    before = jnp.where(_iota((tm, tm), 0) < _iota((tm, tm), 1), 1.0, 0.0).astype(BF16)
    rank = _dot(sel.astype(BF16), before)
    tot = base_ref[...] + carry_ref[...] + rank
    dest = [jnp.sum(jnp.where(hit, tot, 0.0), axis=0, keepdims=True) for hit in hits]
    dest_ref[...] = jnp.concatenate(dest, axis=0).astype(jnp.int32) * PACK_ROWS
    carry_ref[...] += jnp.sum(sel, axis=1, keepdims=True)


def _dest(idx_t, base, tm):
    t = idx_t.shape[1]
    n_exp = base.shape[0]
    tok = pl.BlockSpec((TOP_K, tm), lambda i: (0, i))
    return pl.pallas_call(
        _dest_kernel,
        grid=(t // tm,),
        in_specs=[tok, pl.BlockSpec((n_exp, 1), lambda i: (0, 0))],
        out_specs=tok,
        out_shape=jax.ShapeDtypeStruct((TOP_K, t), jnp.int32),
        scratch_shapes=[pltpu.VMEM((n_exp, 1), F32)],
        compiler_params=_params(("arbitrary",)),
        name="dest",
    )(idx_t, base)


def _dispatch_kernel(dest_ref, h_ref, xs_ref, zero_ref, sem, zsem):
    tm = h_ref.shape[0] // PACK_ROWS
    last = pl.num_programs(0) - 1
    pad_row = xs_ref.shape[0] - zero_ref.shape[0]

    @pl.when(pl.program_id(0) == last)
    def _():
        zero_ref[...] = jnp.zeros_like(zero_ref)
        _tile_copy(zero_ref, 0, xs_ref, pad_row, zsem, EXPERT_CHUNK, PACK_ROWS).start()

    def start(t, c):
        for k in range(TOP_K):
            _tile_copy(h_ref, t * PACK_ROWS, xs_ref, dest_ref[t * TOP_K + k], sem, 1, PACK_ROWS).start(priority=k % 2)
        return c

    lax.fori_loop(0, tm, start, 0)

    for k in range(TOP_K):
        _tile_copy(h_ref, 0, xs_ref, 0, sem, tm, PACK_ROWS).wait()

    @pl.when(pl.program_id(0) == last)
    def _():
        _tile_copy(zero_ref, 0, xs_ref, pad_row, zsem, EXPERT_CHUNK, PACK_ROWS).wait()


def _dispatch(dest_t, h_pk, tm):
    t = dest_t.shape[0] // TOP_K
    n_rows = (t * TOP_K + EXPERT_CHUNK) * PACK_ROWS
    return pl.pallas_call(
        _dispatch_kernel,
        grid=(t // tm,),
        in_specs=[pl.BlockSpec((tm * TOP_K,), lambda i: (i,), memory_space=pltpu.SMEM),
                  pl.BlockSpec((tm * PACK_ROWS, LANES), lambda i: (i, 0))],
        out_specs=pl.BlockSpec(memory_space=pl.ANY),
        out_shape=jax.ShapeDtypeStruct((n_rows, LANES), jnp.uint32),
        scratch_shapes=[pltpu.VMEM((EXPERT_CHUNK * PACK_ROWS, LANES), jnp.uint32),
                        pltpu.SemaphoreType.DMA(()), pltpu.SemaphoreType.DMA(())],
        compiler_params=_params(("arbitrary",)),
        name="dispatch",
    )(dest_t, h_pk)


def _chunk_pieces(n):
    pieces = [(j < n // EXPERT_PIECE, j * EXPERT_PIECE, EXPERT_PIECE) for j in range(EXPERT_CHUNK // EXPERT_PIECE)]
    size = EXPERT_PIECE // 2
    while size >= 1:
        shift = int(math.log2(size)) + 1
        pieces.append(((n & size) != 0, (n >> shift) << shift, size))
        size //= 2
    return pieces


ST_CHUNKS = 0
ST_ISSUED = 1
ST_CUR_E = 2
ST_CUR_I = 3
ST_WRITE = 4
N_XBUF = 4
N_OBUF = 3


def _expert_kernel(start_ref, count_ref, wg_ref, wu_ref, wd_ref, xs_ref, eo_ref,
                   xbuf, obuf, st_ref, rsem, wsem):
    e = pl.program_id(0)
    n_exp = pl.num_programs(0)
    ch = EXPERT_CHUNK
    s0 = start_ref[e]
    cnt = count_ref[e]
    n_ch = (cnt + ch - 1) // ch

    def read_parts(first_slot, buf):
        return [_tile_copy(xs_ref, (first_slot + j * EXPERT_PIECE) * PACK_ROWS, xbuf.at[buf],
                           j * EXPERT_PIECE * PACK_ROWS, rsem.at[buf], EXPERT_PIECE, PACK_ROWS)
                for j in range(ch // EXPERT_PIECE)]

    def read_next():
        def exhausted(c):
            cnt_c = count_ref[jnp.minimum(c[0], n_exp - 1)]
            return jnp.logical_and(c[0] < n_exp, c[1] * ch >= cnt_c)

        ce, ci = lax.while_loop(exhausted, lambda c: (c[0] + 1, jnp.int32(0)),
                                (st_ref[ST_CUR_E], st_ref[ST_CUR_I]))

        @pl.when(ce < n_exp)
        def _():
            first_slot = start_ref[jnp.minimum(ce, n_exp - 1)] + ci * ch
            buf = lax.rem(st_ref[ST_ISSUED], N_XBUF)
            for j, cp in enumerate(read_parts(first_slot, buf)):
                cp.start(priority=j % 2)
            st_ref[ST_ISSUED] = st_ref[ST_ISSUED] + 1

        st_ref[ST_CUR_E] = ce
        st_ref[ST_CUR_I] = ci + 1

    def wait_read(buf):
        for cp in read_parts(0, buf):
            cp.wait()

    def out_pieces(buf, first_slot, n, wait):
        for j, (pred, off, size) in enumerate(_chunk_pieces(n)):
            @pl.when(pred)
            def _(j=j, off=off, size=size):
                cp = _tile_copy(obuf.at[buf], off * PACK_ROWS, eo_ref, (first_slot + off) * PACK_ROWS,
                                wsem.at[buf], size, PACK_ROWS)
                cp.wait() if wait else cp.start(priority=j % 2)

    def drain(buf):
        out_pieces(buf, st_ref[ST_WRITE + 2 * buf], st_ref[ST_WRITE + 2 * buf + 1], True)
        st_ref[ST_WRITE + 2 * buf + 1] = 0

    @pl.when(e == 0)
    def _():
        for i in range(st_ref.shape[0]):
            st_ref[i] = 0
        for _ in range(N_XBUF - 1):
            read_next()

    def body(i, c):
        g = st_ref[ST_CHUNKS]
        buf = lax.rem(g, N_XBUF)
        wait_read(buf)
        read_next()
        obf = lax.rem(g, N_OBUF)
        drain(obf)
        first_slot = s0 + i * ch
        n = jnp.minimum(ch, cnt - i * ch)

        def swiglu(rows):
            xb = _load_packed(xbuf, rows, lead=(buf,), dtype=BF16)
            hid = _silu(_dot(xb, wg_ref[0].astype(BF16))) * _dot(xb, wu_ref[0].astype(BF16))
            _store_packed(obuf, _dot(hid.astype(BF16), wd_ref[0].astype(BF16)), lead=(obf,))

        pl.when(n > ch // 2)(lambda: swiglu(ch))
        pl.when((n <= ch // 2) & (n > ch // 4))(lambda: swiglu(ch // 2))
        pl.when(n <= ch // 4)(lambda: swiglu(ch // 4))
        out_pieces(obf, first_slot, n, False)
        st_ref[ST_WRITE + 2 * obf] = first_slot
        st_ref[ST_WRITE + 2 * obf + 1] = n
        st_ref[ST_CHUNKS] = g + 1
        return c

    lax.fori_loop(0, n_ch, body, 0)

    @pl.when(e == n_exp - 1)
    def _():
        for b in range(N_OBUF):
            drain(b)


def _experts(starts, counts, xs, wg, wu, wd):
    n_exp, d, d_e = wg.shape
    n_slot = xs.shape[0] // PACK_ROWS - EXPERT_CHUNK
    rows = EXPERT_CHUNK * PACK_ROWS
    grid_spec = pltpu.PrefetchScalarGridSpec(
        num_scalar_prefetch=2,
        grid=(n_exp,),
        in_specs=[pl.BlockSpec((1, d, d_e), lambda e, st, ct: (e, 0, 0)),
                  pl.BlockSpec((1, d, d_e), lambda e, st, ct: (e, 0, 0)),
                  pl.BlockSpec((1, d_e, d), lambda e, st, ct: (e, 0, 0)),
                  pl.BlockSpec(memory_space=pl.ANY)],
        out_specs=pl.BlockSpec(memory_space=pl.ANY),
        scratch_shapes=[pltpu.VMEM((N_XBUF, rows, LANES), jnp.uint32), pltpu.VMEM((N_OBUF, rows, LANES), jnp.uint32),
                        pltpu.SMEM((ST_WRITE + 2 * N_OBUF,), jnp.int32),
                        pltpu.SemaphoreType.DMA((N_XBUF,)), pltpu.SemaphoreType.DMA((N_OBUF,))],
    )
    return pl.pallas_call(
        _expert_kernel,
        grid_spec=grid_spec,
        out_shape=jax.ShapeDtypeStruct((n_slot * PACK_ROWS, LANES), jnp.uint32),
        compiler_params=_params(("arbitrary",)),
        name="experts",
    )(starts, counts, wg, wu, wd, xs)


def _combine_kernel(dest_ref, dnext_ref, wts_ref, h_ref, eo_ref, wgs_ref, wus_ref, wds_ref, g_ref, b_ref,
                    out_ref, gbuf, sem):
    i = pl.program_id(0)
    tm = out_ref.shape[0]
    buf = lax.rem(i, 2)

    def gather(d_ref, bf):
        def start(t, c):
            for k in range(TOP_K):
                _tile_copy(eo_ref, d_ref[t * TOP_K + k], gbuf.at[bf, k], t * PACK_ROWS, sem.at[bf],
                           1, PACK_ROWS).start(priority=k % 2)
            return c

        lax.fori_loop(0, tm, start, 0)

    @pl.when(i == 0)
    def _():
        gather(dest_ref, 0)

    def step(bf):
        for t in range(tm):
            for k in range(TOP_K):
                _tile_copy(eo_ref, dnext_ref[t * TOP_K + k], gbuf.at[1 - bf, k], t * PACK_ROWS, sem.at[1 - bf],
                           1, PACK_ROWS).start(priority=k % 2)
        hh = _load_token_tiles(h_ref, tm)
        hb = hh.astype(BF16)
        hid = _silu(_dot(hb, wgs_ref[...])) * _dot(hb, wus_ref[...])
        acc = DN_ALPHA * hh + _dot(hid.astype(BF16), wds_ref[...])
        w_t = jnp.concatenate([wts_ref[...], jnp.zeros((LANES - TOP_K, tm), F32)], axis=0).T
        for k in range(TOP_K):
            _tile_copy(eo_ref, 0, gbuf.at[bf, k], 0, sem.at[bf], tm, PACK_ROWS).wait()
        for k in range(TOP_K):
            acc = acc + _load_packed(gbuf, tm, lead=(bf, k)) * w_t[:, k:k + 1]
        out_ref[...] = _layer_norm(acc, g_ref[...], b_ref[...])

        @pl.when(i == pl.num_programs(0) - 1)
        def _():
            for k in range(TOP_K):
                _tile_copy(eo_ref, 0, gbuf.at[1 - bf, k], 0, sem.at[1 - bf], tm, PACK_ROWS).wait()

    for bf in range(2):
        pl.when(buf == bf)(lambda bf=bf: step(bf))


def _combine(dest_t, wts_t, h_tt, eo, wgs, wus, wds, g2, b2, tm):
    t = dest_t.shape[0] // TOP_K
    d = wgs.shape[0]
    const = lambda shape: pl.BlockSpec(shape, lambda i: (0, 0))
    n_tiles = t // tm
    return pl.pallas_call(
        _combine_kernel,
        grid=(n_tiles,),
        in_specs=[pl.BlockSpec((tm * TOP_K,), lambda i: (i,), memory_space=pltpu.SMEM),
                  pl.BlockSpec((tm * TOP_K,), lambda i: (jnp.minimum(i + 1, n_tiles - 1),), memory_space=pltpu.SMEM),
                  pl.BlockSpec((TOP_K, tm), lambda i: (0, i)),
                  pl.BlockSpec((tm * SUBLANES, LANES), lambda i: (i, 0)),
                  pl.BlockSpec(memory_space=pl.ANY),
                  const(wgs.shape), const(wus.shape), const(wds.shape), const((1, d)), const((1, d))],
        out_specs=pl.BlockSpec((tm, d), lambda i: (i, 0)),
        out_shape=jax.ShapeDtypeStruct((t, d), F32),
        scratch_shapes=[pltpu.VMEM((2, TOP_K, tm * PACK_ROWS, LANES), jnp.uint32), pltpu.SemaphoreType.DMA((2,))],
        compiler_params=_params(("arbitrary",)),
        name="combine",
    )(dest_t, dest_t, wts_t, h_tt, eo, wgs, wus, wds, g2, b2)


def _regroup_w_in(w):
    d = w.shape[0]
    splits = np.cumsum([GDN_QK, GDN_QK, GDN_VW, GDN_VW, GDN_HEADS, GDN_HEADS, RET_QK, RET_QK, RET_VW])
    qa, ka, va, za, ba, aa, qb, kb, vb, gb = jnp.split(w, splits.tolist(), axis=1)
    ba_blk = jnp.concatenate([ba, aa, jnp.zeros((d, LANES - 2 * GDN_HEADS), w.dtype)], axis=1)
    qb = qb.reshape(d, RET_HEADS, RET_DK // 2, 2)
    kb = kb.reshape(d, RET_HEADS, RET_DK // 2, 2)
    qkb = jnp.concatenate([qb[..., 0], qb[..., 1], kb[..., 0], kb[..., 1]], axis=-1).reshape(d, RET_HEADS * LANES)
    return jnp.concatenate([qa, ka, va, za, qkb, vb, gb, ba_blk], axis=1).astype(BF16)


def _rotary_tables(seq):
    inv = 1.0 / (ROPE_BASE ** jnp.linspace(0.0, 1.0, RET_DK // 2, dtype=F32))
    ang = jnp.arange(seq, dtype=F32)[:, None] * inv[None, :]
    c, s = jnp.cos(ang), jnp.sin(ang)
    ks = RET_DK ** -0.5
    rot_c = jnp.concatenate([c, c, c * ks, c * ks], axis=1)
    rot_s = jnp.concatenate([-s, s, -s * ks, s * ks], axis=1)
    return rot_c, rot_s


def _tile(n, pref):
    return pref if n % pref == 0 else n


def kernel(x, w_in, gdn_conv_w, gdn_a_log, gdn_dt_bias, gdn_norm_w, w_out, ln1_g, ln1_b, w_router, router_bias,
           w_gate_e, w_up_e, w_down_e, w_gate_s, w_up_s, w_down_s, ln2_g, ln2_b):
    batch, seq, d = x.shape
    t = batch * seq
    n_exp = w_router.shape[-1]
    hcur = x.reshape(t, d)
    for l in range(DEPTH):
        w_bf = _regroup_w_in(w_in[l])
        log_gamma = jnp.log(1.0 - 2.0 ** (-5.0 - jnp.arange(RET_HEADS, dtype=F32)))
        hp = jnp.stack([-jnp.exp(gdn_a_log[l].astype(F32)), gdn_dt_bias[l].astype(F32), log_gamma])
        rot_c, rot_s = _rotary_tables(seq)
        oa, ob = _mixer(hcur, w_bf, hp, gdn_conv_w[l].astype(F32), gdn_norm_w[l].reshape(1, -1).astype(F32),
                        rot_c, rot_s, batch, seq)
        hh, h_pk, idx_t, wts_t, counts = _post_mix(
            hcur, oa, ob, w_out[l].astype(BF16), ln1_g[l].reshape(1, d), ln1_b[l].reshape(1, d),
            w_router[l].T.astype(BF16), router_bias[l].reshape(n_exp, 1).astype(F32), _tile(t, POST_MIX_TM))
        base = jnp.cumsum(counts, axis=0) - counts
        dest_t = _dest(idx_t, base, _tile(t, DEST_TM)).T.reshape(t * TOP_K)
        xs = _dispatch(dest_t, h_pk, _tile(t, DISPATCH_TM))
        eo = _experts(base.reshape(n_exp).astype(jnp.int32), counts.reshape(n_exp).astype(jnp.int32),
                      xs, w_gate_e[l], w_up_e[l], w_down_e[l])
        hcur = _combine(dest_t, wts_t, hh, eo, w_gate_s[l].astype(BF16), w_up_s[l].astype(BF16),
                        w_down_s[l].astype(BF16), ln2_g[l].reshape(1, d), ln2_b[l].reshape(1, d), _tile(t, COMBINE_TM))
    return hcur.reshape(batch, seq, d)
```

```python
import math

import numpy as np
import jax
import jax.numpy as jnp
from jax import lax
from jax.experimental import pallas as pl
from jax.experimental.pallas import tpu as pltpu

GDN_HEADS = 4
GDN_DK = 128
GDN_DV = 128
GDN_CONV = 4
GDN_CHUNK = 64
RET_HEADS = 4
RET_DK = 64
RET_DV = 128
ROPE_BASE = 10000.0
N_GROUPS = 8
TOPK_GROUPS = 4
TOP_K = 8
ROUTED_SCALE = 2.5
DEPTH = 1
DN_ALPHA = (2.0 * DEPTH) ** 0.25
LN_EPS = 1e-5
NORM_EPS = 1e-6

GDN_QK = GDN_HEADS * GDN_DK
GDN_VW = GDN_HEADS * GDN_DV
RET_QK = RET_HEADS * RET_DK
RET_VW = RET_HEADS * RET_DV

LANES = 128
SUBLANES = 8
VMEM_LIMIT_BYTES = 56 * 1024 * 1024

SUPER = 256
NEG_INF = float("-inf")

COL_QA = 0
COL_KA = COL_QA + GDN_HEADS
COL_VA = COL_KA + GDN_HEADS
COL_ZA = COL_VA + GDN_HEADS
COL_QKB = COL_ZA + GDN_HEADS
COL_VB = COL_QKB + RET_HEADS
COL_GB = COL_VB + RET_HEADS
COL_BA = COL_GB + RET_HEADS
N_COLBLK = COL_BA + 1
PROJ_W = N_COLBLK * LANES
HEAD_W = GDN_HEADS * LANES
MIX_TS = 512
POST_MIX_TM = 512
DEST_TM = 512
DISPATCH_TM = 1024
COMBINE_TM = 256
EXPERT_CHUNK = 1024
EXPERT_PIECE = 128
PACK_ROWS = 4

BF16 = jnp.bfloat16
F32 = jnp.float32


def _dot(a, b):
    return jnp.dot(a, b, preferred_element_type=F32)


def _dot_nt(a, b):
    return lax.dot_general(a, b, (((1,), (1,)), ((), ())), preferred_element_type=F32)


def _silu(x):
    return x * jax.nn.sigmoid(x)


def _softplus(x):
    return jnp.maximum(x, 0.0) + jnp.log1p(jnp.exp(-jnp.abs(x)))


def _iota(shape, dim, dtype=jnp.int32):
    return lax.broadcasted_iota(dtype, shape, dim)


def _params(sem):
    return pltpu.CompilerParams(dimension_semantics=sem, vmem_limit_bytes=VMEM_LIMIT_BYTES)


def _load_token_tiles(ref, n, lead=()):
    parts = [ref[lead + (pl.ds(s, n, stride=SUBLANES), slice(None))] for s in range(SUBLANES)]
    return jnp.concatenate(parts, axis=1)


def _store_token_tiles(ref, val, lead=()):
    n = val.shape[0]
    for s in range(SUBLANES):
        ref[lead + (pl.ds(s, n, stride=SUBLANES), slice(None))] = val[:, s * LANES:(s + 1) * LANES]


def _tile_copy(src_ref, src_row, dst_ref, dst_row, sem, n_slots=1, slot_rows=SUBLANES):
    rows = n_slots * slot_rows
    aligned = lambda r: r if isinstance(r, int) else pl.multiple_of(r, slot_rows)
    return pltpu.make_async_copy(src_ref.at[pl.ds(aligned(src_row), rows)],
                                 dst_ref.at[pl.ds(aligned(dst_row), rows)], sem)


def _store_packed(ref, val, lead=()):
    n, width = val.shape
    bits = pltpu.bitcast(val.astype(BF16).astype(F32), jnp.uint32)
    words = (bits[:, width // 2:] & jnp.uint32(0xFFFF0000)) | (bits[:, :width // 2] >> 16)
    for s in range(PACK_ROWS):
        ref[lead + (pl.ds(s, n, stride=PACK_ROWS), slice(None))] = words[:, s * LANES:(s + 1) * LANES]


def _load_packed(ref, n, lead=(), dtype=F32):
    words = [ref[lead + (pl.ds(s, n, stride=PACK_ROWS), slice(None))] for s in range(PACK_ROWS)]
    lo = [pltpu.bitcast(w << 16, F32) for w in words]
    hi = [pltpu.bitcast(w & jnp.uint32(0xFFFF0000), F32) for w in words]
    return jnp.concatenate(lo + hi, axis=1).astype(dtype)


def _mixer_kernel(hp_ref,
                  x_ref, xn_ref,
                  w_ref, cw_ref, nw_ref, rc_ref, rs_ref,
                  oa_ref, ob_ref, proj_ref, st_ref, st2_ref):
    j = pl.program_id(1)
    ts = x_ref.shape[0]
    n_super = ts // SUPER
    n_chunk = SUPER // GDN_CHUNK

    def project(rows, slot):
        xb = rows.astype(BF16)
        step = 4 * LANES
        for c in range(0, PROJ_W, step):
            n = min(step, PROJ_W - c)
            proj_ref[slot, :, c:c + n] = _dot(xb, w_ref[:, c:c + n])

    @pl.when(j == 0)
    def _():
        st_ref[...] = jnp.zeros_like(st_ref)
        st2_ref[...] = jnp.zeros_like(st2_ref)

    @pl.when((j == 0) & (pl.program_id(0) == 0))
    def _():
        proj_ref[1] = jnp.zeros(proj_ref.shape[1:], F32)
        project(x_ref[0:SUPER, :], 0)

    ri = _iota((SUPER, SUPER), 0)
    ci = _iota((SUPER, SUPER), 1)
    shift = int(math.log2(GDN_CHUNK))
    same = (ri >> shift) == (ci >> shift)
    causal = same & (ci <= ri)
    strict = same & (ci < ri)
    causal_t = same & (ri <= ci)
    eye = jnp.where(ri == ci, 1.0, 0.0)
    lane = _iota((SUPER, LANES), 1)
    dpos = jnp.maximum((ri - ci).astype(F32), 0.0)
    lower = ri >= ci
    pos = _iota((SUPER, 1), 0).astype(F32)
    half = lane < RET_DK
    pair_lo = (lane % RET_DK) < (RET_DK // 2)
    lane1 = _iota((1, LANES), 1)
    neg_a_lane = jnp.zeros((1, LANES), F32)
    dt_b_lane = jnp.zeros((1, LANES), F32)
    for h in range(GDN_HEADS):
        neg_a_lane = jnp.where(lane1 == h + GDN_HEADS, hp_ref[0, h], neg_a_lane)
        dt_b_lane = jnp.where(lane1 == h + GDN_HEADS, hp_ref[1, h], dt_b_lane)

    def pcols(col_blk, h):
        return slice((col_blk + h) * LANES, (col_blk + h + 1) * LANES)

    def conv_silu(col_blk, h, slot, s, tail):
        cur = proj_ref[slot, :, pcols(col_blk, h)]
        prev = jnp.where((s == 0) & (j == 0), 0.0, tail[:, pcols(col_blk, h)])
        ext = jnp.concatenate([prev, cur], axis=0)
        acc = jnp.zeros((SUPER, LANES), F32)
        for jj in range(GDN_CONV):
            off = SUBLANES - (GDN_CONV - 1) + jj
            acc = acc + cw_ref[jj:jj + 1, pcols(col_blk, h)] * ext[off:off + SUPER, :]
        return _silu(acc)

    def l2n(u):
        return u * lax.rsqrt(jnp.sum(u * u, axis=-1, keepdims=True) + NORM_EPS)

    heads = range(GDN_HEADS)
    hcols = [slice(h * LANES, (h + 1) * LANES) for h in heads]

    def gdn_heads(r0, s, slot, tail):
        q = [l2n(conv_silu(COL_QA, h, slot, s, tail)) * (GDN_DK ** -0.5) for h in heads]
        k = [l2n(conv_silu(COL_KA, h, slot, s, tail)) for h in heads]
        v = [conv_silu(COL_VA, h, slot, s, tail) for h in heads]
        ba = proj_ref[slot, :, pcols(COL_BA, 0)]
        gates = jnp.where(lane < GDN_HEADS, jax.nn.sigmoid(ba), neg_a_lane * _softplus(ba + dt_b_lane))
        gates_t = gates.T
        beta, g_col, gl_col, decay_c, decay_s = [], [], [], [], []
        for h in heads:
            beta.append(jnp.sum(jnp.where(lane == h, gates, 0.0), axis=1, keepdims=True))
            la_col = jnp.sum(jnp.where(lane == h + GDN_HEADS, gates, 0.0), axis=1, keepdims=True)
            la_row = gates_t[h + GDN_HEADS:h + GDN_HEADS + 1, :]
            gc = jnp.sum(jnp.where(causal, la_row, 0.0), axis=1, keepdims=True)
            gr = jnp.sum(jnp.where(causal_t, la_col, 0.0), axis=0, keepdims=True)
            g_col.append(gc)
            gl_col.append(jnp.sum(jnp.where(same, la_row, 0.0), axis=1, keepdims=True))
            dc = jnp.exp(jnp.where(causal, gc - gr, NEG_INF))
            decay_c.append(dc)
            decay_s.append(jnp.where(strict, dc, 0.0))
        kb = [k[h] * beta[h] for h in heads]
        k_bf = [k[h].astype(BF16) for h in heads]
        low = [_dot_nt(kb[h].astype(BF16), k_bf[h]) * decay_s[h] for h in heads]
        p = [eye - low[h] for h in heads]
        sq = low
        for _ in range(shift - 1):
            sq_bf = [sq[h].astype(BF16) for h in heads]
            sq = [_dot(sq_bf[h], sq_bf[h]) for h in heads]
            p = [p[h] + _dot(p[h].astype(BF16), sq[h].astype(BF16)) for h in heads]
        eg = [jnp.exp(g_col[h]) for h in heads]
        rhs = [jnp.concatenate([v[h] * beta[h], kb[h] * eg[h]], axis=1).astype(BF16) for h in heads]
        wk = [_dot(p[h].astype(BF16), rhs[h]) for h in heads]
        w_val = [wk[h][:, :GDN_DV] for h in heads]
        k_cum = [wk[h][:, GDN_DV:].astype(BF16) for h in heads]
        attn = [(_dot_nt(q[h].astype(BF16), k_bf[h]) * decay_c[h]).astype(BF16) for h in heads]
        q_dec = [(q[h] * eg[h]).astype(BF16) for h in heads]
        kd_t = [(k[h] * jnp.exp(gl_col[h] - g_col[h])).T.astype(BF16) for h in heads]
        gl = [jnp.exp(gl_col[h]) for h in heads]
        st = [st_ref[h] for h in heads]
        outs = [[] for _ in heads]
        for c in range(n_chunk):
            lo = c * GDN_CHUNK
            hi = lo + GDN_CHUNK
            st_bf = [st[h].astype(BF16) for h in heads]
            v_new = [w_val[h][lo:hi] - _dot(k_cum[h][lo:hi], st_bf[h]) for h in heads]
            v_pad = []
            for h in heads:
                pieces = [jnp.zeros((GDN_CHUNK, GDN_DV), F32)] * n_chunk
                pieces[c] = v_new[h]
                v_pad.append(jnp.concatenate(pieces, axis=0).astype(BF16))
            for h in heads:
                outs[h].append(_dot(q_dec[h][lo:hi], st_bf[h]) + _dot(attn[h][lo:hi], v_pad[h]))
            st = [st[h] * gl[h][lo:lo + 1, :] + _dot(kd_t[h], v_pad[h]) for h in heads]
        for h in heads:
            st_ref[h] = st[h]
            o = jnp.concatenate(outs[h], axis=0)
            o = o * lax.rsqrt(jnp.mean(o * o, axis=-1, keepdims=True) + NORM_EPS) * nw_ref[...]
            z = proj_ref[slot, :, pcols(COL_ZA, h)]
            oa_ref[pl.ds(r0, SUPER), hcols[h]] = (o * _silu(z)).astype(oa_ref.dtype)

    def ret_heads(r0, slot):
        rc = rc_ref[pl.ds(r0, SUPER), :]
        rs = rs_ref[pl.ds(r0, SUPER), :]
        lg = [hp_ref[2, h] for h in heads]
        qm, km, vb = [], [], []
        for h in heads:
            x = proj_ref[slot, :, pcols(COL_QKB, h)]
            swapped = jnp.where(pair_lo, pltpu.roll(x, LANES - RET_DK // 2, 1), pltpu.roll(x, RET_DK // 2, 1))
            xr = x * rc + swapped * rs
            qm.append(jnp.where(half, xr, 0.0))
            km.append(jnp.where(half, pltpu.roll(xr, LANES - RET_DK, 1), 0.0))
            vb.append(proj_ref[slot, :, pcols(COL_VB, h)].astype(BF16))
        inner = [(_dot_nt(qm[h].astype(BF16), km[h].astype(BF16))
                  * jnp.where(lower, jnp.exp(lg[h] * dpos), 0.0)).astype(BF16) for h in heads]
        st2 = [st2_ref[h] for h in heads]
        ob = [_dot(inner[h], vb[h])
              + _dot((qm[h] * jnp.exp(lg[h] * (pos + 1.0))).astype(BF16), st2[h].astype(BF16)) for h in heads]
        for h in heads:
            g_chunk = jnp.exp(jnp.full((1, 1), SUPER, F32) * lg[h])
            st2_ref[h] = st2[h] * g_chunk + _dot((km[h] * jnp.exp(lg[h] * (SUPER - 1.0 - pos))).T.astype(BF16), vb[h])
        for h in heads:
            mu = jnp.mean(ob[h], axis=-1, keepdims=True)
            oc = ob[h] - mu
            oc = oc * lax.rsqrt(jnp.mean(oc * oc, axis=-1, keepdims=True) + NORM_EPS)
            gate = proj_ref[slot, :, pcols(COL_GB, h)]
            ob_ref[pl.ds(r0, SUPER), hcols[h]] = (oc * _silu(gate)).astype(ob_ref.dtype)

    for s in range(n_super):
        r0 = s * SUPER
        slot = s % 2
        tail = proj_ref[1 - slot, SUPER - SUBLANES:SUPER, 0:COL_ZA * LANES]
        nxt = x_ref[r0 + SUPER:r0 + 2 * SUPER, :] if s + 1 < n_super else xn_ref[...]
        project(nxt, 1 - slot)
        gdn_heads(r0, s, slot, tail)
        ret_heads(r0, slot)


def _mixer(x2, w_bf, hp, conv_w, norm_w, rot_c, rot_s, batch, seq):
    t, d = x2.shape
    ts = MIX_TS if seq % MIX_TS == 0 else seq
    nj = seq // ts
    per_blk = ts // SUPER
    assert per_blk % 2 == 0
    last_chunk = t // SUPER - 1
    const2 = lambda shape: pl.BlockSpec(shape, lambda b, j: (0, 0))
    tab = pl.BlockSpec((ts, LANES), lambda b, j: (j, 0))
    in_specs = [
        pl.BlockSpec(memory_space=pltpu.SMEM),
        pl.BlockSpec((ts, d), lambda b, j: (b * nj + j, 0)),
        pl.BlockSpec((SUPER, d), lambda b, j: (jnp.minimum((b * nj + j + 1) * per_blk, last_chunk), 0)),
        const2(w_bf.shape), const2(conv_w.shape), const2((1, GDN_DV)), tab, tab,
    ]
    out_spec = pl.BlockSpec((ts, HEAD_W), lambda b, j: (b * nj + j, 0))
    return pl.pallas_call(
        _mixer_kernel,
        grid=(batch, nj),
        in_specs=in_specs,
        out_specs=[out_spec, out_spec],
        out_shape=[jax.ShapeDtypeStruct((t, GDN_VW), BF16), jax.ShapeDtypeStruct((t, RET_VW), BF16)],
        scratch_shapes=[pltpu.VMEM((2, SUPER, PROJ_W), F32),
                        pltpu.VMEM((GDN_HEADS, GDN_DK, GDN_DV), F32),
                        pltpu.VMEM((RET_HEADS, LANES, RET_DV), F32)],
        compiler_params=_params(("arbitrary", "arbitrary")),
        name="mixer",
    )(hp, x2, x2, w_bf, conv_w, norm_w, rot_c, rot_s)


def _layer_norm(u, g, b):
    mu = jnp.mean(u, axis=-1, keepdims=True)
    uc = u - mu
    var = jnp.mean(uc * uc, axis=-1, keepdims=True)
    return uc * lax.rsqrt(var + LN_EPS) * g + b


def _post_mix_kernel(x_ref, oa_ref, ob_ref, wo_ref, g_ref, b_ref, wr_ref, rb_ref,
                     h_ref, hp_ref, idx_ref, wts_ref, cnt_ref):
    n_exp = wr_ref.shape[0]
    tm = x_ref.shape[0]
    per_grp = n_exp // N_GROUPS
    mix = _dot(oa_ref[...], wo_ref[:GDN_VW, :]) + _dot(ob_ref[...], wo_ref[GDN_VW:, :])
    hh = _layer_norm(DN_ALPHA * x_ref[...] + mix, g_ref[...], b_ref[...])
    _store_token_tiles(h_ref, hh)
    _store_packed(hp_ref, hh)
    scores = jax.nn.sigmoid(_dot_nt(wr_ref[...], hh.astype(BF16)))
    choice = scores + rb_ref[...]
    big = float(n_exp)
    io_g = _iota((per_grp, tm), 0).astype(F32)
    grp = []
    for g in range(N_GROUPS):
        blk = choice[g * per_grp:(g + 1) * per_grp, :]
        m1 = jnp.max(blk, axis=0, keepdims=True)
        i1 = jnp.min(jnp.where(blk == m1, io_g, big), axis=0, keepdims=True)
        m2 = jnp.max(jnp.where(io_g == i1, NEG_INF, blk), axis=0, keepdims=True)
        grp.append(m1 + m2)
    gsc = jnp.concatenate(grp, axis=0)
    io8 = _iota((N_GROUPS, tm), 0).astype(F32)
    gsel = jnp.zeros((N_GROUPS, tm), F32)
    for _ in range(TOPK_GROUPS):
        m = jnp.max(gsc, axis=0, keepdims=True)
        i = jnp.min(jnp.where(gsc == m, io8, big), axis=0, keepdims=True)
        hit = io8 == i
        gsel = jnp.where(hit, 1.0, gsel)
        gsc = jnp.where(hit, NEG_INF, gsc)
    masked = jnp.concatenate(
        [jnp.where(gsel[g:g + 1, :] > 0.0, choice[g * per_grp:(g + 1) * per_grp, :], NEG_INF)
         for g in range(N_GROUPS)], axis=0)
    io_e = _iota((n_exp, tm), 0).astype(F32)
    allowed = masked
    ids, ws = [], []
    for _ in range(TOP_K):
        m = jnp.max(masked, axis=0, keepdims=True)
        i = jnp.min(jnp.where(masked == m, io_e, big), axis=0, keepdims=True)
        hit = io_e == i
        ws.append(jnp.sum(jnp.where(hit, scores, 0.0), axis=0, keepdims=True))
        ids.append(i)
        masked = jnp.where(hit, NEG_INF, masked)
    sel = jnp.where(masked == NEG_INF, jnp.where(allowed == NEG_INF, 0.0, 1.0), 0.0)
    w = jnp.concatenate(ws, axis=0)
    wts_ref[...] = w / jnp.sum(w, axis=0, keepdims=True) * ROUTED_SCALE
    idx_ref[...] = jnp.concatenate(ids, axis=0).astype(jnp.int32)

    @pl.when(pl.program_id(0) == 0)
    def _():
        cnt_ref[...] = jnp.zeros_like(cnt_ref)

    cnt_ref[...] += jnp.sum(sel, axis=1, keepdims=True)


def _post_mix(x2, oa, ob, wo_bf, g1, b1, wr_t, rbias, tm):
    t, d = x2.shape
    n_exp = wr_t.shape[0]
    row = lambda w: pl.BlockSpec((tm, w), lambda i: (i, 0))
    const = lambda shape: pl.BlockSpec(shape, lambda i: (0, 0))
    tok = pl.BlockSpec((TOP_K, tm), lambda i: (0, i))
    return pl.pallas_call(
        _post_mix_kernel,
        grid=(t // tm,),
        in_specs=[row(d), row(GDN_VW), row(RET_VW), const(wo_bf.shape), const((1, d)), const((1, d)),
                  const(wr_t.shape), const((n_exp, 1))],
        out_specs=[pl.BlockSpec((tm * SUBLANES, LANES), lambda i: (i, 0)),
                   pl.BlockSpec((tm * PACK_ROWS, LANES), lambda i: (i, 0)), tok, tok, const((n_exp, 1))],
        out_shape=[jax.ShapeDtypeStruct((t * SUBLANES, LANES), F32),
                   jax.ShapeDtypeStruct((t * PACK_ROWS, LANES), jnp.uint32),
                   jax.ShapeDtypeStruct((TOP_K, t), jnp.int32),
                   jax.ShapeDtypeStruct((TOP_K, t), F32),
                   jax.ShapeDtypeStruct((n_exp, 1), F32)],
        compiler_params=_params(("arbitrary",)),
        name="post_mix",
    )(x2, oa, ob, wo_bf, g1, b1, wr_t, rbias)


def _dest_kernel(idx_ref, base_ref, dest_ref, carry_ref):
    n_exp = base_ref.shape[0]
    tm = idx_ref.shape[1]

    @pl.when(pl.program_id(0) == 0)
    def _():
        carry_ref[...] = jnp.zeros_like(carry_ref)

    io_e = _iota((n_exp, tm), 0)
    idx = idx_ref[...]
    hits = [io_e == idx[k:k + 1, :] for k in range(TOP_K)]
    sel = jnp.zeros((n_exp, tm), F32)
    for hit in hits:
        sel = jnp.where(hit, 1.0, sel)
    before = jnp.where(_iota((tm, tm), 0) < _iota((tm, tm), 1), 1.0, 0.0).astype(BF16)
    rank = _dot(sel.astype(BF16), before)
    tot = base_ref[...] + carry_ref[...] + rank
    dest = [jnp.sum(jnp.where(hit, tot, 0.0), axis=0, keepdims=True) for hit in hits]
    dest_ref[...] = jnp.concatenate(dest, axis=0).astype(jnp.int32) * PACK_ROWS
    carry_ref[...] += jnp.sum(sel, axis=1, keepdims=True)


def _dest(idx_t, base, tm):
    t = idx_t.shape[1]
    n_exp = base.shape[0]
    tok = pl.BlockSpec((TOP_K, tm), lambda i: (0, i))
    return pl.pallas_call(
        _dest_kernel,
        grid=(t // tm,),
        in_specs=[tok, pl.BlockSpec((n_exp, 1), lambda i: (0, 0))],
        out_specs=tok,
        out_shape=jax.ShapeDtypeStruct((TOP_K, t), jnp.int32),
        scratch_shapes=[pltpu.VMEM((n_exp, 1), F32)],
        compiler_params=_params(("arbitrary",)),
        name="dest",
    )(idx_t, base)


def _dispatch_kernel(dest_ref, h_ref, xs_ref, zero_ref, sem, zsem):
    tm = h_ref.shape[0] // PACK_ROWS
    last = pl.num_programs(0) - 1
    pad_row = xs_ref.shape[0] - zero_ref.shape[0]

    @pl.when(pl.program_id(0) == last)
    def _():
        zero_ref[...] = jnp.zeros_like(zero_ref)
        _tile_copy(zero_ref, 0, xs_ref, pad_row, zsem, EXPERT_CHUNK, PACK_ROWS).start()

    def start(t, c):
        for k in range(TOP_K):
            _tile_copy(h_ref, t * PACK_ROWS, xs_ref, dest_ref[t * TOP_K + k], sem, 1, PACK_ROWS).start(priority=k % 2)
        return c

    lax.fori_loop(0, tm, start, 0)

    for k in range(TOP_K):
        _tile_copy(h_ref, 0, xs_ref, 0, sem, tm, PACK_ROWS).wait()

    @pl.when(pl.program_id(0) == last)
    def _():
        _tile_copy(zero_ref, 0, xs_ref, pad_row, zsem, EXPERT_CHUNK, PACK_ROWS).wait()


def _dispatch(dest_t, h_pk, tm):
    t = dest_t.shape[0] // TOP_K
    n_rows = (t * TOP_K + EXPERT_CHUNK) * PACK_ROWS
    return pl.pallas_call(
        _dispatch_kernel,
        grid=(t // tm,),
        in_specs=[pl.BlockSpec((tm * TOP_K,), lambda i: (i,), memory_space=pltpu.SMEM),
                  pl.BlockSpec((tm * PACK_ROWS, LANES), lambda i: (i, 0))],
        out_specs=pl.BlockSpec(memory_space=pl.ANY),
        out_shape=jax.ShapeDtypeStruct((n_rows, LANES), jnp.uint32),
        scratch_shapes=[pltpu.VMEM((EXPERT_CHUNK * PACK_ROWS, LANES), jnp.uint32),
                        pltpu.SemaphoreType.DMA(()), pltpu.SemaphoreType.DMA(())],
        compiler_params=_params(("arbitrary",)),
        name="dispatch",
    )(dest_t, h_pk)


def _chunk_pieces(n):
    pieces = [(j < n // EXPERT_PIECE, j * EXPERT_PIECE, EXPERT_PIECE) for j in range(EXPERT_CHUNK // EXPERT_PIECE)]
    size = EXPERT_PIECE // 2
    while size >= 1:
        shift = int(math.log2(size)) + 1
        pieces.append(((n & size) != 0, (n >> shift) << shift, size))
        size //= 2
    return pieces


ST_CHUNKS = 0
ST_ISSUED = 1
ST_CUR_E = 2
ST_CUR_I = 3
ST_WRITE = 4
N_XBUF = 4
N_OBUF = 3


def _expert_kernel(start_ref, count_ref, wg_ref, wu_ref, wd_ref, xs_ref, eo_ref,
                   xbuf, obuf, st_ref, rsem, wsem):
    e = pl.program_id(0)
    n_exp = pl.num_programs(0)
    ch = EXPERT_CHUNK
    s0 = start_ref[e]
    cnt = count_ref[e]
    n_ch = (cnt + ch - 1) // ch

    def read_parts(first_slot, buf):
        return [_tile_copy(xs_ref, (first_slot + j * EXPERT_PIECE) * PACK_ROWS, xbuf.at[buf],
                           j * EXPERT_PIECE * PACK_ROWS, rsem.at[buf], EXPERT_PIECE, PACK_ROWS)
                for j in range(ch // EXPERT_PIECE)]

    def read_next():
        def exhausted(c):
            cnt_c = count_ref[jnp.minimum(c[0], n_exp - 1)]
            return jnp.logical_and(c[0] < n_exp, c[1] * ch >= cnt_c)

        ce, ci = lax.while_loop(exhausted, lambda c: (c[0] + 1, jnp.int32(0)),
                                (st_ref[ST_CUR_E], st_ref[ST_CUR_I]))

        @pl.when(ce < n_exp)
        def _():
            first_slot = start_ref[jnp.minimum(ce, n_exp - 1)] + ci * ch
            buf = lax.rem(st_ref[ST_ISSUED], N_XBUF)
            for j, cp in enumerate(read_parts(first_slot, buf)):
                cp.start(priority=j % 2)
            st_ref[ST_ISSUED] = st_ref[ST_ISSUED] + 1

        st_ref[ST_CUR_E] = ce
        st_ref[ST_CUR_I] = ci + 1

    def wait_read(buf):
        for cp in read_parts(0, buf):
            cp.wait()

    def out_pieces(buf, first_slot, n, wait):
        for j, (pred, off, size) in enumerate(_chunk_pieces(n)):
            @pl.when(pred)
            def _(j=j, off=off, size=size):
                cp = _tile_copy(obuf.at[buf], off * PACK_ROWS, eo_ref, (first_slot + off) * PACK_ROWS,
                                wsem.at[buf], size, PACK_ROWS)
                cp.wait() if wait else cp.start(priority=j % 2)

    def drain(buf):
        out_pieces(buf, st_ref[ST_WRITE + 2 * buf], st_ref[ST_WRITE + 2 * buf + 1], True)
        st_ref[ST_WRITE + 2 * buf + 1] = 0

    @pl.when(e == 0)
    def _():
        for i in range(st_ref.shape[0]):
            st_ref[i] = 0
        for _ in range(N_XBUF - 1):
            read_next()

    def body(i, c):
        g = st_ref[ST_CHUNKS]
        buf = lax.rem(g, N_XBUF)
        wait_read(buf)
        read_next()
        obf = lax.rem(g, N_OBUF)
        drain(obf)
        first_slot = s0 + i * ch
        n = jnp.minimum(ch, cnt - i * ch)

        def swiglu(rows):
            xb = _load_packed(xbuf, rows, lead=(buf,), dtype=BF16)
            hid = _silu(_dot(xb, wg_ref[0].astype(BF16))) * _dot(xb, wu_ref[0].astype(BF16))
            _store_packed(obuf, _dot(hid.astype(BF16), wd_ref[0].astype(BF16)), lead=(obf,))

        pl.when(n > ch // 2)(lambda: swiglu(ch))
        pl.when((n <= ch // 2) & (n > ch // 4))(lambda: swiglu(ch // 2))
        pl.when(n <= ch // 4)(lambda: swiglu(ch // 4))
        out_pieces(obf, first_slot, n, False)
        st_ref[ST_WRITE + 2 * obf] = first_slot
        st_ref[ST_WRITE + 2 * obf + 1] = n
        st_ref[ST_CHUNKS] = g + 1
        return c

    lax.fori_loop(0, n_ch, body, 0)

    @pl.when(e == n_exp - 1)
    def _():
        for b in range(N_OBUF):
            drain(b)


def _experts(starts, counts, xs, wg, wu, wd):
    n_exp, d, d_e = wg.shape
    n_slot = xs.shape[0] // PACK_ROWS - EXPERT_CHUNK
    rows = EXPERT_CHUNK * PACK_ROWS
    grid_spec = pltpu.PrefetchScalarGridSpec(
        num_scalar_prefetch=2,
        grid=(n_exp,),
        in_specs=[pl.BlockSpec((1, d, d_e), lambda e, st, ct: (e, 0, 0)),
                  pl.BlockSpec((1, d, d_e), lambda e, st, ct: (e, 0, 0)),
                  pl.BlockSpec((1, d_e, d), lambda e, st, ct: (e, 0, 0)),
                  pl.BlockSpec(memory_space=pl.ANY)],
        out_specs=pl.BlockSpec(memory_space=pl.ANY),
        scratch_shapes=[pltpu.VMEM((N_XBUF, rows, LANES), jnp.uint32), pltpu.VMEM((N_OBUF, rows, LANES), jnp.uint32),
                        pltpu.SMEM((ST_WRITE + 2 * N_OBUF,), jnp.int32),
                        pltpu.SemaphoreType.DMA((N_XBUF,)), pltpu.SemaphoreType.DMA((N_OBUF,))],
    )
    return pl.pallas_call(
        _expert_kernel,
        grid_spec=grid_spec,
        out_shape=jax.ShapeDtypeStruct((n_slot * PACK_ROWS, LANES), jnp.uint32),
        compiler_params=_params(("arbitrary",)),
        name="experts",
    )(starts, counts, wg, wu, wd, xs)


def _combine_kernel(dest_ref, dnext_ref, wts_ref, h_ref, eo_ref, wgs_ref, wus_ref, wds_ref, g_ref, b_ref,
                    out_ref, gbuf, sem):
    i = pl.program_id(0)
    tm = out_ref.shape[0]
    buf = lax.rem(i, 2)

    def gather(d_ref, bf):
        def start(t, c):
            for k in range(TOP_K):
                _tile_copy(eo_ref, d_ref[t * TOP_K + k], gbuf.at[bf, k], t * PACK_ROWS, sem.at[bf],
                           1, PACK_ROWS).start(priority=k % 2)
            return c

        lax.fori_loop(0, tm, start, 0)

    @pl.when(i == 0)
    def _():
        gather(dest_ref, 0)

    def step(bf):
        for k in range(TOP_K):
            _tile_copy(eo_ref, 0, gbuf.at[bf, k], 0, sem.at[bf], tm, PACK_ROWS).wait()
        w_t = jnp.concatenate([wts_ref[...], jnp.zeros((LANES - TOP_K, tm), F32)], axis=0).T
        hh = _load_token_tiles(h_ref, tm)
        acc = DN_ALPHA * hh
        for k in range(TOP_K):
            acc = acc + _load_packed(gbuf, tm, lead=(bf, k)) * w_t[:, k:k + 1]
        wgs, wus, wds, gain, bias = wgs_ref[...], wus_ref[...], wds_ref[...], g_ref[...], b_ref[...]
        for t in range(tm):
            for k in range(TOP_K):
                _tile_copy(eo_ref, dnext_ref[t * TOP_K + k], gbuf.at[1 - bf, k], t * PACK_ROWS, sem.at[1 - bf],
                           1, PACK_ROWS).start(priority=k % 2)
        hb = hh.astype(BF16)
        hid = _silu(_dot(hb, wgs)) * _dot(hb, wus)
        acc = acc + _dot(hid.astype(BF16), wds)
        out_ref[...] = _layer_norm(acc, gain, bias)

        @pl.when(i == pl.num_programs(0) - 1)
        def _():
            for k in range(TOP_K):
                _tile_copy(eo_ref, 0, gbuf.at[1 - bf, k], 0, sem.at[1 - bf], tm, PACK_ROWS).wait()

    for bf in range(2):
        pl.when(buf == bf)(lambda bf=bf: step(bf))


def _combine(dest_t, wts_t, h_tt, eo, wgs, wus, wds, g2, b2, tm):
    t = dest_t.shape[0] // TOP_K
    d = wgs.shape[0]
    const = lambda shape: pl.BlockSpec(shape, lambda i: (0, 0))
    n_tiles = t // tm
    return pl.pallas_call(
        _combine_kernel,
        grid=(n_tiles,),
        in_specs=[pl.BlockSpec((tm * TOP_K,), lambda i: (i,), memory_space=pltpu.SMEM),
                  pl.BlockSpec((tm * TOP_K,), lambda i: (jnp.minimum(i + 1, n_tiles - 1),), memory_space=pltpu.SMEM),
                  pl.BlockSpec((TOP_K, tm), lambda i: (0, i)),
                  pl.BlockSpec((tm * SUBLANES, LANES), lambda i: (i, 0)),
                  pl.BlockSpec(memory_space=pl.ANY),
                  const(wgs.shape), const(wus.shape), const(wds.shape), const((1, d)), const((1, d))],
        out_specs=pl.BlockSpec((tm, d), lambda i: (i, 0)),
        out_shape=jax.ShapeDtypeStruct((t, d), F32),
        scratch_shapes=[pltpu.VMEM((2, TOP_K, tm * PACK_ROWS, LANES), jnp.uint32), pltpu.SemaphoreType.DMA((2,))],
        compiler_params=_params(("arbitrary",)),
        name="combine",
    )(dest_t, dest_t, wts_t, h_tt, eo, wgs, wus, wds, g2, b2)


def _regroup_w_in(w):
    d = w.shape[0]
    splits = np.cumsum([GDN_QK, GDN_QK, GDN_VW, GDN_VW, GDN_HEADS, GDN_HEADS, RET_QK, RET_QK, RET_VW])
    qa, ka, va, za, ba, aa, qb, kb, vb, gb = jnp.split(w, splits.tolist(), axis=1)
    ba_blk = jnp.concatenate([ba, aa, jnp.zeros((d, LANES - 2 * GDN_HEADS), w.dtype)], axis=1)
    qb = qb.reshape(d, RET_HEADS, RET_DK // 2, 2)
    kb = kb.reshape(d, RET_HEADS, RET_DK // 2, 2)
    qkb = jnp.concatenate([qb[..., 0], qb[..., 1], kb[..., 0], kb[..., 1]], axis=-1).reshape(d, RET_HEADS * LANES)
    return jnp.concatenate([qa, ka, va, za, qkb, vb, gb, ba_blk], axis=1).astype(BF16)


def _rotary_tables(seq):
    inv = 1.0 / (ROPE_BASE ** jnp.linspace(0.0, 1.0, RET_DK // 2, dtype=F32))
    ang = jnp.arange(seq, dtype=F32)[:, None] * inv[None, :]
    c, s = jnp.cos(ang), jnp.sin(ang)
    ks = RET_DK ** -0.5
    rot_c = jnp.concatenate([c, c, c * ks, c * ks], axis=1)
    rot_s = jnp.concatenate([-s, s, -s * ks, s * ks], axis=1)
    return rot_c, rot_s


def _tile(n, pref):
    return pref if n % pref == 0 else n


def kernel(x, w_in, gdn_conv_w, gdn_a_log, gdn_dt_bias, gdn_norm_w, w_out, ln1_g, ln1_b, w_router, router_bias,
           w_gate_e, w_up_e, w_down_e, w_gate_s, w_up_s, w_down_s, ln2_g, ln2_b):
    batch, seq, d = x.shape
    t = batch * seq
    n_exp = w_router.shape[-1]
    hcur = x.reshape(t, d)
    for l in range(DEPTH):
        w_bf = _regroup_w_in(w_in[l])
        log_gamma = jnp.log(1.0 - 2.0 ** (-5.0 - jnp.arange(RET_HEADS, dtype=F32)))
        hp = jnp.stack([-jnp.exp(gdn_a_log[l].astype(F32)), gdn_dt_bias[l].astype(F32), log_gamma])
        rot_c, rot_s = _rotary_tables(seq)
        oa, ob = _mixer(hcur, w_bf, hp, gdn_conv_w[l].astype(F32), gdn_norm_w[l].reshape(1, -1).astype(F32),
                        rot_c, rot_s, batch, seq)
        hh, h_pk, idx_t, wts_t, counts = _post_mix(
            hcur, oa, ob, w_out[l].astype(BF16), ln1_g[l].reshape(1, d), ln1_b[l].reshape(1, d),
            w_router[l].T.astype(BF16), router_bias[l].reshape(n_exp, 1).astype(F32), _tile(t, POST_MIX_TM))
        base = jnp.cumsum(counts, axis=0) - counts
        dest_t = _dest(idx_t, base, _tile(t, DEST_TM)).T.reshape(t * TOP_K)
        xs = _dispatch(dest_t, h_pk, _tile(t, DISPATCH_TM))
        eo = _experts(base.reshape(n_exp).astype(jnp.int32), counts.reshape(n_exp).astype(jnp.int32),
                      xs, w_gate_e[l], w_up_e[l], w_down_e[l])
        hcur = _combine(dest_t, wts_t, hh, eo, w_gate_s[l].astype(BF16), w_up_s[l].astype(BF16),
                        w_down_s[l].astype(BF16), ln2_g[l].reshape(1, d), ln2_b[l].reshape(1, d), _tile(t, COMBINE_TM))
    return hcur.reshape(batch, seq, d)
```

```python
import math

import numpy as np
import jax
import jax.numpy as jnp
from jax import lax
from jax.experimental import pallas as pl
from jax.experimental.pallas import tpu as pltpu

GDN_HEADS = 4
GDN_DK = 128
GDN_DV = 128
GDN_CONV = 4
GDN_CHUNK = 64
RET_HEADS = 4
RET_DK = 64
RET_DV = 128
ROPE_BASE = 10000.0
N_GROUPS = 8
TOPK_GROUPS = 4
TOP_K = 8
ROUTED_SCALE = 2.5
DEPTH = 1
DN_ALPHA = (2.0 * DEPTH) ** 0.25
LN_EPS = 1e-5
NORM_EPS = 1e-6

GDN_QK = GDN_HEADS * GDN_DK
GDN_VW = GDN_HEADS * GDN_DV
RET_QK = RET_HEADS * RET_DK
RET_VW = RET_HEADS * RET_DV

LANES = 128
SUBLANES = 8
VMEM_LIMIT_BYTES = 56 * 1024 * 1024

SUPER = 256
NEG_INF = float("-inf")

COL_QA = 0
COL_KA = COL_QA + GDN_HEADS
COL_VA = COL_KA + GDN_HEADS
COL_ZA = COL_VA + GDN_HEADS
COL_QKB = COL_ZA + GDN_HEADS
COL_VB = COL_QKB + RET_HEADS
COL_GB = COL_VB + RET_HEADS
COL_BA = COL_GB + RET_HEADS
N_COLBLK = COL_BA + 1
PROJ_W = N_COLBLK * LANES
HEAD_W = GDN_HEADS * LANES
MIX_TS = 512
POST_MIX_TM = 512
DEST_TM = 512
DISPATCH_TM = 1024
COMBINE_TM = 256
EXPERT_CHUNK = 1024
EXPERT_PIECE = 128
PACK_ROWS = 4

BF16 = jnp.bfloat16
F32 = jnp.float32


def _dot(a, b):
    return jnp.dot(a, b, preferred_element_type=F32)


def _dot_nt(a, b):
    return lax.dot_general(a, b, (((1,), (1,)), ((), ())), preferred_element_type=F32)


def _silu(x):
    return x * jax.nn.sigmoid(x)


def _softplus(x):
    return jnp.maximum(x, 0.0) + jnp.log1p(jnp.exp(-jnp.abs(x)))


def _iota(shape, dim, dtype=jnp.int32):
    return lax.broadcasted_iota(dtype, shape, dim)


def _params(sem):
    return pltpu.CompilerParams(dimension_semantics=sem, vmem_limit_bytes=VMEM_LIMIT_BYTES)


def _tile_copy(src_ref, src_row, dst_ref, dst_row, sem, n_slots=1, slot_rows=SUBLANES):
    rows = n_slots * slot_rows
    aligned = lambda r: r if isinstance(r, int) else pl.multiple_of(r, slot_rows)
    return pltpu.make_async_copy(src_ref.at[pl.ds(aligned(src_row), rows)],
                                 dst_ref.at[pl.ds(aligned(dst_row), rows)], sem)


def _store_packed(ref, val, lead=()):
    n, width = val.shape
    bits = pltpu.bitcast(val.astype(BF16).astype(F32), jnp.uint32)
    words = (bits[:, width // 2:] & jnp.uint32(0xFFFF0000)) | (bits[:, :width // 2] >> 16)
    for s in range(PACK_ROWS):
        ref[lead + (pl.ds(s, n, stride=PACK_ROWS), slice(None))] = words[:, s * LANES:(s + 1) * LANES]


def _load_packed(ref, n, lead=(), dtype=F32):
    words = [ref[lead + (pl.ds(s, n, stride=PACK_ROWS), slice(None))] for s in range(PACK_ROWS)]
    lo = [pltpu.bitcast(w << 16, F32) for w in words]
    hi = [pltpu.bitcast(w & jnp.uint32(0xFFFF0000), F32) for w in words]
    return jnp.concatenate(lo + hi, axis=1).astype(dtype)


def _mixer_kernel(hp_ref,
                  x_ref, xn_ref,
                  w_ref, cw_ref, nw_ref, rc_ref, rs_ref,
                  oa_ref, ob_ref, proj_ref, st_ref, st2_ref):
    j = pl.program_id(1)
    ts = x_ref.shape[0]
    n_super = ts // SUPER
    n_chunk = SUPER // GDN_CHUNK

    def project(rows, slot):
        xb = rows.astype(BF16)
        step = 4 * LANES
        for c in range(0, PROJ_W, step):
            n = min(step, PROJ_W - c)
            proj_ref[slot, :, c:c + n] = _dot(xb, w_ref[:, c:c + n])

    @pl.when(j == 0)
    def _():
        st_ref[...] = jnp.zeros_like(st_ref)
        st2_ref[...] = jnp.zeros_like(st2_ref)

    @pl.when((j == 0) & (pl.program_id(0) == 0))
    def _():
        proj_ref[1] = jnp.zeros(proj_ref.shape[1:], F32)
        project(x_ref[0:SUPER, :], 0)

    ri = _iota((SUPER, SUPER), 0)
    ci = _iota((SUPER, SUPER), 1)
    shift = int(math.log2(GDN_CHUNK))
    same = (ri >> shift) == (ci >> shift)
    causal = same & (ci <= ri)
    strict = same & (ci < ri)
    causal_t = same & (ri <= ci)
    eye = jnp.where(ri == ci, 1.0, 0.0)
    lane = _iota((SUPER, LANES), 1)
    dpos = jnp.maximum((ri - ci).astype(F32), 0.0)
    lower = ri >= ci
    pos = _iota((SUPER, 1), 0).astype(F32)
    half = lane < RET_DK
    pair_lo = (lane % RET_DK) < (RET_DK // 2)
    lane1 = _iota((1, LANES), 1)
    neg_a_lane = jnp.zeros((1, LANES), F32)
    dt_b_lane = jnp.zeros((1, LANES), F32)
    for h in range(GDN_HEADS):
        neg_a_lane = jnp.where(lane1 == h + GDN_HEADS, hp_ref[0, h], neg_a_lane)
        dt_b_lane = jnp.where(lane1 == h + GDN_HEADS, hp_ref[1, h], dt_b_lane)

    def pcols(col_blk, h):
        return slice((col_blk + h) * LANES, (col_blk + h + 1) * LANES)

    def conv_silu(col_blk, h, slot, s, tail):
        cur = proj_ref[slot, :, pcols(col_blk, h)]
        prev = jnp.where((s == 0) & (j == 0), 0.0, tail[:, pcols(col_blk, h)])
        ext = jnp.concatenate([prev, cur], axis=0)
        acc = jnp.zeros((SUPER, LANES), F32)
        for jj in range(GDN_CONV):
            off = SUBLANES - (GDN_CONV - 1) + jj
            acc = acc + cw_ref[jj:jj + 1, pcols(col_blk, h)] * ext[off:off + SUPER, :]
        return _silu(acc)

    def l2n(u):
        return u * lax.rsqrt(jnp.sum(u * u, axis=-1, keepdims=True) + NORM_EPS)

    heads = range(GDN_HEADS)
    hcols = [slice(h * LANES, (h + 1) * LANES) for h in heads]

    def gdn_heads(r0, s, slot, tail):
        q = [l2n(conv_silu(COL_QA, h, slot, s, tail)) * (GDN_DK ** -0.5) for h in heads]
        k = [l2n(conv_silu(COL_KA, h, slot, s, tail)) for h in heads]
        v = [conv_silu(COL_VA, h, slot, s, tail) for h in heads]
        ba = proj_ref[slot, :, pcols(COL_BA, 0)]
        gates = jnp.where(lane < GDN_HEADS, jax.nn.sigmoid(ba), neg_a_lane * _softplus(ba + dt_b_lane))
        gates_t = gates.T
        beta, g_col, gl_col, decay_c, decay_s = [], [], [], [], []
        for h in heads:
            beta.append(jnp.sum(jnp.where(lane == h, gates, 0.0), axis=1, keepdims=True))
            la_col = jnp.sum(jnp.where(lane == h + GDN_HEADS, gates, 0.0), axis=1, keepdims=True)
            la_row = gates_t[h + GDN_HEADS:h + GDN_HEADS + 1, :]
            gc = jnp.sum(jnp.where(causal, la_row, 0.0), axis=1, keepdims=True)
            gr = jnp.sum(jnp.where(causal_t, la_col, 0.0), axis=0, keepdims=True)
            g_col.append(gc)
            gl_col.append(jnp.sum(jnp.where(same, la_row, 0.0), axis=1, keepdims=True))
            dc = jnp.exp(jnp.where(causal, gc - gr, NEG_INF))
            decay_c.append(dc)
            decay_s.append(jnp.where(strict, dc, 0.0))
        kb = [k[h] * beta[h] for h in heads]
        k_bf = [k[h].astype(BF16) for h in heads]
        low = [_dot_nt(kb[h].astype(BF16), k_bf[h]) * decay_s[h] for h in heads]
        p = [eye - low[h] for h in heads]
        sq = low
        for _ in range(shift - 1):
            sq_bf = [sq[h].astype(BF16) for h in heads]
            sq = [_dot(sq_bf[h], sq_bf[h]) for h in heads]
            p = [p[h] + _dot(p[h].astype(BF16), sq[h].astype(BF16)) for h in heads]
        eg = [jnp.exp(g_col[h]) for h in heads]
        rhs = [jnp.concatenate([v[h] * beta[h], kb[h] * eg[h]], axis=1).astype(BF16) for h in heads]
        wk = [_dot(p[h].astype(BF16), rhs[h]) for h in heads]
        w_val = [wk[h][:, :GDN_DV] for h in heads]
        k_cum = [wk[h][:, GDN_DV:].astype(BF16) for h in heads]
        attn = [(_dot_nt(q[h].astype(BF16), k_bf[h]) * decay_c[h]).astype(BF16) for h in heads]
        q_dec = [(q[h] * eg[h]).astype(BF16) for h in heads]
        kd_t = [(k[h] * jnp.exp(gl_col[h] - g_col[h])).T.astype(BF16) for h in heads]
        gl = [jnp.exp(gl_col[h]) for h in heads]
        st = [st_ref[h] for h in heads]
        outs = [[] for _ in heads]
        for c in range(n_chunk):
            lo = c * GDN_CHUNK
            hi = lo + GDN_CHUNK
            st_bf = [st[h].astype(BF16) for h in heads]
            v_new = [w_val[h][lo:hi] - _dot(k_cum[h][lo:hi], st_bf[h]) for h in heads]
            v_pad = []
            for h in heads:
                pieces = [jnp.zeros((GDN_CHUNK, GDN_DV), F32)] * n_chunk
                pieces[c] = v_new[h]
                v_pad.append(jnp.concatenate(pieces, axis=0).astype(BF16))
            for h in heads:
                outs[h].append(_dot(q_dec[h][lo:hi], st_bf[h]) + _dot(attn[h][lo:hi], v_pad[h]))
            st = [st[h] * gl[h][lo:lo + 1, :] + _dot(kd_t[h], v_pad[h]) for h in heads]
        for h in heads:
            st_ref[h] = st[h]
            o = jnp.concatenate(outs[h], axis=0)
            o = o * lax.rsqrt(jnp.mean(o * o, axis=-1, keepdims=True) + NORM_EPS) * nw_ref[...]
            z = proj_ref[slot, :, pcols(COL_ZA, h)]
            oa_ref[pl.ds(r0, SUPER), hcols[h]] = (o * _silu(z)).astype(oa_ref.dtype)

    def ret_heads(r0, slot):
        rc = rc_ref[pl.ds(r0, SUPER), :]
        rs = rs_ref[pl.ds(r0, SUPER), :]
        lg = [hp_ref[2, h] for h in heads]
        qm, km, vb = [], [], []
        for h in heads:
            x = proj_ref[slot, :, pcols(COL_QKB, h)]
            swapped = jnp.where(pair_lo, pltpu.roll(x, LANES - RET_DK // 2, 1), pltpu.roll(x, RET_DK // 2, 1))
            xr = x * rc + swapped * rs
            qm.append(jnp.where(half, xr, 0.0))
            km.append(jnp.where(half, pltpu.roll(xr, LANES - RET_DK, 1), 0.0))
            vb.append(proj_ref[slot, :, pcols(COL_VB, h)].astype(BF16))
        inner = [(_dot_nt(qm[h].astype(BF16), km[h].astype(BF16))
                  * jnp.where(lower, jnp.exp(lg[h] * dpos), 0.0)).astype(BF16) for h in heads]
        st2 = [st2_ref[h] for h in heads]
        ob = [_dot(inner[h], vb[h])
              + _dot((qm[h] * jnp.exp(lg[h] * (pos + 1.0))).astype(BF16), st2[h].astype(BF16)) for h in heads]
        for h in heads:
            g_chunk = jnp.exp(jnp.full((1, 1), SUPER, F32) * lg[h])
            st2_ref[h] = st2[h] * g_chunk + _dot((km[h] * jnp.exp(lg[h] * (SUPER - 1.0 - pos))).T.astype(BF16), vb[h])
        for h in heads:
            mu = jnp.mean(ob[h], axis=-1, keepdims=True)
            oc = ob[h] - mu
            oc = oc * lax.rsqrt(jnp.mean(oc * oc, axis=-1, keepdims=True) + NORM_EPS)
            gate = proj_ref[slot, :, pcols(COL_GB, h)]
            ob_ref[pl.ds(r0, SUPER), hcols[h]] = (oc * _silu(gate)).astype(ob_ref.dtype)

    for s in range(n_super):
        r0 = s * SUPER
        slot = s % 2
        tail = proj_ref[1 - slot, SUPER - SUBLANES:SUPER, 0:COL_ZA * LANES]
        nxt = x_ref[r0 + SUPER:r0 + 2 * SUPER, :] if s + 1 < n_super else xn_ref[...]
        project(nxt, 1 - slot)
        gdn_heads(r0, s, slot, tail)
        ret_heads(r0, slot)


def _mixer(x2, w_bf, hp, conv_w, norm_w, rot_c, rot_s, batch, seq):
    t, d = x2.shape
    ts = MIX_TS if seq % MIX_TS == 0 else seq
    nj = seq // ts
    per_blk = ts // SUPER
    assert per_blk % 2 == 0
    last_chunk = t // SUPER - 1
    const2 = lambda shape: pl.BlockSpec(shape, lambda b, j: (0, 0))
    tab = pl.BlockSpec((ts, LANES), lambda b, j: (j, 0))
    in_specs = [
        pl.BlockSpec(memory_space=pltpu.SMEM),
        pl.BlockSpec((ts, d), lambda b, j: (b * nj + j, 0)),
        pl.BlockSpec((SUPER, d), lambda b, j: (jnp.minimum((b * nj + j + 1) * per_blk, last_chunk), 0)),
        const2(w_bf.shape), const2(conv_w.shape), const2((1, GDN_DV)), tab, tab,
    ]
    out_spec = pl.BlockSpec((ts, HEAD_W), lambda b, j: (b * nj + j, 0))
    return pl.pallas_call(
        _mixer_kernel,
        grid=(batch, nj),
        in_specs=in_specs,
        out_specs=[out_spec, out_spec],
        out_shape=[jax.ShapeDtypeStruct((t, GDN_VW), BF16), jax.ShapeDtypeStruct((t, RET_VW), BF16)],
        scratch_shapes=[pltpu.VMEM((2, SUPER, PROJ_W), F32),
                        pltpu.VMEM((GDN_HEADS, GDN_DK, GDN_DV), F32),
                        pltpu.VMEM((RET_HEADS, LANES, RET_DV), F32)],
        compiler_params=_params(("arbitrary", "arbitrary")),
        name="mixer",
    )(hp, x2, x2, w_bf, conv_w, norm_w, rot_c, rot_s)


def _layer_norm(u, g, b):
    mu = jnp.mean(u, axis=-1, keepdims=True)
    uc = u - mu
    var = jnp.mean(uc * uc, axis=-1, keepdims=True)
    return uc * lax.rsqrt(var + LN_EPS) * g + b


def _post_mix_kernel(x_ref, oa_ref, ob_ref, wo_ref, g_ref, b_ref, wr_ref, rb_ref,
                     h_ref, hp_ref, idx_ref, wts_ref, cnt_ref):
    n_exp = wr_ref.shape[0]
    tm = x_ref.shape[0]
    per_grp = n_exp // N_GROUPS
    mix = _dot(oa_ref[...], wo_ref[:GDN_VW, :]) + _dot(ob_ref[...], wo_ref[GDN_VW:, :])
    hh = _layer_norm(DN_ALPHA * x_ref[...] + mix, g_ref[...], b_ref[...])
    h_ref[...] = hh
    _store_packed(hp_ref, hh)
    scores = jax.nn.sigmoid(_dot_nt(wr_ref[...], hh.astype(BF16)))
    choice = scores + rb_ref[...]
    big = float(n_exp)
    io_g = _iota((per_grp, tm), 0).astype(F32)
    grp = []
    for g in range(N_GROUPS):
        blk = choice[g * per_grp:(g + 1) * per_grp, :]
        m1 = jnp.max(blk, axis=0, keepdims=True)
        i1 = jnp.min(jnp.where(blk == m1, io_g, big), axis=0, keepdims=True)
        m2 = jnp.max(jnp.where(io_g == i1, NEG_INF, blk), axis=0, keepdims=True)
        grp.append(m1 + m2)
    gsc = jnp.concatenate(grp, axis=0)
    io8 = _iota((N_GROUPS, tm), 0).astype(F32)
    gsel = jnp.zeros((N_GROUPS, tm), F32)
    for _ in range(TOPK_GROUPS):
        m = jnp.max(gsc, axis=0, keepdims=True)
        i = jnp.min(jnp.where(gsc == m, io8, big), axis=0, keepdims=True)
        hit = io8 == i
        gsel = jnp.where(hit, 1.0, gsel)
        gsc = jnp.where(hit, NEG_INF, gsc)
    masked = jnp.concatenate(
        [jnp.where(gsel[g:g + 1, :] > 0.0, choice[g * per_grp:(g + 1) * per_grp, :], NEG_INF)
         for g in range(N_GROUPS)], axis=0)
    io_e = _iota((n_exp, tm), 0).astype(F32)
    allowed = masked
    ids, ws = [], []
    for _ in range(TOP_K):
        m = jnp.max(masked, axis=0, keepdims=True)
        i = jnp.min(jnp.where(masked == m, io_e, big), axis=0, keepdims=True)
        hit = io_e == i
        ws.append(jnp.sum(jnp.where(hit, scores, 0.0), axis=0, keepdims=True))
        ids.append(i)
        masked = jnp.where(hit, NEG_INF, masked)
    sel = jnp.where(masked == NEG_INF, jnp.where(allowed == NEG_INF, 0.0, 1.0), 0.0)
    w = jnp.concatenate(ws, axis=0)
    wts_ref[...] = w / jnp.sum(w, axis=0, keepdims=True) * ROUTED_SCALE
    idx_ref[...] = jnp.concatenate(ids, axis=0).astype(jnp.int32)

    @pl.when(pl.program_id(0) == 0)
    def _():
        cnt_ref[...] = jnp.zeros_like(cnt_ref)

    cnt_ref[...] += jnp.sum(sel, axis=1, keepdims=True)


def _post_mix(x2, oa, ob, wo_bf, g1, b1, wr_t, rbias, tm):
    t, d = x2.shape
    n_exp = wr_t.shape[0]
    row = lambda w: pl.BlockSpec((tm, w), lambda i: (i, 0))
    const = lambda shape: pl.BlockSpec(shape, lambda i: (0, 0))
    tok = pl.BlockSpec((TOP_K, tm), lambda i: (0, i))
    return pl.pallas_call(
        _post_mix_kernel,
        grid=(t // tm,),
        in_specs=[row(d), row(GDN_VW), row(RET_VW), const(wo_bf.shape), const((1, d)), const((1, d)),
                  const(wr_t.shape), const((n_exp, 1))],
        out_specs=[row(d),
                   pl.BlockSpec((tm * PACK_ROWS, LANES), lambda i: (i, 0)), tok, tok, const((n_exp, 1))],
        out_shape=[jax.ShapeDtypeStruct((t, d), F32),
                   jax.ShapeDtypeStruct((t * PACK_ROWS, LANES), jnp.uint32),
                   jax.ShapeDtypeStruct((TOP_K, t), jnp.int32),
                   jax.ShapeDtypeStruct((TOP_K, t), F32),
                   jax.ShapeDtypeStruct((n_exp, 1), F32)],
        compiler_params=_params(("arbitrary",)),
        name="post_mix",
    )(x2, oa, ob, wo_bf, g1, b1, wr_t, rbias)


def _dest_kernel(idx_ref, base_ref, dest_ref, carry_ref):
    n_exp = base_ref.shape[0]
    tm = idx_ref.shape[1]

    @pl.when(pl.program_id(0) == 0)
    def _():
        carry_ref[...] = jnp.zeros_like(carry_ref)

    io_e = _iota((n_exp, tm), 0)
    idx = idx_ref[...]
    hits = [io_e == idx[k:k + 1, :] for k in range(TOP_K)]
    sel = jnp.zeros((n_exp, tm), F32)
    for hit in hits:
        sel = jnp.where(hit, 1.0, sel)
    before = jnp.where(_iota((tm, tm), 0) < _iota((tm, tm), 1), 1.0, 0.0).astype(BF16)
    rank = _dot(sel.astype(BF16), before)
    tot = base_ref[...] + carry_ref[...] + rank
    dest = [jnp.sum(jnp.where(hit, tot, 0.0), axis=0, keepdims=True) for hit in hits]
    dest_ref[...] = jnp.concatenate(dest, axis=0).astype(jnp.int32) * PACK_ROWS
    carry_ref[...] += jnp.sum(sel, axis=1, keepdims=True)


def _dest(idx_t, base, tm):
    t = idx_t.shape[1]
    n_exp = base.shape[0]
    tok = pl.BlockSpec((TOP_K, tm), lambda i: (0, i))
    return pl.pallas_call(
        _dest_kernel,
        grid=(t // tm,),
        in_specs=[tok, pl.BlockSpec((n_exp, 1), lambda i: (0, 0))],
        out_specs=tok,
        out_shape=jax.ShapeDtypeStruct((TOP_K, t), jnp.int32),
        scratch_shapes=[pltpu.VMEM((n_exp, 1), F32)],
        compiler_params=_params(("arbitrary",)),
        name="dest",
    )(idx_t, base)


def _dispatch_kernel(dest_ref, h_ref, xs_ref, zero_ref, sem, zsem):
    tm = h_ref.shape[0] // PACK_ROWS
    last = pl.num_programs(0) - 1
    pad_row = xs_ref.shape[0] - zero_ref.shape[0]

    @pl.when(pl.program_id(0) == last)
    def _():
        zero_ref[...] = jnp.zeros_like(zero_ref)
        _tile_copy(zero_ref, 0, xs_ref, pad_row, zsem, EXPERT_CHUNK, PACK_ROWS).start()

    def start(t, c):
        for k in range(TOP_K):
            _tile_copy(h_ref, t * PACK_ROWS, xs_ref, dest_ref[t * TOP_K + k], sem, 1, PACK_ROWS).start(priority=k % 2)
        return c

    lax.fori_loop(0, tm, start, 0)

    for k in range(TOP_K):
        _tile_copy(h_ref, 0, xs_ref, 0, sem, tm, PACK_ROWS).wait()

    @pl.when(pl.program_id(0) == last)
    def _():
        _tile_copy(zero_ref, 0, xs_ref, pad_row, zsem, EXPERT_CHUNK, PACK_ROWS).wait()


def _dispatch(dest_t, h_pk, tm):
    t = dest_t.shape[0] // TOP_K
    n_rows = (t * TOP_K + EXPERT_CHUNK) * PACK_ROWS
    return pl.pallas_call(
        _dispatch_kernel,
        grid=(t // tm,),
        in_specs=[pl.BlockSpec((tm * TOP_K,), lambda i: (i,), memory_space=pltpu.SMEM),
                  pl.BlockSpec((tm * PACK_ROWS, LANES), lambda i: (i, 0))],
        out_specs=pl.BlockSpec(memory_space=pl.ANY),
        out_shape=jax.ShapeDtypeStruct((n_rows, LANES), jnp.uint32),
        scratch_shapes=[pltpu.VMEM((EXPERT_CHUNK * PACK_ROWS, LANES), jnp.uint32),
                        pltpu.SemaphoreType.DMA(()), pltpu.SemaphoreType.DMA(())],
        compiler_params=_params(("arbitrary",)),
        name="dispatch",
    )(dest_t, h_pk)


def _chunk_pieces(n):
    pieces = [(j < n // EXPERT_PIECE, j * EXPERT_PIECE, EXPERT_PIECE) for j in range(EXPERT_CHUNK // EXPERT_PIECE)]
    size = EXPERT_PIECE // 2
    while size >= 1:
        shift = int(math.log2(size)) + 1
        pieces.append(((n & size) != 0, (n >> shift) << shift, size))
        size //= 2
    return pieces


ST_CHUNKS = 0
ST_ISSUED = 1
ST_CUR_E = 2
ST_CUR_I = 3
ST_WRITE = 4
N_XBUF = 4
N_OBUF = 3


def _expert_kernel(start_ref, count_ref, wg_ref, wu_ref, wd_ref, xs_ref, eo_ref,
                   xbuf, obuf, st_ref, rsem, wsem):
    e = pl.program_id(0)
    n_exp = pl.num_programs(0)
    ch = EXPERT_CHUNK
    s0 = start_ref[e]
    cnt = count_ref[e]
    n_ch = (cnt + ch - 1) // ch

    def read_parts(first_slot, buf):
        return [_tile_copy(xs_ref, (first_slot + j * EXPERT_PIECE) * PACK_ROWS, xbuf.at[buf],
                           j * EXPERT_PIECE * PACK_ROWS, rsem.at[buf], EXPERT_PIECE, PACK_ROWS)
                for j in range(ch // EXPERT_PIECE)]

    def read_next():
        def exhausted(c):
            cnt_c = count_ref[jnp.minimum(c[0], n_exp - 1)]
            return jnp.logical_and(c[0] < n_exp, c[1] * ch >= cnt_c)

        ce, ci = lax.while_loop(exhausted, lambda c: (c[0] + 1, jnp.int32(0)),
                                (st_ref[ST_CUR_E], st_ref[ST_CUR_I]))

        @pl.when(ce < n_exp)
        def _():
            first_slot = start_ref[jnp.minimum(ce, n_exp - 1)] + ci * ch
            buf = lax.rem(st_ref[ST_ISSUED], N_XBUF)
            for j, cp in enumerate(read_parts(first_slot, buf)):
                cp.start(priority=j % 2)
            st_ref[ST_ISSUED] = st_ref[ST_ISSUED] + 1

        st_ref[ST_CUR_E] = ce
        st_ref[ST_CUR_I] = ci + 1

    def wait_read(buf):
        for cp in read_parts(0, buf):
            cp.wait()

    def out_pieces(buf, first_slot, n, wait):
        for j, (pred, off, size) in enumerate(_chunk_pieces(n)):
            @pl.when(pred)
            def _(j=j, off=off, size=size):
                cp = _tile_copy(obuf.at[buf], off * PACK_ROWS, eo_ref, (first_slot + off) * PACK_ROWS,
                                wsem.at[buf], size, PACK_ROWS)
                cp.wait() if wait else cp.start(priority=j % 2)

    def drain(buf):
        out_pieces(buf, st_ref[ST_WRITE + 2 * buf], st_ref[ST_WRITE + 2 * buf + 1], True)
        st_ref[ST_WRITE + 2 * buf + 1] = 0

    @pl.when(e == 0)
    def _():
        for i in range(st_ref.shape[0]):
            st_ref[i] = 0
        for _ in range(N_XBUF - 1):
            read_next()

    def body(i, c):
        g = st_ref[ST_CHUNKS]
        buf = lax.rem(g, N_XBUF)
        wait_read(buf)
        read_next()
        obf = lax.rem(g, N_OBUF)
        drain(obf)
        first_slot = s0 + i * ch
        n = jnp.minimum(ch, cnt - i * ch)

        def swiglu(rows):
            xb = _load_packed(xbuf, rows, lead=(buf,), dtype=BF16)
            hid = _silu(_dot(xb, wg_ref[0].astype(BF16))) * _dot(xb, wu_ref[0].astype(BF16))
            _store_packed(obuf, _dot(hid.astype(BF16), wd_ref[0].astype(BF16)), lead=(obf,))

        pl.when(n > ch // 2)(lambda: swiglu(ch))
        pl.when((n <= ch // 2) & (n > ch // 4))(lambda: swiglu(ch // 2))
        pl.when(n <= ch // 4)(lambda: swiglu(ch // 4))
        out_pieces(obf, first_slot, n, False)
        st_ref[ST_WRITE + 2 * obf] = first_slot
        st_ref[ST_WRITE + 2 * obf + 1] = n
        st_ref[ST_CHUNKS] = g + 1
        return c

    lax.fori_loop(0, n_ch, body, 0)

    @pl.when(e == n_exp - 1)
    def _():
        for b in range(N_OBUF):
            drain(b)


def _experts(starts, counts, xs, wg, wu, wd):
    n_exp, d, d_e = wg.shape
    n_slot = xs.shape[0] // PACK_ROWS - EXPERT_CHUNK
    rows = EXPERT_CHUNK * PACK_ROWS
    grid_spec = pltpu.PrefetchScalarGridSpec(
        num_scalar_prefetch=2,
        grid=(n_exp,),
        in_specs=[pl.BlockSpec((1, d, d_e), lambda e, st, ct: (e, 0, 0)),
                  pl.BlockSpec((1, d, d_e), lambda e, st, ct: (e, 0, 0)),
                  pl.BlockSpec((1, d_e, d), lambda e, st, ct: (e, 0, 0)),
                  pl.BlockSpec(memory_space=pl.ANY)],
        out_specs=pl.BlockSpec(memory_space=pl.ANY),
        scratch_shapes=[pltpu.VMEM((N_XBUF, rows, LANES), jnp.uint32), pltpu.VMEM((N_OBUF, rows, LANES), jnp.uint32),
                        pltpu.SMEM((ST_WRITE + 2 * N_OBUF,), jnp.int32),
                        pltpu.SemaphoreType.DMA((N_XBUF,)), pltpu.SemaphoreType.DMA((N_OBUF,))],
    )
    return pl.pallas_call(
        _expert_kernel,
        grid_spec=grid_spec,
        out_shape=jax.ShapeDtypeStruct((n_slot * PACK_ROWS, LANES), jnp.uint32),
        compiler_params=_params(("arbitrary",)),
        name="experts",
    )(starts, counts, wg, wu, wd, xs)


def _combine_kernel(dest_ref, dnext_ref, wts_ref, h_ref, eo_ref, wgs_ref, wus_ref, wds_ref, g_ref, b_ref,
                    out_ref, gbuf, sem):
    i = pl.program_id(0)
    tm = out_ref.shape[0]
    buf = lax.rem(i, 2)

    def gather(d_ref, bf):
        def start(t, c):
            for k in range(TOP_K):
                _tile_copy(eo_ref, d_ref[t * TOP_K + k], gbuf.at[bf, k], t * PACK_ROWS, sem.at[bf],
                           1, PACK_ROWS).start(priority=k % 2)
            return c

        lax.fori_loop(0, tm, start, 0)

    @pl.when(i == 0)
    def _():
        gather(dest_ref, 0)

    def step(bf):
        for t in range(tm):
            for k in range(TOP_K):
                _tile_copy(eo_ref, dnext_ref[t * TOP_K + k], gbuf.at[1 - bf, k], t * PACK_ROWS, sem.at[1 - bf],
                           1, PACK_ROWS).start(priority=k % 2)
        hh = h_ref[...]
        hb = hh.astype(BF16)
        hid = _silu(_dot(hb, wgs_ref[...])) * _dot(hb, wus_ref[...])
        acc = DN_ALPHA * hh + _dot(hid.astype(BF16), wds_ref[...])
        w_t = jnp.concatenate([wts_ref[...], jnp.zeros((LANES - TOP_K, tm), F32)], axis=0).T
        for k in range(TOP_K):
            _tile_copy(eo_ref, 0, gbuf.at[bf, k], 0, sem.at[bf], tm, PACK_ROWS).wait()
        for k in range(TOP_K):
            acc = acc + _load_packed(gbuf, tm, lead=(bf, k)) * w_t[:, k:k + 1]
        out_ref[...] = _layer_norm(acc, g_ref[...], b_ref[...])

        @pl.when(i == pl.num_programs(0) - 1)
        def _():
            for k in range(TOP_K):
                _tile_copy(eo_ref, 0, gbuf.at[1 - bf, k], 0, sem.at[1 - bf], tm, PACK_ROWS).wait()

    for bf in range(2):
        pl.when(buf == bf)(lambda bf=bf: step(bf))


def _combine(dest_t, wts_t, h_tt, eo, wgs, wus, wds, g2, b2, tm):
    t = dest_t.shape[0] // TOP_K
    d = wgs.shape[0]
    const = lambda shape: pl.BlockSpec(shape, lambda i: (0, 0))
    n_tiles = t // tm
    return pl.pallas_call(
        _combine_kernel,
        grid=(n_tiles,),
        in_specs=[pl.BlockSpec((tm * TOP_K,), lambda i: (i,), memory_space=pltpu.SMEM),
                  pl.BlockSpec((tm * TOP_K,), lambda i: (jnp.minimum(i + 1, n_tiles - 1),), memory_space=pltpu.SMEM),
                  pl.BlockSpec((TOP_K, tm), lambda i: (0, i)),
                  pl.BlockSpec((tm, d), lambda i: (i, 0)),
                  pl.BlockSpec(memory_space=pl.ANY),
                  const(wgs.shape), const(wus.shape), const(wds.shape), const((1, d)), const((1, d))],
        out_specs=pl.BlockSpec((tm, d), lambda i: (i, 0)),
        out_shape=jax.ShapeDtypeStruct((t, d), F32),
        scratch_shapes=[pltpu.VMEM((2, TOP_K, tm * PACK_ROWS, LANES), jnp.uint32), pltpu.SemaphoreType.DMA((2,))],
        compiler_params=_params(("arbitrary",)),
        name="combine",
    )(dest_t, dest_t, wts_t, h_tt, eo, wgs, wus, wds, g2, b2)


def _regroup_w_in(w):
    d = w.shape[0]
    splits = np.cumsum([GDN_QK, GDN_QK, GDN_VW, GDN_VW, GDN_HEADS, GDN_HEADS, RET_QK, RET_QK, RET_VW])
    qa, ka, va, za, ba, aa, qb, kb, vb, gb = jnp.split(w, splits.tolist(), axis=1)
    ba_blk = jnp.concatenate([ba, aa, jnp.zeros((d, LANES - 2 * GDN_HEADS), w.dtype)], axis=1)
    qb = qb.reshape(d, RET_HEADS, RET_DK // 2, 2)
    kb = kb.reshape(d, RET_HEADS, RET_DK // 2, 2)
    qkb = jnp.concatenate([qb[..., 0], qb[..., 1], kb[..., 0], kb[..., 1]], axis=-1).reshape(d, RET_HEADS * LANES)
    return jnp.concatenate([qa, ka, va, za, qkb, vb, gb, ba_blk], axis=1).astype(BF16)


def _rotary_tables(seq):
    inv = 1.0 / (ROPE_BASE ** jnp.linspace(0.0, 1.0, RET_DK // 2, dtype=F32))
    ang = jnp.arange(seq, dtype=F32)[:, None] * inv[None, :]
    c, s = jnp.cos(ang), jnp.sin(ang)
    ks = RET_DK ** -0.5
    rot_c = jnp.concatenate([c, c, c * ks, c * ks], axis=1)
    rot_s = jnp.concatenate([-s, s, -s * ks, s * ks], axis=1)
    return rot_c, rot_s


def _tile(n, pref):
    return pref if n % pref == 0 else n


def kernel(x, w_in, gdn_conv_w, gdn_a_log, gdn_dt_bias, gdn_norm_w, w_out, ln1_g, ln1_b, w_router, router_bias,
           w_gate_e, w_up_e, w_down_e, w_gate_s, w_up_s, w_down_s, ln2_g, ln2_b):
    batch, seq, d = x.shape
    t = batch * seq
    n_exp = w_router.shape[-1]
    hcur = x.reshape(t, d)
    for l in range(DEPTH):
        w_bf = _regroup_w_in(w_in[l])
        log_gamma = jnp.log(1.0 - 2.0 ** (-5.0 - jnp.arange(RET_HEADS, dtype=F32)))
        hp = jnp.stack([-jnp.exp(gdn_a_log[l].astype(F32)), gdn_dt_bias[l].astype(F32), log_gamma])
        rot_c, rot_s = _rotary_tables(seq)
        oa, ob = _mixer(hcur, w_bf, hp, gdn_conv_w[l].astype(F32), gdn_norm_w[l].reshape(1, -1).astype(F32),
                        rot_c, rot_s, batch, seq)
        hh, h_pk, idx_t, wts_t, counts = _post_mix(
            hcur, oa, ob, w_out[l].astype(BF16), ln1_g[l].reshape(1, d), ln1_b[l].reshape(1, d),
            w_router[l].T.astype(BF16), router_bias[l].reshape(n_exp, 1).astype(F32), _tile(t, POST_MIX_TM))
        base = jnp.cumsum(counts, axis=0) - counts
        dest_t = _dest(idx_t, base, _tile(t, DEST_TM)).T.reshape(t * TOP_K)
        xs = _dispatch(dest_t, h_pk, _tile(t, DISPATCH_TM))
        eo = _experts(base.reshape(n_exp).astype(jnp.int32), counts.reshape(n_exp).astype(jnp.int32),
                      xs, w_gate_e[l], w_up_e[l], w_down_e[l])
        hcur = _combine(dest_t, wts_t, hh, eo, w_gate_s[l].astype(BF16), w_up_s[l].astype(BF16),
                        w_down_s[l].astype(BF16), ln2_g[l].reshape(1, d), ln2_b[l].reshape(1, d), _tile(t, COMBINE_TM))
    return hcur.reshape(batch, seq, d)
```
